```python
import math
import jax, jax.numpy as jnp
from jax import lax
import numpy as np

D_MODEL = 1024
BATCH = 4
SEQ = 8192
DEPTH = 1

PLE_DIM = 256
EPS = 1e-6
BLOCK = 128
NEG = -1e30

A_HEADS = 16
A_KV_HEADS = 2
A_HEAD_DIM = 64
A_WIDTH = A_HEADS * A_HEAD_DIM
A_KV_WIDTH = A_KV_HEADS * A_HEAD_DIM
WINDOW = 128
N_BUCKETS = 32
MAX_DISTANCE = 128

B_HEADS = 16
B_Q_RANK = 256
B_KV_RANK = 128
B_NOPE = 64
B_ROPE = 32
B_QK_DIM = B_NOPE + B_ROPE
B_VDIM = 64
B_WIDTH = B_HEADS * B_VDIM
ROPE_THETA = 10000.0

SPLIT_SIZES = (A_WIDTH, A_KV_WIDTH, A_KV_WIDTH, A_WIDTH,
               B_Q_RANK, B_KV_RANK, B_ROPE, B_WIDTH,
               D_MODEL, D_MODEL)
D_IN = 2 * A_WIDTH + 2 * A_KV_WIDTH + B_Q_RANK + B_KV_RANK + B_ROPE + B_WIDTH + 2 * D_MODEL

kernel_name = "hybrid_swa_sink_mla_gated_block"


def _rms(x, g):
    x32 = x.astype(jnp.float32)
    y = x32 * lax.rsqrt(jnp.mean(x32 * x32, axis=-1, keepdims=True) + EPS)
    return (y * g.astype(jnp.float32)).astype(x.dtype)


def _split(z):
    offsets = [int(o) for o in np.cumsum(SPLIT_SIZES)[:-1]]
    return jnp.split(z, offsets, axis=-1)


def _t5_bucket(dist):
    max_exact = N_BUCKETS // 2
    d = jnp.maximum(dist, 1).astype(jnp.float32)
    large = max_exact + (jnp.log(d / max_exact) / math.log(MAX_DISTANCE / max_exact)
                         * (N_BUCKETS - max_exact)).astype(jnp.int32)
    large = jnp.minimum(large, N_BUCKETS - 1)
    return jnp.where(dist < max_exact, dist, large)


def _rope_tables(positions, dim):
    inv_freq = ROPE_THETA ** (-jnp.arange(0, dim, 2, dtype=jnp.float32) / dim)
    ang = positions.astype(jnp.float32)[..., None] * inv_freq
    return jnp.cos(ang), jnp.sin(ang)


def _rope(x, cos, sin):
    cos = cos.astype(x.dtype)
    sin = sin.astype(x.dtype)
    x1, x2 = jnp.split(x, 2, axis=-1)
    return jnp.concatenate([x1 * cos - x2 * sin, x2 * cos + x1 * sin], axis=-1)


def _swa_branch(q, k, v, rel_bias, sinks, qn_g, kn_g):
    b, s_len, _ = q.shape
    nb = s_len // BLOCK
    grp = A_HEADS // A_KV_HEADS
    q = _rms(q.reshape(b, nb, BLOCK, A_KV_HEADS, grp, A_HEAD_DIM), qn_g)
    k = _rms(k.reshape(b, nb, BLOCK, A_KV_HEADS, A_HEAD_DIM), kn_g)
    v = v.reshape(b, nb, BLOCK, A_KV_HEADS, A_HEAD_DIM)
    pad = jnp.zeros_like(k[:, :1])
    k_band = jnp.concatenate([jnp.concatenate([pad, k[:, :-1]], axis=1), k], axis=2)
    v_band = jnp.concatenate([jnp.concatenate([pad, v[:, :-1]], axis=1), v], axis=2)
    scale = A_HEAD_DIM ** -0.5
    s = jnp.einsum('bnqkgd,bnckd->bnkgqc', q, k_band).astype(jnp.float32) * scale
    dist = (jnp.arange(BLOCK)[:, None] + BLOCK) - jnp.arange(2 * BLOCK)[None, :]
    valid = (dist >= 0) & (dist < WINDOW)
    bias = rel_bias.astype(jnp.float32)[_t5_bucket(jnp.maximum(dist, 0))]
    bias = jnp.transpose(bias, (2, 0, 1)).reshape(A_KV_HEADS, grp, BLOCK, 2 * BLOCK)
    first_pad = (jnp.arange(nb)[:, None, None] == 0) & (jnp.arange(2 * BLOCK)[None, None, :] < BLOCK)
    mask = valid[None] & ~first_pad
    s = jnp.where(mask[None, :, None, None], s + bias, NEG)
    sink = sinks.astype(jnp.float32).reshape(1, 1, A_KV_HEADS, grp, 1, 1)
    m = jnp.maximum(jnp.max(s, axis=-1, keepdims=True), sink)
    e = jnp.exp(s - m)
    probs = e / (jnp.sum(e, axis=-1, keepdims=True) + jnp.exp(sink - m))
    o = jnp.einsum('bnkgqc,bnckd->bnqkgd', probs.astype(v.dtype), v_band)
    return o.reshape(b, s_len, A_WIDTH)


def _mla_branch(c_q, c_kv, k_r, positions, cq_g, w_uq, ckv_g, w_uk, w_uv, qn_g, kn_g, krn_g):
    b, s_len, _ = c_q.shape
    nb = s_len // BLOCK
    cos, sin = _rope_tables(positions, B_ROPE)
    q = (_rms(c_q, cq_g) @ w_uq).reshape(b, s_len, B_HEADS, B_QK_DIM)
    q = _rms(q, qn_g)
    q = jnp.concatenate([q[..., :B_NOPE], _rope(q[..., B_NOPE:], cos[:, :, None], sin[:, :, None])], axis=-1)
    ckv = _rms(c_kv, ckv_g)
    k_nope = _rms((ckv @ w_uk).reshape(b, s_len, B_HEADS, B_NOPE), kn_g)
    v = (ckv @ w_uv).reshape(b, s_len, B_HEADS, B_VDIM)
    k_rope = _rope(_rms(k_r, krn_g), cos, sin)
    scale = B_QK_DIM ** -0.5
    q_blocks = jnp.transpose(q.reshape(b, nb, BLOCK, B_HEADS, B_QK_DIM), (1, 0, 2, 3, 4))
    k_pos = jnp.arange(s_len)

    def one_block(args):
        q_blk, idx = args
        s = (jnp.einsum('bqhd,bkhd->bhqk', q_blk[..., :B_NOPE], k_nope)
             + jnp.einsum('bqhr,bkr->bhqk', q_blk[..., B_NOPE:], k_rope)).astype(jnp.float32) * scale
        q_pos = idx * BLOCK + jnp.arange(BLOCK)
        s = jnp.where(k_pos[None, :] <= q_pos[:, None], s, NEG)
        probs = jax.nn.softmax(s, axis=-1)
        return jnp.einsum('bhqk,bkhd->bqhd', probs.astype(v.dtype), v)

    o = lax.map(one_block, (q_blocks, jnp.arange(nb)))
    return jnp.transpose(o, (1, 0, 2, 3, 4)).reshape(b, s_len, B_WIDTH)


def setup_inputs(seed: int = 0) -> dict:
    key = jax.random.key(seed)
    ks = jax.random.split(key, 24)
    f32 = jnp.float32

    def w(k, shape, fan_in):
        return jax.random.normal(k, shape, f32) * fan_in ** -0.5

    def gain(k, shape):
        return 1.0 + 0.1 * jax.random.normal(k, shape, f32)

    return {
        "x": jax.random.normal(ks[0], (BATCH, SEQ, D_MODEL), f32),
        "p": jax.random.normal(ks[1], (DEPTH, BATCH, SEQ, PLE_DIM), f32),
        "positions": jnp.broadcast_to(jnp.arange(SEQ, dtype=jnp.int32)[None], (BATCH, SEQ)),
        "norm_g": gain(ks[2], (DEPTH, D_MODEL)),
        "w_in": w(ks[3], (DEPTH, D_MODEL, D_IN), D_MODEL),
        "a_q_norm": gain(ks[4], (DEPTH, A_HEAD_DIM)),
        "a_k_norm": gain(ks[5], (DEPTH, A_HEAD_DIM)),
        "a_sinks": 0.5 * jax.random.normal(ks[6], (DEPTH, A_HEADS), f32),
        "rel_bias": 0.5 * jax.random.normal(ks[7], (N_BUCKETS, A_HEADS), f32),
        "w_o_a": w(ks[8], (DEPTH, A_WIDTH, D_MODEL), A_WIDTH),
        "b_cq_norm": gain(ks[9], (DEPTH, B_Q_RANK)),
        "w_uq": w(ks[10], (DEPTH, B_Q_RANK, B_HEADS * B_QK_DIM), B_Q_RANK),
        "b_ckv_norm": gain(ks[11], (DEPTH, B_KV_RANK)),
        "w_uk": w(ks[12], (DEPTH, B_KV_RANK, B_HEADS * B_NOPE), B_KV_RANK),
        "w_uv": w(ks[13], (DEPTH, B_KV_RANK, B_HEADS * B_VDIM), B_KV_RANK),
        "b_q_norm": gain(ks[14], (DEPTH, B_QK_DIM)),
        "b_k_norm": gain(ks[15], (DEPTH, B_NOPE)),
        "b_kr_norm": gain(ks[16], (DEPTH, B_ROPE)),
        "w_o_b": w(ks[17], (DEPTH, B_WIDTH, D_MODEL), B_WIDTH),
        "w_out": w(ks[18], (DEPTH, D_MODEL, D_MODEL), D_MODEL),
        "ple_norm_g": gain(ks[19], (DEPTH, D_MODEL)),
        "w_ple_gate": w(ks[20], (DEPTH, D_MODEL, D_MODEL), D_MODEL),
        "w_ple_proj": w(ks[21], (DEPTH, PLE_DIM, D_MODEL), PLE_DIM),
        "ple_post_g": gain(ks[22], (DEPTH, D_MODEL)),
    }


def reference(x, p, positions, norm_g, w_in, a_q_norm, a_k_norm, a_sinks, rel_bias, w_o_a,
              b_cq_norm, w_uq, b_ckv_norm, w_uk, w_uv, b_q_norm, b_k_norm, b_kr_norm, w_o_b,
              w_out, ple_norm_g, w_ple_gate, w_ple_proj, ple_post_g):
    for i in range(DEPTH):
        h = _rms(x, norm_g[i])
        z = h @ w_in[i]
        q_a, k_a, v_a, z_a, c_q, c_kv, k_r, z_b, g_a, g_b = _split(z)
        y_a = _swa_branch(q_a, k_a, v_a, rel_bias, a_sinks[i], a_q_norm[i], a_k_norm[i])
        y_a = (y_a * jax.nn.silu(z_a)) @ w_o_a[i]
        y_b = _mla_branch(c_q, c_kv, k_r, positions, b_cq_norm[i], w_uq[i], b_ckv_norm[i],
                          w_uk[i], w_uv[i], b_q_norm[i], b_k_norm[i], b_kr_norm[i])
        y_b = (y_b * jax.nn.silu(z_b)) @ w_o_b[i]
        merged = jax.nn.sigmoid(g_a) * y_a + jax.nn.sigmoid(g_b) * y_b
        x = x + merged @ w_out[i]
        gate = jax.nn.sigmoid(_rms(x, ple_norm_g[i]) @ w_ple_gate[i])
        emb = _rms(p[i] @ w_ple_proj[i], ple_post_g[i])
        x = x + gate * emb
    return x
```

```python
import functools
import math

import numpy as np
import jax
import jax.numpy as jnp
from jax import lax
from jax.experimental import pallas as pl
from jax.experimental.pallas import tpu as pltpu

F32 = jnp.float32
BF16 = jnp.bfloat16

D_MODEL = 1024
PLE_DIM = 256
EPS = 1e-6
NEG = -1e30

A_HEADS = 16
A_KV_HEADS = 2
A_HEAD_DIM = 64
A_WIDTH = A_HEADS * A_HEAD_DIM
A_KV_WIDTH = A_KV_HEADS * A_HEAD_DIM
A_GROUP = A_HEADS // A_KV_HEADS
WINDOW = 128
A_BLOCK = 128
N_BUCKETS = 32
MAX_DISTANCE = 128

B_HEADS = 16
B_Q_RANK = 256
B_KV_RANK = 128
B_NOPE = 64
B_ROPE = 32
B_QK_DIM = B_NOPE + B_ROPE
B_VDIM = 64
B_WIDTH = B_HEADS * B_VDIM
ROPE_THETA = 10000.0

SPLIT_SIZES = (A_WIDTH, A_KV_WIDTH, A_KV_WIDTH, A_WIDTH,
               B_Q_RANK, B_KV_RANK, B_ROPE, B_WIDTH,
               D_MODEL, D_MODEL)

LANES = 128
QK_PAD = 128
V_ROWS = 80
VMEM_LIMIT = 56 * 1024 * 1024

C_QA = 0
C_KVA = C_QA + A_WIDTH
C_ZA = C_KVA + 2 * A_KV_WIDTH
C_LAT = C_ZA + A_WIDTH
LAT_W = B_Q_RANK + B_KV_RANK + LANES
C_ZB = C_LAT + LAT_W
C_GA = C_ZB + B_WIDTH
C_GB = C_GA + D_MODEL
C_END = C_GB + D_MODEL


def _dot(a, b):
    return jnp.dot(a, b, preferred_element_type=F32)


def _dot_nt(a, b):
    return lax.dot_general(a, b, (((1,), (1,)), ((), ())), preferred_element_type=F32)


def _split_bf16(v):
    hi = v.astype(BF16)
    lo = (v - hi.astype(F32)).astype(BF16)
    return hi, lo


def _seg_rms(z, e, et, inv_width):
    hi, lo = _split_bf16(z * z)
    ss = _dot(hi, e) + _dot(lo, e)
    r = lax.rsqrt(ss * inv_width + EPS)
    r_hi, r_lo = _split_bf16(r)
    return z * (_dot(r_hi, et) + _dot(r_lo, et))


def _row_rms(v, width):
    return v * lax.rsqrt(jnp.sum(v * v, axis=-1, keepdims=True) * (1.0 / width) + EPS)


def _proj_kernel(x_ref, g_ref, w_ref, e_ref, et_ref, gq_ref, gk_ref,
                 qa_ref, ka_ref, va_ref, za_ref, lat_ref, zb_ref, ga_ref, gb_ref):
    x = x_ref[...]
    h = (_row_rms(x, D_MODEL) * g_ref[...]).astype(BF16)

    def proj(c0, c1):
        return _dot(h, w_ref[:, c0:c1])

    inv = 1.0 / A_HEAD_DIM
    q = proj(C_QA, C_KVA)
    qa_ref[...] = (_seg_rms(q, e_ref[...], et_ref[...], inv) * gq_ref[...]).astype(BF16)
    kv = proj(C_KVA, C_ZA)
    k = kv[:, :A_KV_WIDTH]
    ka_ref[...] = (_seg_rms(k, e_ref[:A_KV_WIDTH, :], et_ref[:, :A_KV_WIDTH], inv)
                   * gk_ref[...]).astype(BF16)
    va_ref[...] = kv[:, A_KV_WIDTH:].astype(BF16)
    za_ref[...] = proj(C_ZA, C_LAT).astype(BF16)
    lat_ref[...] = proj(C_LAT, C_ZB)
    zb_ref[...] = proj(C_ZB, C_GA).astype(BF16)
    ga_ref[...] = proj(C_GA, C_GB).astype(BF16)
    gb_ref[...] = proj(C_GB, C_END).astype(BF16)


def _proj(x2, g, w1, e, et, gq, gk, tm):
    t = x2.shape[0]
    row = lambda w: pl.BlockSpec((tm, w), lambda i: (i, 0))
    full = lambda a: pl.BlockSpec(a.shape, lambda i: (0,) * a.ndim)
    widths = (A_WIDTH, A_KV_WIDTH, A_KV_WIDTH, A_WIDTH, LAT_W, B_WIDTH, D_MODEL, D_MODEL)
    dtypes = (BF16, BF16, BF16, BF16, F32, BF16, BF16, BF16)
    return pl.pallas_call(
        _proj_kernel,
        grid=(t // tm,),
        in_specs=[row(D_MODEL), full(g), full(w1), full(e), full(et), full(gq), full(gk)],
        out_specs=[row(w) for w in widths],
        out_shape=[jax.ShapeDtypeStruct((t, w), d) for w, d in zip(widths, dtypes)],
        compiler_params=pltpu.CompilerParams(
            dimension_semantics=("arbitrary",), vmem_limit_bytes=VMEM_LIMIT),
        name="proj",
    )(x2, g, w1, e, et, gq, gk)


def _prep_kernel(lat_ref, posc_ref, posr_ref, gcq_ref, wuqt_ref, gqn_ref, gckv_ref, wuk_ref,
                 wuvt_ref, e_ref, et_ref, gkn_ref, gkr_ref, freqc_ref, freqr_ref,
                 qt_ref, k_ref, vt_ref):
    lat = lat_ref[...]
    tm = lat.shape[0]
    cq = (_row_rms(lat[:, :B_Q_RANK], B_Q_RANK) * gcq_ref[...]).astype(BF16)
    qt = _dot_nt(wuqt_ref[...], cq)
    ang_r = freqc_ref[...] * posr_ref[...].astype(F32)
    cos_r = jnp.cos(ang_r)
    sin_r = jnp.sin(ang_r)
    gqn = gqn_ref[...]
    half = B_ROPE // 2
    zeros_q = jnp.zeros((QK_PAD - B_QK_DIM, tm), BF16)
    for h in range(B_HEADS):
        blk = qt[h * B_QK_DIM:(h + 1) * B_QK_DIM, :]
        ms = jnp.sum(blk * blk, axis=0, keepdims=True) * (1.0 / B_QK_DIM)
        qn = blk * lax.rsqrt(ms + EPS) * gqn
        x1 = qn[B_NOPE:B_NOPE + half, :]
        x2 = qn[B_NOPE + half:, :]
        qt_ref[0, h, 0:B_NOPE, :] = qn[:B_NOPE, :].astype(BF16)
        qt_ref[0, h, B_NOPE:B_NOPE + half, :] = (x1 * cos_r - x2 * sin_r).astype(BF16)
        qt_ref[0, h, B_NOPE + half:B_QK_DIM, :] = (x2 * cos_r + x1 * sin_r).astype(BF16)
        qt_ref[0, h, B_QK_DIM:, :] = zeros_q

    ckv = (_row_rms(lat[:, B_Q_RANK:B_Q_RANK + B_KV_RANK], B_KV_RANK) * gckv_ref[...]).astype(BF16)
    kn = _seg_rms(_dot(ckv, wuk_ref[...]), e_ref[...], et_ref[...], 1.0 / B_NOPE) * gkn_ref[...]
    vt = _dot_nt(wuvt_ref[...], ckv)

    kr = lat[:, B_Q_RANK + B_KV_RANK:]
    kr = kr * lax.rsqrt(jnp.sum(kr * kr, axis=-1, keepdims=True) * (1.0 / B_ROPE) + EPS) * gkr_ref[...]
    ang_c = posc_ref[...].astype(F32) * freqr_ref[...]
    lane = lax.broadcasted_iota(jnp.int32, (tm, LANES), 1)
    rot = jnp.where(lane < half, -pltpu.roll(kr, LANES - half, 1),
                    jnp.where(lane < B_ROPE, pltpu.roll(kr, half, 1), 0.0))
    kr = kr * jnp.cos(ang_c) + rot * jnp.sin(ang_c)
    kr = pltpu.roll(kr, B_NOPE, 1)
    lo_half = lane < B_NOPE
    ones_rows = (lax.broadcasted_iota(jnp.int32, (V_ROWS - B_VDIM, tm), 0) == 0).astype(BF16)
    for h2 in range(B_HEADS // 2):
        pair = kn[:, h2 * LANES:(h2 + 1) * LANES]
        k_ref[0, 2 * h2] = jnp.where(lo_half, pair, kr).astype(BF16)
        k_ref[0, 2 * h2 + 1] = jnp.where(lo_half, pltpu.roll(pair, B_NOPE, 1), kr).astype(BF16)
    for h in range(B_HEADS):
        vt_ref[0, h, 0, 0:B_VDIM, :] = vt[h * B_VDIM:(h + 1) * B_VDIM, :].astype(BF16)
        vt_ref[0, h, 0, B_VDIM:, :] = ones_rows


def _prep(lat, pos_col, pos_row, gcq, wuqt, gqn, gckv, wuk, wuvt, e, et, gkn, gkr, freq_c, freq_r,
          batch, seq, tm):
    nt = seq // tm
    full = lambda a: pl.BlockSpec(a.shape, lambda b, i: (0,) * a.ndim)
    return pl.pallas_call(
        _prep_kernel,
        grid=(batch, nt),
        in_specs=[pl.BlockSpec((tm, LAT_W), lambda b, i: (b * nt + i, 0)),
                  pl.BlockSpec((tm, 1), lambda b, i: (b * nt + i, 0)),
                  pl.BlockSpec((1, tm), lambda b, i: (0, b * nt + i)),
                  full(gcq), full(wuqt), full(gqn), full(gckv), full(wuk), full(wuvt),
                  full(e), full(et), full(gkn), full(gkr), full(freq_c), full(freq_r)],
        out_specs=[pl.BlockSpec((1, B_HEADS, QK_PAD, tm), lambda b, i: (b, 0, 0, i)),
                   pl.BlockSpec((1, B_HEADS, tm, QK_PAD), lambda b, i: (b, 0, i, 0)),
                   pl.BlockSpec((1, B_HEADS, 1, V_ROWS, tm), lambda b, i: (b, 0, i, 0, 0))],
        out_shape=[jax.ShapeDtypeStruct((batch, B_HEADS, QK_PAD, seq), BF16),
                   jax.ShapeDtypeStruct((batch, B_HEADS, seq, QK_PAD), BF16),
                   jax.ShapeDtypeStruct((batch, B_HEADS, nt, V_ROWS, tm), BF16)],
        compiler_params=pltpu.CompilerParams(
            dimension_semantics=("arbitrary", "arbitrary"), vmem_limit_bytes=VMEM_LIMIT),
        name="prep",
    )(lat, pos_col, pos_row, gcq, wuqt, gqn, gckv, wuk, wuvt, e, et, gkn, gkr, freq_c, freq_r)


def _swa_kernel(sink_ref, q_ref, k_ref, v_ref, kp_ref, vp_ref, bias_ref, o_ref, *, nsub):
    first = pl.program_id(1) == 0
    col = lax.broadcasted_iota(jnp.int32, (A_BLOCK, 2 * A_BLOCK), 1)
    for sb in range(nsub):
        r0 = sb * A_BLOCK
        if sb == 0:
            kb = jnp.concatenate([kp_ref[0], k_ref[0, 0:A_BLOCK, :]], axis=0)
            vb = jnp.concatenate([vp_ref[0], v_ref[0, 0:A_BLOCK, :]], axis=0)
            pad = jnp.logical_and(first, col < A_BLOCK)
        else:
            kb = k_ref[0, r0 - A_BLOCK:r0 + A_BLOCK, :]
            vb = v_ref[0, r0 - A_BLOCK:r0 + A_BLOCK, :]
            pad = None
        for h in range(A_HEADS):
            g = h // A_GROUP
            qh = q_ref[0, r0:r0 + A_BLOCK, h * A_HEAD_DIM:(h + 1) * A_HEAD_DIM]
            kg = kb[:, g * A_HEAD_DIM:(g + 1) * A_HEAD_DIM]
            vg = vb[:, g * A_HEAD_DIM:(g + 1) * A_HEAD_DIM]
            s = _dot_nt(qh, kg) + bias_ref[h]
            if pad is not None:
                s = jnp.where(pad, NEG, s)
            sink = sink_ref[h]
            m = jnp.maximum(jnp.max(s, axis=-1, keepdims=True), sink)
            e = jnp.exp(s - m)
            den = jnp.sum(e, axis=-1, keepdims=True) + jnp.exp(sink - m)
            o = _dot(e.astype(BF16), vg) / den
            o_ref[0, r0:r0 + A_BLOCK, h * A_HEAD_DIM:(h + 1) * A_HEAD_DIM] = o.astype(BF16)


def _swa(sinks, qa, ka, va, bias, tq):
    batch, seq, _ = qa.shape
    nsub = tq // A_BLOCK
    prev = lambda b, i: (b, jnp.maximum(i * nsub - 1, 0), 0)
    return pl.pallas_call(
        functools.partial(_swa_kernel, nsub=nsub),
        grid=(batch, seq // tq),
        in_specs=[pl.BlockSpec(memory_space=pltpu.SMEM),
                  pl.BlockSpec((1, tq, A_WIDTH), lambda b, i: (b, i, 0)),
                  pl.BlockSpec((1, tq, A_KV_WIDTH), lambda b, i: (b, i, 0)),
                  pl.BlockSpec((1, tq, A_KV_WIDTH), lambda b, i: (b, i, 0)),
                  pl.BlockSpec((1, A_BLOCK, A_KV_WIDTH), prev),
                  pl.BlockSpec((1, A_BLOCK, A_KV_WIDTH), prev),
                  pl.BlockSpec(bias.shape, lambda b, i: (0, 0, 0))],
        out_specs=pl.BlockSpec((1, tq, A_WIDTH), lambda b, i: (b, i, 0)),
        out_shape=jax.ShapeDtypeStruct((batch, seq, A_WIDTH), BF16),
        compiler_params=pltpu.CompilerParams(
            dimension_semantics=("arbitrary", "arbitrary"), vmem_limit_bytes=VMEM_LIMIT),
        name="swa",
    )(sinks, qa, ka, va, ka, va, bias)


def _flash_kernel(qt_ref, k_ref, vt_ref, o_ref, m_sc, acc_sc, *, blk, nh):
    i = pl.program_id(2)
    m_sc[...] = jnp.full(m_sc.shape, NEG, F32)
    acc_sc[...] = jnp.zeros(acc_sc.shape, F32)

    def step(j, masked):
        for h in range(nh):
            kb = k_ref[0, h, pl.ds(pl.multiple_of(j * blk, blk), blk), :]
            s = _dot(kb, qt_ref[0, h])
            if masked:
                key = lax.broadcasted_iota(jnp.int32, (blk, blk), 0)
                qry = lax.broadcasted_iota(jnp.int32, (blk, blk), 1)
                s = jnp.where(key <= qry, s, NEG)
            m_old = m_sc[h]
            m_new = jnp.maximum(m_old, jnp.max(s, axis=0, keepdims=True))
            p = jnp.exp(s - m_new).astype(BF16)
            alpha = jnp.exp(m_old - m_new)
            m_sc[h] = m_new
            acc_sc[h] = acc_sc[h] * alpha + _dot(vt_ref[0, h, j], p)

    def body(j, carry):
        step(j, False)
        return carry

    lax.fori_loop(0, i, body, 0)
    step(i, True)

    outs = []
    for h in range(nh):
        acc = acc_sc[h]
        outs.append(acc[:B_VDIM, :] / acc[B_VDIM:B_VDIM + 1, :])
    o_ref[0] = jnp.concatenate(outs, axis=0).T.astype(BF16)


def _flash(qt, k, vt, blk, nh):
    batch, heads, _, seq = qt.shape
    nq = seq // blk
    return pl.pallas_call(
        functools.partial(_flash_kernel, blk=blk, nh=nh),
        grid=(batch, heads // nh, nq),
        in_specs=[pl.BlockSpec((1, nh, QK_PAD, blk), lambda b, g, i: (b, g, 0, i)),
                  pl.BlockSpec((1, nh, seq, QK_PAD), lambda b, g, i: (b, g, 0, 0)),
                  pl.BlockSpec((1, nh, nq, V_ROWS, blk), lambda b, g, i: (b, g, 0, 0, 0))],
        out_specs=pl.BlockSpec((1, blk, nh * B_VDIM), lambda b, g, i: (b, i, g)),
        out_shape=jax.ShapeDtypeStruct((batch, seq, B_WIDTH), BF16),
        scratch_shapes=[pltpu.VMEM((nh, 1, blk), F32), pltpu.VMEM((nh, V_ROWS, blk), F32)],
        compiler_params=pltpu.CompilerParams(
            dimension_semantics=("arbitrary", "arbitrary", "arbitrary"),
            vmem_limit_bytes=VMEM_LIMIT),
        name="flash",
    )(qt, k, vt)


def _final_kernel(x_ref, p_ref, oa_ref, za_ref, ob_ref, zb_ref, ga_ref, gb_ref,
                  woa_ref, wob_ref, wout_ref, gpl_ref, wpg_ref, wpp_ref, gpost_ref, out_ref):
    ya = _dot((oa_ref[...].astype(F32) * jax.nn.silu(za_ref[...].astype(F32))).astype(BF16), woa_ref[...])
    yb = _dot((ob_ref[...].astype(F32) * jax.nn.silu(zb_ref[...].astype(F32))).astype(BF16), wob_ref[...])
    merged = (jax.nn.sigmoid(ga_ref[...].astype(F32)) * ya
              + jax.nn.sigmoid(gb_ref[...].astype(F32)) * yb)
    x1 = x_ref[...] + _dot(merged.astype(BF16), wout_ref[...])
    gate = jax.nn.sigmoid(_dot((_row_rms(x1, D_MODEL) * gpl_ref[...]).astype(BF16), wpg_ref[...]))
    emb = _row_rms(_dot(p_ref[...].astype(BF16), wpp_ref[...]), D_MODEL) * gpost_ref[...]
    out_ref[...] = x1 + gate * emb


def _final(x2, p2, oa, za, ob, zb, ga, gb, woa, wob, wout, gpl, wpg, wpp, gpost, tm):
    t = x2.shape[0]
    row = lambda a: pl.BlockSpec((tm, a.shape[1]), lambda i: (i, 0))
    full = lambda a: pl.BlockSpec(a.shape, lambda i: (0,) * a.ndim)
    acts = (x2, p2, oa, za, ob, zb, ga, gb)
    consts = (woa, wob, wout, gpl, wpg, wpp, gpost)
    return pl.pallas_call(
        _final_kernel,
        grid=(t // tm,),
        in_specs=[row(a) for a in acts] + [full(a) for a in consts],
        out_specs=pl.BlockSpec((tm, D_MODEL), lambda i: (i, 0)),
        out_shape=jax.ShapeDtypeStruct((t, D_MODEL), F32),
        compiler_params=pltpu.CompilerParams(
            dimension_semantics=("arbitrary",), vmem_limit_bytes=VMEM_LIMIT),
        name="final",
    )(*acts, *consts)


def _t5_bucket(dist):
    max_exact = N_BUCKETS // 2
    d = jnp.maximum(dist, 1).astype(F32)
    large = max_exact + (jnp.log(d / max_exact) / math.log(MAX_DISTANCE / max_exact)
                         * (N_BUCKETS - max_exact)).astype(jnp.int32)
    large = jnp.minimum(large, N_BUCKETS - 1)
    return jnp.where(dist < max_exact, dist, large)


def _band_bias(rel_bias):
    dist = (jnp.arange(A_BLOCK)[:, None] + A_BLOCK) - jnp.arange(2 * A_BLOCK)[None, :]
    valid = (dist >= 0) & (dist < WINDOW)
    bias = rel_bias.astype(F32)[_t5_bucket(jnp.maximum(dist, 0))]
    return jnp.where(valid[None], jnp.transpose(bias, (2, 0, 1)), NEG)


def _head_indicator(width, head_dim):
    e = (np.arange(width)[:, None] // head_dim == np.arange(LANES)[None, :]).astype(np.float32)
    return jnp.asarray(e, BF16), jnp.asarray(e.T, BF16)


def kernel(x, p, positions, norm_g, w_in, a_q_norm, a_k_norm, a_sinks, rel_bias, w_o_a, b_cq_norm, w_uq, b_ckv_norm, w_uk, w_uv, b_q_norm, b_k_norm, b_kr_norm, w_o_b, w_out, ple_norm_g, w_ple_gate, w_ple_proj, ple_post_g):
    batch, seq, _ = x.shape
    depth = p.shape[0]
    t = batch * seq
    tm = 512
    blk = 512
    offs = np.concatenate([[0], np.cumsum(SPLIT_SIZES)])
    e_a, et_a = _head_indicator(A_WIDTH, A_HEAD_DIM)
    e_b, et_b = _head_indicator(B_HEADS * B_NOPE, B_NOPE)
    inv_freq = ROPE_THETA ** (-jnp.arange(0, B_ROPE, 2, dtype=F32) / B_ROPE)
    freq_c = inv_freq[:, None]
    freq_r = jnp.tile(inv_freq, LANES // (B_ROPE // 2))[None, :]
    pos = positions.reshape(t)
    band_bias = _band_bias(rel_bias)
    row = lambda v: v.astype(F32)[None, :]

    x2 = x.reshape(t, D_MODEL)
    for i in range(depth):
        w = w_in[i]
        cols = [w[:, offs[j]:offs[j + 1]] for j in range(len(SPLIT_SIZES))]
        cols[6] = jnp.pad(cols[6], ((0, 0), (0, LANES - B_ROPE)))
        order = (0, 1, 2, 3, 4, 5, 6, 7, 8, 9)
        w1 = jnp.concatenate([cols[j] for j in order], axis=1).astype(BF16)
        gq_a = jnp.tile(a_q_norm[i].astype(F32) * A_HEAD_DIM ** -0.5, A_HEADS)[None, :]
        gk_a = jnp.tile(a_k_norm[i].astype(F32), A_KV_HEADS)[None, :]
        qa, ka, va, za, lat, zb, ga, gb = _proj(x2, row(norm_g[i]), w1, e_a, et_a, gq_a, gk_a, tm)

        gqn = (b_q_norm[i].astype(F32) * B_QK_DIM ** -0.5)[:, None]
        gkn = jnp.tile(b_k_norm[i].astype(F32), B_HEADS)[None, :]
        gkr = jnp.pad(b_kr_norm[i].astype(F32), (0, LANES - B_ROPE))[None, :]
        qt, kb, vt = _prep(lat, pos[:, None], pos[None, :], row(b_cq_norm[i]),
                           w_uq[i].T.astype(BF16), gqn, row(b_ckv_norm[i]), w_uk[i].astype(BF16),
                           w_uv[i].T.astype(BF16), e_b, et_b, gkn, gkr, freq_c, freq_r,
                           batch, seq, blk)

        sh = lambda a: a.reshape(batch, seq, a.shape[-1])
        oa = _swa(a_sinks[i].astype(F32), sh(qa), sh(ka), sh(va), band_bias, tm)
        ob = _flash(qt, kb, vt, blk, 2)

        x2 = _final(x2, p[i].reshape(t, PLE_DIM), oa.reshape(t, A_WIDTH), za,
                    ob.reshape(t, B_WIDTH), zb, ga, gb,
                    w_o_a[i].astype(BF16), w_o_b[i].astype(BF16), w_out[i].astype(BF16),
                    row(ple_norm_g[i]), w_ple_gate[i].astype(BF16), w_ple_proj[i].astype(BF16),
                    row(ple_post_g[i]), tm)
    return x2.reshape(batch, seq, D_MODEL)
```

```python
import functools
import math

import numpy as np
import jax
import jax.numpy as jnp
from jax import lax
from jax.experimental import pallas as pl
from jax.experimental.pallas import tpu as pltpu

F32 = jnp.float32
BF16 = jnp.bfloat16

D_MODEL = 1024
PLE_DIM = 256
EPS = 1e-6
NEG = -1e30

A_HEADS = 16
A_KV_HEADS = 2
A_HEAD_DIM = 64
A_WIDTH = A_HEADS * A_HEAD_DIM
A_KV_WIDTH = A_KV_HEADS * A_HEAD_DIM
A_GROUP = A_HEADS // A_KV_HEADS
WINDOW = 128
A_BLOCK = 128
N_BUCKETS = 32
MAX_DISTANCE = 128

B_HEADS = 16
B_Q_RANK = 256
B_KV_RANK = 128
B_NOPE = 64
B_ROPE = 32
B_QK_DIM = B_NOPE + B_ROPE
B_VDIM = 64
B_WIDTH = B_HEADS * B_VDIM
ROPE_THETA = 10000.0

SPLIT_SIZES = (A_WIDTH, A_KV_WIDTH, A_KV_WIDTH, A_WIDTH,
               B_Q_RANK, B_KV_RANK, B_ROPE, B_WIDTH,
               D_MODEL, D_MODEL)

LANES = 128
QK_PAD = 128
V_ROWS = 80
VMEM_LIMIT = 56 * 1024 * 1024

C_QA = 0
C_KVA = C_QA + A_WIDTH
C_ZA = C_KVA + 2 * A_KV_WIDTH
C_LAT = C_ZA + A_WIDTH
LAT_W = B_Q_RANK + B_KV_RANK + LANES
C_ZB = C_LAT + LAT_W
C_GA = C_ZB + B_WIDTH
C_GB = C_GA + D_MODEL
C_END = C_GB + D_MODEL


def _dot(a, b):
    return jnp.dot(a, b, preferred_element_type=F32)


def _dot_nt(a, b):
    return lax.dot_general(a, b, (((1,), (1,)), ((), ())), preferred_element_type=F32)


def _split_bf16(v):
    hi = v.astype(BF16)
    lo = (v - hi.astype(F32)).astype(BF16)
    return hi, lo


def _seg_rms(z, e, et, inv_width):
    hi, lo = _split_bf16(z * z)
    ss = _dot(hi, e) + _dot(lo, e)
    r = lax.rsqrt(ss * inv_width + EPS)
    r_hi, r_lo = _split_bf16(r)
    return z * (_dot(r_hi, et) + _dot(r_lo, et))


def _row_rms(v, width):
    return v * lax.rsqrt(jnp.sum(v * v, axis=-1, keepdims=True) * (1.0 / width) + EPS)


def _proj_kernel(x_ref, g_ref, w_ref, e_ref, et_ref, gq_ref, gk_ref,
                 qa_ref, ka_ref, va_ref, za_ref, lat_ref, zb_ref, ga_ref, gb_ref):
    x = x_ref[...]
    h = (_row_rms(x, D_MODEL) * g_ref[...]).astype(BF16)

    def proj(c0, c1):
        return _dot(h, w_ref[:, c0:c1])

    inv = 1.0 / A_HEAD_DIM
    q = proj(C_QA, C_KVA)
    qa_ref[...] = (_seg_rms(q, e_ref[...], et_ref[...], inv) * gq_ref[...]).astype(BF16)
    kv = proj(C_KVA, C_ZA)
    k = kv[:, :A_KV_WIDTH]
    ka_ref[...] = (_seg_rms(k, e_ref[:A_KV_WIDTH, :], et_ref[:, :A_KV_WIDTH], inv)
                   * gk_ref[...]).astype(BF16)
    va_ref[...] = kv[:, A_KV_WIDTH:].astype(BF16)
    za_ref[...] = proj(C_ZA, C_LAT).astype(BF16)
    lat_ref[...] = proj(C_LAT, C_ZB)
    zb_ref[...] = proj(C_ZB, C_GA).astype(BF16)
    ga_ref[...] = proj(C_GA, C_GB).astype(BF16)
    gb_ref[...] = proj(C_GB, C_END).astype(BF16)


def _proj(x2, g, w1, e, et, gq, gk, tm):
    t = x2.shape[0]
    row = lambda w: pl.BlockSpec((tm, w), lambda i: (i, 0))
    full = lambda a: pl.BlockSpec(a.shape, lambda i: (0,) * a.ndim)
    widths = (A_WIDTH, A_KV_WIDTH, A_KV_WIDTH, A_WIDTH, LAT_W, B_WIDTH, D_MODEL, D_MODEL)
    dtypes = (BF16, BF16, BF16, BF16, F32, BF16, BF16, BF16)
    return pl.pallas_call(
        _proj_kernel,
        grid=(t // tm,),
        in_specs=[row(D_MODEL), full(g), full(w1), full(e), full(et), full(gq), full(gk)],
        out_specs=[row(w) for w in widths],
        out_shape=[jax.ShapeDtypeStruct((t, w), d) for w, d in zip(widths, dtypes)],
        compiler_params=pltpu.CompilerParams(
            dimension_semantics=("arbitrary",), vmem_limit_bytes=VMEM_LIMIT),
        name="proj",
    )(x2, g, w1, e, et, gq, gk)


def _prep_kernel(lat_ref, posc_ref, posr_ref, gcq_ref, wuqt_ref, gqn_ref, gckv_ref, wuk_ref,
                 wuvt_ref, e_ref, et_ref, gkn_ref, gkr_ref, freqc_ref, freqr_ref,
                 qt_ref, k_ref, vt_ref):
    lat = lat_ref[...]
    tm = lat.shape[0]
    cq = (_row_rms(lat[:, :B_Q_RANK], B_Q_RANK) * gcq_ref[...]).astype(BF16)
    qt = _dot_nt(wuqt_ref[...], cq)
    ang_r = freqc_ref[...] * posr_ref[...].astype(F32)
    cos_r = jnp.cos(ang_r)
    sin_r = jnp.sin(ang_r)
    gqn = gqn_ref[...]
    half = B_ROPE // 2
    zeros_q = jnp.zeros((QK_PAD - B_QK_DIM, tm), BF16)
    for h in range(B_HEADS):
        blk = qt[h * B_QK_DIM:(h + 1) * B_QK_DIM, :]
        ms = jnp.sum(blk * blk, axis=0, keepdims=True) * (1.0 / B_QK_DIM)
        qn = blk * lax.rsqrt(ms + EPS) * gqn
        x1 = qn[B_NOPE:B_NOPE + half, :]
        x2 = qn[B_NOPE + half:, :]
        qt_ref[0, h, 0:B_NOPE, :] = qn[:B_NOPE, :].astype(BF16)
        qt_ref[0, h, B_NOPE:B_NOPE + half, :] = (x1 * cos_r - x2 * sin_r).astype(BF16)
        qt_ref[0, h, B_NOPE + half:B_QK_DIM, :] = (x2 * cos_r + x1 * sin_r).astype(BF16)
        qt_ref[0, h, B_QK_DIM:, :] = zeros_q

    ckv = (_row_rms(lat[:, B_Q_RANK:B_Q_RANK + B_KV_RANK], B_KV_RANK) * gckv_ref[...]).astype(BF16)
    kn = _seg_rms(_dot(ckv, wuk_ref[...]), e_ref[...], et_ref[...], 1.0 / B_NOPE) * gkn_ref[...]
    vt = _dot_nt(wuvt_ref[...], ckv)

    kr = lat[:, B_Q_RANK + B_KV_RANK:]
    kr = kr * lax.rsqrt(jnp.sum(kr * kr, axis=-1, keepdims=True) * (1.0 / B_ROPE) + EPS) * gkr_ref[...]
    ang_c = posc_ref[...].astype(F32) * freqr_ref[...]
    lane = lax.broadcasted_iota(jnp.int32, (tm, LANES), 1)
    rot = jnp.where(lane < half, -pltpu.roll(kr, LANES - half, 1),
                    jnp.where(lane < B_ROPE, pltpu.roll(kr, half, 1), 0.0))
    kr = kr * jnp.cos(ang_c) + rot * jnp.sin(ang_c)
    kr = pltpu.roll(kr, B_NOPE, 1)
    lo_half = lane < B_NOPE
    tk = vt_ref.shape[-1]
    ones_rows = (lax.broadcasted_iota(jnp.int32, (V_ROWS - B_VDIM, tk), 0) == 0).astype(BF16)
    for h2 in range(B_HEADS // 2):
        pair = kn[:, h2 * LANES:(h2 + 1) * LANES]
        k_ref[0, 2 * h2] = jnp.where(lo_half, pair, kr).astype(BF16)
        k_ref[0, 2 * h2 + 1] = jnp.where(lo_half, pltpu.roll(pair, B_NOPE, 1), kr).astype(BF16)
    for h in range(B_HEADS):
        for c in range(tm // tk):
            vt_ref[0, h, c, 0:B_VDIM, :] = vt[h * B_VDIM:(h + 1) * B_VDIM, c * tk:(c + 1) * tk].astype(BF16)
            vt_ref[0, h, c, B_VDIM:, :] = ones_rows


def _prep(lat, pos_col, pos_row, gcq, wuqt, gqn, gckv, wuk, wuvt, e, et, gkn, gkr, freq_c, freq_r,
          batch, seq, tm, tk):
    nt = seq // tm
    nc = tm // tk
    full = lambda a: pl.BlockSpec(a.shape, lambda b, i: (0,) * a.ndim)
    return pl.pallas_call(
        _prep_kernel,
        grid=(batch, nt),
        in_specs=[pl.BlockSpec((tm, LAT_W), lambda b, i: (b * nt + i, 0)),
                  pl.BlockSpec((tm, 1), lambda b, i: (b * nt + i, 0)),
                  pl.BlockSpec((1, tm), lambda b, i: (0, b * nt + i)),
                  full(gcq), full(wuqt), full(gqn), full(gckv), full(wuk), full(wuvt),
                  full(e), full(et), full(gkn), full(gkr), full(freq_c), full(freq_r)],
        out_specs=[pl.BlockSpec((1, B_HEADS, QK_PAD, tm), lambda b, i: (b, 0, 0, i)),
                   pl.BlockSpec((1, B_HEADS, tm, QK_PAD), lambda b, i: (b, 0, i, 0)),
                   pl.BlockSpec((1, B_HEADS, nc, V_ROWS, tk), lambda b, i: (b, 0, i, 0, 0))],
        out_shape=[jax.ShapeDtypeStruct((batch, B_HEADS, QK_PAD, seq), BF16),
                   jax.ShapeDtypeStruct((batch, B_HEADS, seq, QK_PAD), BF16),
                   jax.ShapeDtypeStruct((batch, B_HEADS, nt * nc, V_ROWS, tk), BF16)],
        compiler_params=pltpu.CompilerParams(
            dimension_semantics=("arbitrary", "arbitrary"), vmem_limit_bytes=VMEM_LIMIT),
        name="prep",
    )(lat, pos_col, pos_row, gcq, wuqt, gqn, gckv, wuk, wuvt, e, et, gkn, gkr, freq_c, freq_r)


def _swa_kernel(sink_ref, q_ref, k_ref, v_ref, kp_ref, vp_ref, bias_ref, o_ref, *, nsub):
    first = pl.program_id(1) == 0
    col = lax.broadcasted_iota(jnp.int32, (A_BLOCK, 2 * A_BLOCK), 1)
    for sb in range(nsub):
        r0 = sb * A_BLOCK
        if sb == 0:
            kb = jnp.concatenate([kp_ref[0], k_ref[0, 0:A_BLOCK, :]], axis=0)
            vb = jnp.concatenate([vp_ref[0], v_ref[0, 0:A_BLOCK, :]], axis=0)
            pad = jnp.logical_and(first, col < A_BLOCK)
        else:
            kb = k_ref[0, r0 - A_BLOCK:r0 + A_BLOCK, :]
            vb = v_ref[0, r0 - A_BLOCK:r0 + A_BLOCK, :]
            pad = None
        for h in range(A_HEADS):
            g = h // A_GROUP
            qh = q_ref[0, r0:r0 + A_BLOCK, h * A_HEAD_DIM:(h + 1) * A_HEAD_DIM]
            kg = kb[:, g * A_HEAD_DIM:(g + 1) * A_HEAD_DIM]
            vg = vb[:, g * A_HEAD_DIM:(g + 1) * A_HEAD_DIM]
            s = _dot_nt(qh, kg) + bias_ref[h]
            if pad is not None:
                s = jnp.where(pad, NEG, s)
            sink = sink_ref[h]
            m = jnp.maximum(jnp.max(s, axis=-1, keepdims=True), sink)
            e = jnp.exp(s - m)
            den = jnp.sum(e, axis=-1, keepdims=True) + jnp.exp(sink - m)
            o = _dot(e.astype(BF16), vg) / den
            o_ref[0, r0:r0 + A_BLOCK, h * A_HEAD_DIM:(h + 1) * A_HEAD_DIM] = o.astype(BF16)


def _swa(sinks, qa, ka, va, bias, tq):
    batch, seq, _ = qa.shape
    nsub = tq // A_BLOCK
    prev = lambda b, i: (b, jnp.maximum(i * nsub - 1, 0), 0)
    return pl.pallas_call(
        functools.partial(_swa_kernel, nsub=nsub),
        grid=(batch, seq // tq),
        in_specs=[pl.BlockSpec(memory_space=pltpu.SMEM),
                  pl.BlockSpec((1, tq, A_WIDTH), lambda b, i: (b, i, 0)),
                  pl.BlockSpec((1, tq, A_KV_WIDTH), lambda b, i: (b, i, 0)),
                  pl.BlockSpec((1, tq, A_KV_WIDTH), lambda b, i: (b, i, 0)),
                  pl.BlockSpec((1, A_BLOCK, A_KV_WIDTH), prev),
                  pl.BlockSpec((1, A_BLOCK, A_KV_WIDTH), prev),
                  pl.BlockSpec(bias.shape, lambda b, i: (0, 0, 0))],
        out_specs=pl.BlockSpec((1, tq, A_WIDTH), lambda b, i: (b, i, 0)),
        out_shape=jax.ShapeDtypeStruct((batch, seq, A_WIDTH), BF16),
        compiler_params=pltpu.CompilerParams(
            dimension_semantics=("arbitrary", "arbitrary"), vmem_limit_bytes=VMEM_LIMIT),
        name="swa",
    )(sinks, qa, ka, va, ka, va, bias)


def _flash_kernel(qt_ref, k_ref, vt_ref, o_ref, m_sc, acc_sc, s0_sc, s1_sc, bm0_sc, bm1_sc, *, tq, nh):
    tk = tq // 2
    i = pl.program_id(2)
    m_sc[...] = jnp.full(m_sc.shape, NEG, F32)
    acc_sc[...] = jnp.zeros(acc_sc.shape, F32)
    s_bufs = (s0_sc, s1_sc)
    bm_bufs = (bm0_sc, bm1_sc)

    def scores(h, j, slot, diag_half):
        kb = k_ref[0, h, pl.ds(pl.multiple_of(j * tk, tk), tk), :]
        s = _dot(kb, qt_ref[0, h])
        if diag_half is not None:
            key = lax.broadcasted_iota(jnp.int32, (tk, tq), 0) + diag_half * tk
            qry = lax.broadcasted_iota(jnp.int32, (tk, tq), 1)
            s = jnp.where(key <= qry, s, NEG)
        s_bufs[slot][h] = s
        bm_bufs[slot][h] = jnp.max(s, axis=0, keepdims=True)

    def softmax_pv(h, j, slot):
        m_old = m_sc[h]
        m_new = jnp.maximum(m_old, bm_bufs[slot][h])
        p = jnp.exp2(s_bufs[slot][h] - m_new).astype(BF16)
        alpha = jnp.exp2(m_old - m_new)
        m_sc[h] = m_new
        acc_sc[h] = acc_sc[h] * alpha + _dot(vt_ref[0, h, j], p)

    def stage(j, slot, next_diag_half, has_next=True):
        for h in range(nh):
            if has_next:
                scores(h, j + 1, 1 - slot, next_diag_half)
            softmax_pv(h, j, slot)

    @pl.when(i == 0)
    def _():
        for h in range(nh):
            scores(h, 0, 0, 0)

    @pl.when(i > 0)
    def _():
        for h in range(nh):
            scores(h, 0, 0, None)

    def pair(jj, carry):
        stage(2 * jj, 0, None)
        stage(2 * jj + 1, 1, None)
        return carry

    lax.fori_loop(0, i - 1, pair, 0)

    @pl.when(i > 0)
    def _():
        stage(2 * i - 2, 0, None)
        stage(2 * i - 1, 1, 0)

    stage(2 * i, 0, 1)
    stage(2 * i + 1, 1, None, has_next=False)

    outs = []
    for h in range(nh):
        acc = acc_sc[h]
        outs.append(acc[:B_VDIM, :] / acc[B_VDIM:B_VDIM + 1, :])
    o_ref[0] = jnp.concatenate(outs, axis=0).T.astype(BF16)


def _flash(qt, k, vt, tq, nh):
    batch, heads, _, seq = qt.shape
    tk = tq // 2
    nq = seq // tq
    return pl.pallas_call(
        functools.partial(_flash_kernel, tq=tq, nh=nh),
        grid=(batch, heads // nh, nq),
        in_specs=[pl.BlockSpec((1, nh, QK_PAD, tq), lambda b, g, i: (b, g, 0, i)),
                  pl.BlockSpec((1, nh, seq, QK_PAD), lambda b, g, i: (b, g, 0, 0)),
                  pl.BlockSpec((1, nh, seq // tk, V_ROWS, tk), lambda b, g, i: (b, g, 0, 0, 0))],
        out_specs=pl.BlockSpec((1, tq, nh * B_VDIM), lambda b, g, i: (b, i, g)),
        out_shape=jax.ShapeDtypeStruct((batch, seq, B_WIDTH), BF16),
        scratch_shapes=[pltpu.VMEM((nh, 1, tq), F32), pltpu.VMEM((nh, V_ROWS, tq), F32),
                        pltpu.VMEM((nh, tk, tq), F32), pltpu.VMEM((nh, tk, tq), F32),
                        pltpu.VMEM((nh, 1, tq), F32), pltpu.VMEM((nh, 1, tq), F32)],
        compiler_params=pltpu.CompilerParams(
            dimension_semantics=("arbitrary", "arbitrary", "arbitrary"),
            vmem_limit_bytes=VMEM_LIMIT),
        name="flash",
    )(qt, k, vt)


def _final_kernel(x_ref, p_ref, oa_ref, za_ref, ob_ref, zb_ref, ga_ref, gb_ref,
                  woa_ref, wob_ref, wout_ref, gpl_ref, wpg_ref, wpp_ref, gpost_ref, out_ref):
    ya = _dot((oa_ref[...].astype(F32) * jax.nn.silu(za_ref[...].astype(F32))).astype(BF16), woa_ref[...])
    yb = _dot((ob_ref[...].astype(F32) * jax.nn.silu(zb_ref[...].astype(F32))).astype(BF16), wob_ref[...])
    merged = (jax.nn.sigmoid(ga_ref[...].astype(F32)) * ya
              + jax.nn.sigmoid(gb_ref[...].astype(F32)) * yb)
    x1 = x_ref[...] + _dot(merged.astype(BF16), wout_ref[...])
    gate = jax.nn.sigmoid(_dot((_row_rms(x1, D_MODEL) * gpl_ref[...]).astype(BF16), wpg_ref[...]))
    emb = _row_rms(_dot(p_ref[...].astype(BF16), wpp_ref[...]), D_MODEL) * gpost_ref[...]
    out_ref[...] = x1 + gate * emb


def _final(x2, p2, oa, za, ob, zb, ga, gb, woa, wob, wout, gpl, wpg, wpp, gpost, tm):
    t = x2.shape[0]
    row = lambda a: pl.BlockSpec((tm, a.shape[1]), lambda i: (i, 0))
    full = lambda a: pl.BlockSpec(a.shape, lambda i: (0,) * a.ndim)
    acts = (x2, p2, oa, za, ob, zb, ga, gb)
    consts = (woa, wob, wout, gpl, wpg, wpp, gpost)
    return pl.pallas_call(
        _final_kernel,
        grid=(t // tm,),
        in_specs=[row(a) for a in acts] + [full(a) for a in consts],
        out_specs=pl.BlockSpec((tm, D_MODEL), lambda i: (i, 0)),
        out_shape=jax.ShapeDtypeStruct((t, D_MODEL), F32),
        compiler_params=pltpu.CompilerParams(
            dimension_semantics=("arbitrary",), vmem_limit_bytes=VMEM_LIMIT),
        name="final",
    )(*acts, *consts)


def _t5_bucket(dist):
    max_exact = N_BUCKETS // 2
    d = jnp.maximum(dist, 1).astype(F32)
    large = max_exact + (jnp.log(d / max_exact) / math.log(MAX_DISTANCE / max_exact)
                         * (N_BUCKETS - max_exact)).astype(jnp.int32)
    large = jnp.minimum(large, N_BUCKETS - 1)
    return jnp.where(dist < max_exact, dist, large)


def _band_bias(rel_bias):
    dist = (jnp.arange(A_BLOCK)[:, None] + A_BLOCK) - jnp.arange(2 * A_BLOCK)[None, :]
    valid = (dist >= 0) & (dist < WINDOW)
    bias = rel_bias.astype(F32)[_t5_bucket(jnp.maximum(dist, 0))]
    return jnp.where(valid[None], jnp.transpose(bias, (2, 0, 1)), NEG)


def _head_indicator(width, head_dim):
    e = (np.arange(width)[:, None] // head_dim == np.arange(LANES)[None, :]).astype(np.float32)
    return jnp.asarray(e, BF16), jnp.asarray(e.T, BF16)


def kernel(x, p, positions, norm_g, w_in, a_q_norm, a_k_norm, a_sinks, rel_bias, w_o_a, b_cq_norm, w_uq, b_ckv_norm, w_uk, w_uv, b_q_norm, b_k_norm, b_kr_norm, w_o_b, w_out, ple_norm_g, w_ple_gate, w_ple_proj, ple_post_g):
    batch, seq, _ = x.shape
    depth = p.shape[0]
    t = batch * seq
    tm = 512
    blk = 512
    offs = np.concatenate([[0], np.cumsum(SPLIT_SIZES)])
    e_a, et_a = _head_indicator(A_WIDTH, A_HEAD_DIM)
    e_b, et_b = _head_indicator(B_HEADS * B_NOPE, B_NOPE)
    inv_freq = ROPE_THETA ** (-jnp.arange(0, B_ROPE, 2, dtype=F32) / B_ROPE)
    freq_c = inv_freq[:, None]
    freq_r = jnp.tile(inv_freq, LANES // (B_ROPE // 2))[None, :]
    pos = positions.reshape(t)
    band_bias = _band_bias(rel_bias)
    row = lambda v: v.astype(F32)[None, :]

    x2 = x.reshape(t, D_MODEL)
    for i in range(depth):
        w = w_in[i]
        cols = [w[:, offs[j]:offs[j + 1]] for j in range(len(SPLIT_SIZES))]
        cols[6] = jnp.pad(cols[6], ((0, 0), (0, LANES - B_ROPE)))
        order = (0, 1, 2, 3, 4, 5, 6, 7, 8, 9)
        w1 = jnp.concatenate([cols[j] for j in order], axis=1).astype(BF16)
        gq_a = jnp.tile(a_q_norm[i].astype(F32) * A_HEAD_DIM ** -0.5, A_HEADS)[None, :]
        gk_a = jnp.tile(a_k_norm[i].astype(F32), A_KV_HEADS)[None, :]
        qa, ka, va, za, lat, zb, ga, gb = _proj(x2, row(norm_g[i]), w1, e_a, et_a, gq_a, gk_a, tm)

        gqn = (b_q_norm[i].astype(F32) * (B_QK_DIM ** -0.5 * math.log2(math.e)))[:, None]
        gkn = jnp.tile(b_k_norm[i].astype(F32), B_HEADS)[None, :]
        gkr = jnp.pad(b_kr_norm[i].astype(F32), (0, LANES - B_ROPE))[None, :]
        qt, kb, vt = _prep(lat, pos[:, None], pos[None, :], row(b_cq_norm[i]),
                           w_uq[i].T.astype(BF16), gqn, row(b_ckv_norm[i]), w_uk[i].astype(BF16),
                           w_uv[i].T.astype(BF16), e_b, et_b, gkn, gkr, freq_c, freq_r,
                           batch, seq, blk, blk // 2)

        sh = lambda a: a.reshape(batch, seq, a.shape[-1])
        oa = _swa(a_sinks[i].astype(F32), sh(qa), sh(ka), sh(va), band_bias, tm)
        ob = _flash(qt, kb, vt, blk, 2)

        x2 = _final(x2, p[i].reshape(t, PLE_DIM), oa.reshape(t, A_WIDTH), za,
                    ob.reshape(t, B_WIDTH), zb, ga, gb,
                    w_o_a[i].astype(BF16), w_o_b[i].astype(BF16), w_out[i].astype(BF16),
                    row(ple_norm_g[i]), w_ple_gate[i].astype(BF16), w_ple_proj[i].astype(BF16),
                    row(ple_post_g[i]), tm)
    return x2.reshape(batch, seq, D_MODEL)
```

```python
import functools
import math

import numpy as np
import jax
import jax.numpy as jnp
from jax import lax
from jax.experimental import pallas as pl
from jax.experimental.pallas import tpu as pltpu

F32 = jnp.float32
BF16 = jnp.bfloat16

D_MODEL = 1024
PLE_DIM = 256
EPS = 1e-6
NEG = -1e30

A_HEADS = 16
A_KV_HEADS = 2
A_HEAD_DIM = 64
A_WIDTH = A_HEADS * A_HEAD_DIM
A_KV_WIDTH = A_KV_HEADS * A_HEAD_DIM
A_GROUP = A_HEADS // A_KV_HEADS
WINDOW = 128
A_BLOCK = 128
N_BUCKETS = 32
MAX_DISTANCE = 128

B_HEADS = 16
B_Q_RANK = 256
B_KV_RANK = 128
B_NOPE = 64
B_ROPE = 32
B_QK_DIM = B_NOPE + B_ROPE
B_VDIM = 64
B_WIDTH = B_HEADS * B_VDIM
ROPE_THETA = 10000.0

SPLIT_SIZES = (A_WIDTH, A_KV_WIDTH, A_KV_WIDTH, A_WIDTH,
               B_Q_RANK, B_KV_RANK, B_ROPE, B_WIDTH,
               D_MODEL, D_MODEL)

LANES = 128
QK_PAD = 128
V_ROWS = 80
VMEM_LIMIT = 56 * 1024 * 1024

QVT_ROWS = A_WIDTH + A_KV_WIDTH
SWA_V_ROWS = A_HEAD_DIM + 16
C_KA = 0
C_ZA = C_KA + A_KV_WIDTH
C_LAT = C_ZA + A_WIDTH
LAT_W = B_Q_RANK + B_KV_RANK + LANES
C_ZB = C_LAT + LAT_W
C_GA = C_ZB + B_WIDTH
C_GB = C_GA + D_MODEL
C_END = C_GB + D_MODEL


def _dot(a, b):
    return jnp.dot(a, b, preferred_element_type=F32)


def _dot_nt(a, b):
    return lax.dot_general(a, b, (((1,), (1,)), ((), ())), preferred_element_type=F32)


def _split_bf16(v):
    hi = v.astype(BF16)
    lo = (v - hi.astype(F32)).astype(BF16)
    return hi, lo


def _seg_rms(z, e, et, inv_width):
    hi, lo = _split_bf16(z * z)
    ss = _dot(hi, e) + _dot(lo, e)
    r = lax.rsqrt(ss * inv_width + EPS)
    r_hi, r_lo = _split_bf16(r)
    return z * (_dot(r_hi, et) + _dot(r_lo, et))


def _row_rms(v, width):
    return v * lax.rsqrt(jnp.sum(v * v, axis=-1, keepdims=True) * (1.0 / width) + EPS)


def _proj_kernel(x_ref, g_ref, wt_ref, w_ref, e_ref, et_ref, gq_ref, gk_ref,
                 qvt_ref, ka_ref, za_ref, lat_ref, zb_ref, ga_ref, gb_ref):
    x = x_ref[...]
    h = (_row_rms(x, D_MODEL) * g_ref[...]).astype(BF16)

    qvt = _dot_nt(wt_ref[...], h)
    gq = gq_ref[...]
    for hd in range(A_HEADS):
        blk = qvt[hd * A_HEAD_DIM:(hd + 1) * A_HEAD_DIM, :]
        ms = jnp.sum(blk * blk, axis=0, keepdims=True) * (1.0 / A_HEAD_DIM)
        qvt_ref[hd * A_HEAD_DIM:(hd + 1) * A_HEAD_DIM, :] = (blk * lax.rsqrt(ms + EPS) * gq).astype(BF16)
    qvt_ref[A_WIDTH:, :] = qvt[A_WIDTH:, :].astype(BF16)

    def proj(c0, c1):
        return _dot(h, w_ref[:, c0:c1])

    k = proj(C_KA, C_ZA)
    ka_ref[...] = (_seg_rms(k, e_ref[...], et_ref[...], 1.0 / A_HEAD_DIM) * gk_ref[...]).astype(BF16)
    za_ref[...] = proj(C_ZA, C_LAT).astype(BF16)
    lat_ref[...] = proj(C_LAT, C_ZB)
    zb_ref[...] = proj(C_ZB, C_GA).astype(BF16)
    ga_ref[...] = proj(C_GA, C_GB).astype(BF16)
    gb_ref[...] = proj(C_GB, C_END).astype(BF16)


def _proj(x2, g, wt, w1, e, et, gq, gk, tm):
    t = x2.shape[0]
    row = lambda w: pl.BlockSpec((tm, w), lambda i: (i, 0))
    full = lambda a: pl.BlockSpec(a.shape, lambda i: (0,) * a.ndim)
    widths = (A_KV_WIDTH, A_WIDTH, LAT_W, B_WIDTH, D_MODEL, D_MODEL)
    dtypes = (BF16, BF16, F32, BF16, BF16, BF16)
    return pl.pallas_call(
        _proj_kernel,
        grid=(t // tm,),
        in_specs=[row(D_MODEL), full(g), full(wt), full(w1), full(e), full(et), full(gq), full(gk)],
        out_specs=[pl.BlockSpec((QVT_ROWS, tm), lambda i: (0, i))] + [row(w) for w in widths],
        out_shape=[jax.ShapeDtypeStruct((QVT_ROWS, t), BF16)]
        + [jax.ShapeDtypeStruct((t, w), d) for w, d in zip(widths, dtypes)],
        compiler_params=pltpu.CompilerParams(
            dimension_semantics=("arbitrary",), vmem_limit_bytes=VMEM_LIMIT),
        name="proj",
    )(x2, g, wt, w1, e, et, gq, gk)


def _prep_kernel(lat_ref, posc_ref, posr_ref, gcq_ref, wuqt_ref, gqn_ref, gckv_ref, wuk_ref,
                 wuvt_ref, e_ref, et_ref, gkn_ref, gkr_ref, freqc_ref, freqr_ref,
                 qt_ref, k_ref, vt_ref):
    lat = lat_ref[...]
    tm = lat.shape[0]
    cq = (_row_rms(lat[:, :B_Q_RANK], B_Q_RANK) * gcq_ref[...]).astype(BF16)
    qt = _dot_nt(wuqt_ref[...], cq)
    ang_r = freqc_ref[...] * posr_ref[...].astype(F32)
    cos_r = jnp.cos(ang_r)
    sin_r = jnp.sin(ang_r)
    gqn = gqn_ref[...]
    half = B_ROPE // 2
    zeros_q = jnp.zeros((QK_PAD - B_QK_DIM, tm), BF16)
    for h in range(B_HEADS):
        blk = qt[h * B_QK_DIM:(h + 1) * B_QK_DIM, :]
        ms = jnp.sum(blk * blk, axis=0, keepdims=True) * (1.0 / B_QK_DIM)
        qn = blk * lax.rsqrt(ms + EPS) * gqn
        x1 = qn[B_NOPE:B_NOPE + half, :]
        x2 = qn[B_NOPE + half:, :]
        qt_ref[0, h, 0:B_NOPE, :] = qn[:B_NOPE, :].astype(BF16)
        qt_ref[0, h, B_NOPE:B_NOPE + half, :] = (x1 * cos_r - x2 * sin_r).astype(BF16)
        qt_ref[0, h, B_NOPE + half:B_QK_DIM, :] = (x2 * cos_r + x1 * sin_r).astype(BF16)
        qt_ref[0, h, B_QK_DIM:, :] = zeros_q

    ckv = (_row_rms(lat[:, B_Q_RANK:B_Q_RANK + B_KV_RANK], B_KV_RANK) * gckv_ref[...]).astype(BF16)
    kn = _seg_rms(_dot(ckv, wuk_ref[...]), e_ref[...], et_ref[...], 1.0 / B_NOPE) * gkn_ref[...]
    vt = _dot_nt(wuvt_ref[...], ckv)

    kr = lat[:, B_Q_RANK + B_KV_RANK:]
    kr = kr * lax.rsqrt(jnp.sum(kr * kr, axis=-1, keepdims=True) * (1.0 / B_ROPE) + EPS) * gkr_ref[...]
    ang_c = posc_ref[...].astype(F32) * freqr_ref[...]
    lane = lax.broadcasted_iota(jnp.int32, (tm, LANES), 1)
    rot = jnp.where(lane < half, -pltpu.roll(kr, LANES - half, 1),
                    jnp.where(lane < B_ROPE, pltpu.roll(kr, half, 1), 0.0))
    kr = kr * jnp.cos(ang_c) + rot * jnp.sin(ang_c)
    kr = pltpu.roll(kr, B_NOPE, 1)
    lo_half = lane < B_NOPE
    tk = vt_ref.shape[-1]
    ones_rows = (lax.broadcasted_iota(jnp.int32, (V_ROWS - B_VDIM, tk), 0) == 0).astype(BF16)
    for h2 in range(B_HEADS // 2):
        pair = kn[:, h2 * LANES:(h2 + 1) * LANES]
        k_ref[0, 2 * h2] = jnp.where(lo_half, pair, kr).astype(BF16)
        k_ref[0, 2 * h2 + 1] = jnp.where(lo_half, pltpu.roll(pair, B_NOPE, 1), kr).astype(BF16)
    for h in range(B_HEADS):
        for c in range(tm // tk):
            vt_ref[0, h, c, 0:B_VDIM, :] = vt[h * B_VDIM:(h + 1) * B_VDIM, c * tk:(c + 1) * tk].astype(BF16)
            vt_ref[0, h, c, B_VDIM:, :] = ones_rows


def _prep(lat, pos_col, pos_row, gcq, wuqt, gqn, gckv, wuk, wuvt, e, et, gkn, gkr, freq_c, freq_r,
          batch, seq, tm, tk):
    nt = seq // tm
    nc = tm // tk
    full = lambda a: pl.BlockSpec(a.shape, lambda b, i: (0,) * a.ndim)
    return pl.pallas_call(
        _prep_kernel,
        grid=(batch, nt),
        in_specs=[pl.BlockSpec((tm, LAT_W), lambda b, i: (b * nt + i, 0)),
                  pl.BlockSpec((tm, 1), lambda b, i: (b * nt + i, 0)),
                  pl.BlockSpec((1, tm), lambda b, i: (0, b * nt + i)),
                  full(gcq), full(wuqt), full(gqn), full(gckv), full(wuk), full(wuvt),
                  full(e), full(et), full(gkn), full(gkr), full(freq_c), full(freq_r)],
        out_specs=[pl.BlockSpec((1, B_HEADS, QK_PAD, tm), lambda b, i: (b, 0, 0, i)),
                   pl.BlockSpec((1, B_HEADS, tm, QK_PAD), lambda b, i: (b, 0, i, 0)),
                   pl.BlockSpec((1, B_HEADS, nc, V_ROWS, tk), lambda b, i: (b, 0, i, 0, 0))],
        out_shape=[jax.ShapeDtypeStruct((batch, B_HEADS, QK_PAD, seq), BF16),
                   jax.ShapeDtypeStruct((batch, B_HEADS, seq, QK_PAD), BF16),
                   jax.ShapeDtypeStruct((batch, B_HEADS, nt * nc, V_ROWS, tk), BF16)],
        compiler_params=pltpu.CompilerParams(
            dimension_semantics=("arbitrary", "arbitrary"), vmem_limit_bytes=VMEM_LIMIT),
        name="prep",
    )(lat, pos_col, pos_row, gcq, wuqt, gqn, gckv, wuk, wuvt, e, et, gkn, gkr, freq_c, freq_r)


def _band_bias_kernel(rbt_ref, onehot_ref, o_ref):
    rbt = rbt_ref[...]
    onehot = onehot_ref[...]
    t = jnp.zeros((A_HEADS, WINDOW), F32)
    for b in range(N_BUCKETS):
        t = t + rbt[:, b:b + 1] * onehot[b:b + 1, :]
    t = t * math.log2(math.e)
    band = 2 * A_BLOCK
    key = lax.broadcasted_iota(jnp.int32, (band, band), 0)
    neg = jnp.full((1, A_BLOCK), NEG, F32)
    for hd in range(A_HEADS):
        x = jnp.broadcast_to(jnp.concatenate([neg, t[hd:hd + 1, :]], axis=1), (band, band))
        for bit in range(8):
            x = jnp.where((key >> bit) & 1 == 1, pltpu.roll(x, 1 << bit, 1), x)
        g, hh = divmod(hd, A_GROUP)
        o_ref[g, :, hh * A_BLOCK:(hh + 1) * A_BLOCK] = x[:, :A_BLOCK]


def _band_bias(rel_bias):
    dist = np.arange(WINDOW)
    max_exact = N_BUCKETS // 2
    large = max_exact + (np.log(np.maximum(dist, 1).astype(np.float32) / max_exact)
                         / math.log(MAX_DISTANCE / max_exact) * (N_BUCKETS - max_exact)).astype(np.int32)
    bucket = np.where(dist < max_exact, dist, np.minimum(large, N_BUCKETS - 1))
    onehot = jnp.asarray(np.arange(N_BUCKETS)[:, None] == bucket[None, :], F32)
    return pl.pallas_call(
        _band_bias_kernel,
        out_shape=jax.ShapeDtypeStruct((A_KV_HEADS, 2 * A_BLOCK, A_GROUP * A_BLOCK), F32),
        name="band_bias",
    )(rel_bias.astype(F32).T, onehot)


def _swa_kernel(qv_ref, k_ref, kp_ref, vp_ref, bias_ref, sink_ref, o_ref, *, nsub):
    first = pl.program_id(1) == 0
    band = 2 * A_BLOCK
    width = A_GROUP * A_BLOCK
    pad = jnp.logical_and(first, lax.broadcasted_iota(jnp.int32, (band, width), 0) < A_BLOCK)
    zeros_q = jnp.zeros((A_HEAD_DIM, width), BF16)
    ones_rows = (lax.broadcasted_iota(jnp.int32, (SWA_V_ROWS - A_HEAD_DIM, band), 0) == 0).astype(BF16)

    def scores(sb, g):
        r0 = sb * A_BLOCK
        if sb == 0:
            kb = jnp.concatenate([kp_ref[0], k_ref[0, 0:A_BLOCK, :]], axis=0)
        else:
            kb = k_ref[0, r0 - A_BLOCK:r0 + A_BLOCK, :]
        tiles = [qv_ref[(g * A_GROUP + hh) * A_HEAD_DIM:(g * A_GROUP + hh + 1) * A_HEAD_DIM, r0:r0 + A_BLOCK]
                 for hh in range(A_GROUP)]
        qt = jnp.concatenate(tiles, axis=1)
        rhs = jnp.concatenate([qt, zeros_q] if g == 0 else [zeros_q, qt], axis=0)
        s = _dot(kb, rhs) + bias_ref[g]
        if sb == 0:
            s = jnp.where(pad, NEG, s)
        return s

    def finish(sb, g, s):
        r0 = sb * A_BLOCK
        sink = sink_ref[g]
        m = jnp.maximum(jnp.max(s, axis=0, keepdims=True), sink)
        e = jnp.exp2(s - m).astype(BF16)
        v0 = A_WIDTH + g * A_HEAD_DIM
        if sb == 0:
            vt = jnp.concatenate([vp_ref[g * A_HEAD_DIM:(g + 1) * A_HEAD_DIM, :],
                                  qv_ref[v0:v0 + A_HEAD_DIM, 0:A_BLOCK]], axis=1)
        else:
            vt = qv_ref[v0:v0 + A_HEAD_DIM, r0 - A_BLOCK:r0 + A_BLOCK]
        acc = _dot(jnp.concatenate([vt, ones_rows], axis=0), e)
        den = acc[A_HEAD_DIM:A_HEAD_DIM + 1, :] + jnp.exp2(sink - m)
        o = (acc[:A_HEAD_DIM, :] / den).astype(BF16)
        for hh in range(A_GROUP):
            hd = g * A_GROUP + hh
            o_ref[hd * A_HEAD_DIM:(hd + 1) * A_HEAD_DIM, r0:r0 + A_BLOCK] = o[:, hh * A_BLOCK:(hh + 1) * A_BLOCK]

    units = [(sb, g) for sb in range(nsub) for g in range(A_KV_HEADS)]
    s_next = scores(*units[0])
    for n, unit in enumerate(units):
        s_cur = s_next
        if n + 1 < len(units):
            s_next = scores(*units[n + 1])
        finish(*unit, s_cur)


def _swa(qvt, ka, bias, sinks, batch, seq, tq):
    nsub = tq // A_BLOCK
    nt = seq // tq
    vrow = A_WIDTH // A_KV_WIDTH
    return pl.pallas_call(
        functools.partial(_swa_kernel, nsub=nsub),
        grid=(batch, nt),
        in_specs=[pl.BlockSpec((QVT_ROWS, tq), lambda b, i: (0, b * nt + i)),
                  pl.BlockSpec((1, tq, A_KV_WIDTH), lambda b, i: (b, i, 0)),
                  pl.BlockSpec((1, A_BLOCK, A_KV_WIDTH), lambda b, i: (b, jnp.maximum(i * nsub - 1, 0), 0)),
                  pl.BlockSpec((A_KV_WIDTH, A_BLOCK),
                               lambda b, i: (vrow, jnp.maximum((b * nt + i) * nsub - 1, 0))),
                  pl.BlockSpec(bias.shape, lambda b, i: (0, 0, 0)),
                  pl.BlockSpec(sinks.shape, lambda b, i: (0, 0, 0))],
        out_specs=pl.BlockSpec((A_WIDTH, tq), lambda b, i: (0, b * nt + i)),
        out_shape=jax.ShapeDtypeStruct((A_WIDTH, batch * seq), BF16),
        compiler_params=pltpu.CompilerParams(
            dimension_semantics=("arbitrary", "arbitrary"), vmem_limit_bytes=VMEM_LIMIT),
        name="swa",
    )(qvt, ka, ka, qvt, bias, sinks)


def _flash_kernel(qt_ref, k_ref, vt_ref, o_ref, m_sc, acc_sc, s0_sc, s1_sc, bm0_sc, bm1_sc, *, tq, nh):
    tk = tq // 2
    i = pl.program_id(2)
    m_sc[...] = jnp.full(m_sc.shape, NEG, F32)
    acc_sc[...] = jnp.zeros(acc_sc.shape, F32)
    s_bufs = (s0_sc, s1_sc)
    bm_bufs = (bm0_sc, bm1_sc)

    def scores(h, j, slot, diag_half):
        kb = k_ref[0, h, pl.ds(pl.multiple_of(j * tk, tk), tk), :]
        s = _dot(kb, qt_ref[0, h])
        if diag_half is not None:
            key = lax.broadcasted_iota(jnp.int32, (tk, tq), 0) + diag_half * tk
            qry = lax.broadcasted_iota(jnp.int32, (tk, tq), 1)
            s = jnp.where(key <= qry, s, NEG)
        s_bufs[slot][h] = s
        bm_bufs[slot][h] = jnp.max(s, axis=0, keepdims=True)

    def softmax_pv(h, j, slot):
        m_old = m_sc[h]
        m_new = jnp.maximum(m_old, bm_bufs[slot][h])
        p = jnp.exp2(s_bufs[slot][h] - m_new).astype(BF16)
        alpha = jnp.exp2(m_old - m_new)
        m_sc[h] = m_new
        acc_sc[h] = acc_sc[h] * alpha + _dot(vt_ref[0, h, j], p)

    def stage(j, slot, next_diag_half, has_next=True):
        for h in range(nh):
            if has_next:
                scores(h, j + 1, 1 - slot, next_diag_half)
            softmax_pv(h, j, slot)

    @pl.when(i == 0)
    def _():
        for h in range(nh):
            scores(h, 0, 0, 0)

    @pl.when(i > 0)
    def _():
        for h in range(nh):
            scores(h, 0, 0, None)

    def pair(jj, carry):
        stage(2 * jj, 0, None)
        stage(2 * jj + 1, 1, None)
        return carry

    lax.fori_loop(0, i - 1, pair, 0)

    @pl.when(i > 0)
    def _():
        stage(2 * i - 2, 0, None)
        stage(2 * i - 1, 1, 0)

    stage(2 * i, 0, 1)
    stage(2 * i + 1, 1, None, has_next=False)

    outs = []
    for h in range(nh):
        acc = acc_sc[h]
        outs.append(acc[:B_VDIM, :] / acc[B_VDIM:B_VDIM + 1, :])
    o_ref[0] = jnp.concatenate(outs, axis=0).T.astype(BF16)


def _flash(qt, k, vt, tq, nh):
    batch, heads, _, seq = qt.shape
    tk = tq // 2
    nq = seq // tq
    return pl.pallas_call(
        functools.partial(_flash_kernel, tq=tq, nh=nh),
        grid=(batch, heads // nh, nq),
        in_specs=[pl.BlockSpec((1, nh, QK_PAD, tq), lambda b, g, i: (b, g, 0, i)),
                  pl.BlockSpec((1, nh, seq, QK_PAD), lambda b, g, i: (b, g, 0, 0)),
                  pl.BlockSpec((1, nh, seq // tk, V_ROWS, tk), lambda b, g, i: (b, g, 0, 0, 0))],
        out_specs=pl.BlockSpec((1, tq, nh * B_VDIM), lambda b, g, i: (b, i, g)),
        out_shape=jax.ShapeDtypeStruct((batch, seq, B_WIDTH), BF16),
        scratch_shapes=[pltpu.VMEM((nh, 1, tq), F32), pltpu.VMEM((nh, V_ROWS, tq), F32),
                        pltpu.VMEM((nh, tk, tq), F32), pltpu.VMEM((nh, tk, tq), F32),
                        pltpu.VMEM((nh, 1, tq), F32), pltpu.VMEM((nh, 1, tq), F32)],
        compiler_params=pltpu.CompilerParams(
            dimension_semantics=("arbitrary", "arbitrary", "arbitrary"),
            vmem_limit_bytes=VMEM_LIMIT),
        name="flash",
    )(qt, k, vt)


def _final_kernel(x_ref, p_ref, oat_ref, za_ref, ob_ref, zb_ref, ga_ref, gb_ref,
                  woa_ref, wob_ref, wout_ref, gpl_ref, wpg_ref, wpp_ref, gpost_ref, out_ref):
    oa = oat_ref[...].astype(F32).T
    ya = _dot((oa * jax.nn.silu(za_ref[...].astype(F32))).astype(BF16), woa_ref[...])
    yb = _dot((ob_ref[...].astype(F32) * jax.nn.silu(zb_ref[...].astype(F32))).astype(BF16), wob_ref[...])
    merged = (jax.nn.sigmoid(ga_ref[...].astype(F32)) * ya
              + jax.nn.sigmoid(gb_ref[...].astype(F32)) * yb)
    x1 = x_ref[...] + _dot(merged.astype(BF16), wout_ref[...])
    gate = jax.nn.sigmoid(_dot((_row_rms(x1, D_MODEL) * gpl_ref[...]).astype(BF16), wpg_ref[...]))
    emb = _row_rms(_dot(p_ref[...].astype(BF16), wpp_ref[...]), D_MODEL) * gpost_ref[...]
    out_ref[...] = x1 + gate * emb


def _final(x2, p2, oat, za, ob, zb, ga, gb, woa, wob, wout, gpl, wpg, wpp, gpost, tm):
    t = x2.shape[0]
    row = lambda a: pl.BlockSpec((tm, a.shape[1]), lambda i: (i, 0))
    full = lambda a: pl.BlockSpec(a.shape, lambda i: (0,) * a.ndim)
    acts = (x2, p2, oat, za, ob, zb, ga, gb)
    consts = (woa, wob, wout, gpl, wpg, wpp, gpost)
    act_specs = [row(a) for a in acts]
    act_specs[2] = pl.BlockSpec((A_WIDTH, tm), lambda i: (0, i))
    return pl.pallas_call(
        _final_kernel,
        grid=(t // tm,),
        in_specs=act_specs + [full(a) for a in consts],
        out_specs=pl.BlockSpec((tm, D_MODEL), lambda i: (i, 0)),
        out_shape=jax.ShapeDtypeStruct((t, D_MODEL), F32),
        compiler_params=pltpu.CompilerParams(
            dimension_semantics=("arbitrary",), vmem_limit_bytes=VMEM_LIMIT),
        name="final",
    )(*acts, *consts)


def _head_indicator(width, head_dim):
    e = (np.arange(width)[:, None] // head_dim == np.arange(LANES)[None, :]).astype(np.float32)
    return jnp.asarray(e, BF16), jnp.asarray(e.T, BF16)


def kernel(x, p, positions, norm_g, w_in, a_q_norm, a_k_norm, a_sinks, rel_bias, w_o_a, b_cq_norm, w_uq, b_ckv_norm, w_uk, w_uv, b_q_norm, b_k_norm, b_kr_norm, w_o_b, w_out, ple_norm_g, w_ple_gate, w_ple_proj, ple_post_g):
    batch, seq, _ = x.shape
    depth = p.shape[0]
    t = batch * seq
    tm = 512
    blk = 512
    offs = np.concatenate([[0], np.cumsum(SPLIT_SIZES)])
    e_a, et_a = _head_indicator(A_KV_WIDTH, A_HEAD_DIM)
    e_b, et_b = _head_indicator(B_HEADS * B_NOPE, B_NOPE)
    log2e = math.log2(math.e)
    inv_freq = ROPE_THETA ** (-jnp.arange(0, B_ROPE, 2, dtype=F32) / B_ROPE)
    freq_c = inv_freq[:, None]
    freq_r = jnp.tile(inv_freq, LANES // (B_ROPE // 2))[None, :]
    pos = positions.reshape(t)
    band_bias = _band_bias(rel_bias)
    row = lambda v: v.astype(F32)[None, :]

    x2 = x.reshape(t, D_MODEL)
    for i in range(depth):
        w = w_in[i]
        cols = [w[:, offs[j]:offs[j + 1]] for j in range(len(SPLIT_SIZES))]
        cols[6] = jnp.pad(cols[6], ((0, 0), (0, LANES - B_ROPE)))
        wt = jnp.concatenate([cols[0], cols[2]], axis=1).T.astype(BF16)
        w1 = jnp.concatenate([cols[j] for j in (1, 3, 4, 5, 6, 7, 8, 9)], axis=1).astype(BF16)
        gq_a = (a_q_norm[i].astype(F32) * (A_HEAD_DIM ** -0.5 * log2e))[:, None]
        gk_a = jnp.tile(a_k_norm[i].astype(F32), A_KV_HEADS)[None, :]
        qvt, ka, za, lat, zb, ga, gb = _proj(x2, row(norm_g[i]), wt, w1, e_a, et_a, gq_a, gk_a, tm)

        gqn = (b_q_norm[i].astype(F32) * (B_QK_DIM ** -0.5 * log2e))[:, None]
        gkn = jnp.tile(b_k_norm[i].astype(F32), B_HEADS)[None, :]
        gkr = jnp.pad(b_kr_norm[i].astype(F32), (0, LANES - B_ROPE))[None, :]
        qt, kb, vt = _prep(lat, pos[:, None], pos[None, :], row(b_cq_norm[i]),
                           w_uq[i].T.astype(BF16), gqn, row(b_ckv_norm[i]), w_uk[i].astype(BF16),
                           w_uv[i].T.astype(BF16), e_b, et_b, gkn, gkr, freq_c, freq_r,
                           batch, seq, blk, blk // 2)

        sinks = jnp.repeat(a_sinks[i].astype(F32) * log2e, A_BLOCK).reshape(A_KV_HEADS, 1, A_GROUP * A_BLOCK)
        oat = _swa(qvt, ka.reshape(batch, seq, A_KV_WIDTH), band_bias, sinks, batch, seq, tm)
        ob = _flash(qt, kb, vt, blk, 2)

        x2 = _final(x2, p[i].reshape(t, PLE_DIM), oat, za,
                    ob.reshape(t, B_WIDTH), zb, ga, gb,
                    w_o_a[i].astype(BF16), w_o_b[i].astype(BF16), w_out[i].astype(BF16),
                    row(ple_norm_g[i]), w_ple_gate[i].astype(BF16), w_ple_proj[i].astype(BF16),
                    row(ple_post_g[i]), tm)
    return x2.reshape(batch, seq, D_MODEL)
```

```python
import functools
import math

import numpy as np
import jax
import jax.numpy as jnp
from jax import lax
from jax.experimental import pallas as pl
from jax.experimental.pallas import tpu as pltpu

F32 = jnp.float32
BF16 = jnp.bfloat16

D_MODEL = 1024
PLE_DIM = 256
EPS = 1e-6
NEG = -1e30

A_HEADS = 16
A_KV_HEADS = 2
A_HEAD_DIM = 64
A_WIDTH = A_HEADS * A_HEAD_DIM
A_KV_WIDTH = A_KV_HEADS * A_HEAD_DIM
A_GROUP = A_HEADS // A_KV_HEADS
WINDOW = 128
A_BLOCK = 128
N_BUCKETS = 32
MAX_DISTANCE = 128

B_HEADS = 16
B_Q_RANK = 256
B_KV_RANK = 128
B_NOPE = 64
B_ROPE = 32
B_QK_DIM = B_NOPE + B_ROPE
B_VDIM = 64
B_WIDTH = B_HEADS * B_VDIM
ROPE_THETA = 10000.0

SPLIT_SIZES = (A_WIDTH, A_KV_WIDTH, A_KV_WIDTH, A_WIDTH,
               B_Q_RANK, B_KV_RANK, B_ROPE, B_WIDTH,
               D_MODEL, D_MODEL)

LANES = 128
QK_PAD = 128
V_ROWS = 80
VMEM_LIMIT = 56 * 1024 * 1024

QVT_ROWS = A_WIDTH + A_KV_WIDTH
SWA_V_ROWS = A_HEAD_DIM + 16
C_KA = 0
C_ZA = C_KA + A_KV_WIDTH
C_LAT = C_ZA + A_WIDTH
LAT_W = B_Q_RANK + B_KV_RANK + LANES
C_ZB = C_LAT + LAT_W
C_GA = C_ZB + B_WIDTH
C_GB = C_GA + D_MODEL
C_END = C_GB + D_MODEL


def _dot(a, b):
    return jnp.dot(a, b, preferred_element_type=F32)


def _dot_nt(a, b):
    return lax.dot_general(a, b, (((1,), (1,)), ((), ())), preferred_element_type=F32)


def _split_bf16(v):
    hi = v.astype(BF16)
    lo = (v - hi.astype(F32)).astype(BF16)
    return hi, lo


def _seg_rms(z, e, et, inv_width):
    hi, lo = _split_bf16(z * z)
    ss = _dot(hi, e) + _dot(lo, e)
    r = lax.rsqrt(ss * inv_width + EPS)
    r_hi, r_lo = _split_bf16(r)
    return z * (_dot(r_hi, et) + _dot(r_lo, et))


def _row_rms(v, width):
    return v * lax.rsqrt(jnp.sum(v * v, axis=-1, keepdims=True) * (1.0 / width) + EPS)


def _proj_kernel(x_ref, g_ref, wt_ref, w_ref, e_ref, et_ref, gq_ref, gk_ref,
                 qvt_ref, ka_ref, za_ref, lat_ref, zb_ref, ga_ref, gb_ref):
    x = x_ref[...]
    h = (_row_rms(x, D_MODEL) * g_ref[...]).astype(BF16)

    qvt = _dot_nt(wt_ref[...], h)
    gq = gq_ref[...]
    for hd in range(A_HEADS):
        blk = qvt[hd * A_HEAD_DIM:(hd + 1) * A_HEAD_DIM, :]
        ms = jnp.sum(blk * blk, axis=0, keepdims=True) * (1.0 / A_HEAD_DIM)
        qvt_ref[hd * A_HEAD_DIM:(hd + 1) * A_HEAD_DIM, :] = (blk * lax.rsqrt(ms + EPS) * gq).astype(BF16)
    qvt_ref[A_WIDTH:, :] = qvt[A_WIDTH:, :].astype(BF16)

    def proj(c0, c1):
        return _dot(h, w_ref[:, c0:c1])

    k = proj(C_KA, C_ZA)
    ka_ref[...] = (_seg_rms(k, e_ref[...], et_ref[...], 1.0 / A_HEAD_DIM) * gk_ref[...]).astype(BF16)
    za_ref[...] = proj(C_ZA, C_LAT).astype(BF16)
    lat_ref[...] = proj(C_LAT, C_ZB)
    zb_ref[...] = proj(C_ZB, C_GA).astype(BF16)
    ga_ref[...] = proj(C_GA, C_GB).astype(BF16)
    gb_ref[...] = proj(C_GB, C_END).astype(BF16)


def _proj(x2, g, wt, w1, e, et, gq, gk, tm):
    t = x2.shape[0]
    row = lambda w: pl.BlockSpec((tm, w), lambda i: (i, 0))
    full = lambda a: pl.BlockSpec(a.shape, lambda i: (0,) * a.ndim)
    widths = (A_KV_WIDTH, A_WIDTH, LAT_W, B_WIDTH, D_MODEL, D_MODEL)
    dtypes = (BF16, BF16, F32, BF16, BF16, BF16)
    return pl.pallas_call(
        _proj_kernel,
        grid=(t // tm,),
        in_specs=[row(D_MODEL), full(g), full(wt), full(w1), full(e), full(et), full(gq), full(gk)],
        out_specs=[pl.BlockSpec((QVT_ROWS, tm), lambda i: (0, i))] + [row(w) for w in widths],
        out_shape=[jax.ShapeDtypeStruct((QVT_ROWS, t), BF16)]
        + [jax.ShapeDtypeStruct((t, w), d) for w, d in zip(widths, dtypes)],
        compiler_params=pltpu.CompilerParams(
            dimension_semantics=("arbitrary",), vmem_limit_bytes=VMEM_LIMIT),
        name="proj",
    )(x2, g, wt, w1, e, et, gq, gk)


def _prep_kernel(lat_ref, posc_ref, posr_ref, gcq_ref, wuqt_ref, gqn_ref, gckv_ref, wuk_ref,
                 wuvt_ref, e_ref, et_ref, gkn_ref, gkr_ref, freqc_ref, freqr_ref,
                 qt_ref, k_ref, vt_ref):
    lat = lat_ref[...]
    tm = lat.shape[0]
    cq = (_row_rms(lat[:, :B_Q_RANK], B_Q_RANK) * gcq_ref[...]).astype(BF16)
    qt = _dot_nt(wuqt_ref[...], cq)
    ang_r = freqc_ref[...] * posr_ref[...].astype(F32)
    cos_r = jnp.cos(ang_r)
    sin_r = jnp.sin(ang_r)
    gqn = gqn_ref[...]
    half = B_ROPE // 2
    zeros_q = jnp.zeros((QK_PAD - B_QK_DIM, tm), BF16)
    for h in range(B_HEADS):
        blk = qt[h * B_QK_DIM:(h + 1) * B_QK_DIM, :]
        ms = jnp.sum(blk * blk, axis=0, keepdims=True) * (1.0 / B_QK_DIM)
        qn = blk * lax.rsqrt(ms + EPS) * gqn
        x1 = qn[B_NOPE:B_NOPE + half, :]
        x2 = qn[B_NOPE + half:, :]
        qt_ref[0, h, 0:B_NOPE, :] = qn[:B_NOPE, :].astype(BF16)
        qt_ref[0, h, B_NOPE:B_NOPE + half, :] = (x1 * cos_r - x2 * sin_r).astype(BF16)
        qt_ref[0, h, B_NOPE + half:B_QK_DIM, :] = (x2 * cos_r + x1 * sin_r).astype(BF16)
        qt_ref[0, h, B_QK_DIM:, :] = zeros_q

    ckv = (_row_rms(lat[:, B_Q_RANK:B_Q_RANK + B_KV_RANK], B_KV_RANK) * gckv_ref[...]).astype(BF16)
    kn = _seg_rms(_dot(ckv, wuk_ref[...]), e_ref[...], et_ref[...], 1.0 / B_NOPE) * gkn_ref[...]
    vt = _dot_nt(wuvt_ref[...], ckv)

    kr = lat[:, B_Q_RANK + B_KV_RANK:]
    kr = kr * lax.rsqrt(jnp.sum(kr * kr, axis=-1, keepdims=True) * (1.0 / B_ROPE) + EPS) * gkr_ref[...]
    ang_c = posc_ref[...].astype(F32) * freqr_ref[...]
    lane = lax.broadcasted_iota(jnp.int32, (tm, LANES), 1)
    rot = jnp.where(lane < half, -pltpu.roll(kr, LANES - half, 1),
                    jnp.where(lane < B_ROPE, pltpu.roll(kr, half, 1), 0.0))
    kr = kr * jnp.cos(ang_c) + rot * jnp.sin(ang_c)
    kr = pltpu.roll(kr, B_NOPE, 1)
    lo_half = lane < B_NOPE
    tk = vt_ref.shape[-1]
    ones_rows = (lax.broadcasted_iota(jnp.int32, (V_ROWS - B_VDIM, tk), 0) == 0).astype(BF16)
    for h2 in range(B_HEADS // 2):
        pair = kn[:, h2 * LANES:(h2 + 1) * LANES]
        k_ref[0, 2 * h2] = jnp.where(lo_half, pair, kr).astype(BF16)
        k_ref[0, 2 * h2 + 1] = jnp.where(lo_half, pltpu.roll(pair, B_NOPE, 1), kr).astype(BF16)
    for h in range(B_HEADS):
        for c in range(tm // tk):
            vt_ref[0, h, c, 0:B_VDIM, :] = vt[h * B_VDIM:(h + 1) * B_VDIM, c * tk:(c + 1) * tk].astype(BF16)
            vt_ref[0, h, c, B_VDIM:, :] = ones_rows


def _prep(lat, pos_col, pos_row, gcq, wuqt, gqn, gckv, wuk, wuvt, e, et, gkn, gkr, freq_c, freq_r,
          batch, seq, tm, tk):
    nt = seq // tm
    nc = tm // tk
    full = lambda a: pl.BlockSpec(a.shape, lambda b, i: (0,) * a.ndim)
    return pl.pallas_call(
        _prep_kernel,
        grid=(batch, nt),
        in_specs=[pl.BlockSpec((tm, LAT_W), lambda b, i: (b * nt + i, 0)),
                  pl.BlockSpec((tm, 1), lambda b, i: (b * nt + i, 0)),
                  pl.BlockSpec((1, tm), lambda b, i: (0, b * nt + i)),
                  full(gcq), full(wuqt), full(gqn), full(gckv), full(wuk), full(wuvt),
                  full(e), full(et), full(gkn), full(gkr), full(freq_c), full(freq_r)],
        out_specs=[pl.BlockSpec((1, B_HEADS, QK_PAD, tm), lambda b, i: (b, 0, 0, i)),
                   pl.BlockSpec((1, B_HEADS, tm, QK_PAD), lambda b, i: (b, 0, i, 0)),
                   pl.BlockSpec((1, B_HEADS, nc, V_ROWS, tk), lambda b, i: (b, 0, i, 0, 0))],
        out_shape=[jax.ShapeDtypeStruct((batch, B_HEADS, QK_PAD, seq), BF16),
                   jax.ShapeDtypeStruct((batch, B_HEADS, seq, QK_PAD), BF16),
                   jax.ShapeDtypeStruct((batch, B_HEADS, nt * nc, V_ROWS, tk), BF16)],
        compiler_params=pltpu.CompilerParams(
            dimension_semantics=("arbitrary", "arbitrary"), vmem_limit_bytes=VMEM_LIMIT),
        name="prep",
    )(lat, pos_col, pos_row, gcq, wuqt, gqn, gckv, wuk, wuvt, e, et, gkn, gkr, freq_c, freq_r)


def _band_bias_kernel(rbt_ref, onehot_ref, o_ref):
    rbt = rbt_ref[...]
    onehot = onehot_ref[...]
    t = jnp.zeros((A_HEADS, WINDOW), F32)
    for b in range(N_BUCKETS):
        t = t + rbt[:, b:b + 1] * onehot[b:b + 1, :]
    t = t * math.log2(math.e)
    band = 2 * A_BLOCK
    key = lax.broadcasted_iota(jnp.int32, (band, band), 0)
    neg = jnp.full((1, A_BLOCK), NEG, F32)
    for hd in range(A_HEADS):
        x = jnp.broadcast_to(jnp.concatenate([neg, t[hd:hd + 1, :]], axis=1), (band, band))
        for bit in range(8):
            x = jnp.where((key >> bit) & 1 == 1, pltpu.roll(x, 1 << bit, 1), x)
        g, hh = divmod(hd, A_GROUP)
        o_ref[g, :, hh * A_BLOCK:(hh + 1) * A_BLOCK] = x[:, :A_BLOCK]


def _band_bias(rel_bias):
    dist = np.arange(WINDOW)
    max_exact = N_BUCKETS // 2
    large = max_exact + (np.log(np.maximum(dist, 1).astype(np.float32) / max_exact)
                         / math.log(MAX_DISTANCE / max_exact) * (N_BUCKETS - max_exact)).astype(np.int32)
    bucket = np.where(dist < max_exact, dist, np.minimum(large, N_BUCKETS - 1))
    onehot = jnp.asarray(np.arange(N_BUCKETS)[:, None] == bucket[None, :], F32)
    return pl.pallas_call(
        _band_bias_kernel,
        out_shape=jax.ShapeDtypeStruct((A_KV_HEADS, 2 * A_BLOCK, A_GROUP * A_BLOCK), F32),
        name="band_bias",
    )(rel_bias.astype(F32).T, onehot)


def _swa_kernel(qv_ref, k_ref, kp_ref, vp_ref, bias_ref, sink_ref, o_ref, *, nsub):
    first = pl.program_id(1) == 0
    band = 2 * A_BLOCK
    width = A_GROUP * A_BLOCK
    pad = jnp.logical_and(first, lax.broadcasted_iota(jnp.int32, (band, width), 0) < A_BLOCK)
    zeros_q = jnp.zeros((A_HEAD_DIM, width), BF16)
    ones_rows = (lax.broadcasted_iota(jnp.int32, (SWA_V_ROWS - A_HEAD_DIM, band), 0) == 0).astype(BF16)

    def scores(sb, g):
        r0 = sb * A_BLOCK
        if sb == 0:
            kb = jnp.concatenate([kp_ref[0], k_ref[0, 0:A_BLOCK, :]], axis=0)
        else:
            kb = k_ref[0, r0 - A_BLOCK:r0 + A_BLOCK, :]
        tiles = [qv_ref[(g * A_GROUP + hh) * A_HEAD_DIM:(g * A_GROUP + hh + 1) * A_HEAD_DIM, r0:r0 + A_BLOCK]
                 for hh in range(A_GROUP)]
        qt = jnp.concatenate(tiles, axis=1)
        rhs = jnp.concatenate([qt, zeros_q] if g == 0 else [zeros_q, qt], axis=0)
        s = _dot(kb, rhs) + bias_ref[g]
        if sb == 0:
            s = jnp.where(pad, NEG, s)
        return s

    def finish(sb, g, s):
        r0 = sb * A_BLOCK
        sink = sink_ref[g]
        m = jnp.maximum(jnp.max(s, axis=0, keepdims=True), sink)
        e = jnp.exp2(s - m).astype(BF16)
        v0 = A_WIDTH + g * A_HEAD_DIM
        if sb == 0:
            vt = jnp.concatenate([vp_ref[g * A_HEAD_DIM:(g + 1) * A_HEAD_DIM, :],
                                  qv_ref[v0:v0 + A_HEAD_DIM, 0:A_BLOCK]], axis=1)
        else:
            vt = qv_ref[v0:v0 + A_HEAD_DIM, r0 - A_BLOCK:r0 + A_BLOCK]
        acc = _dot(jnp.concatenate([vt, ones_rows], axis=0), e)
        den = acc[A_HEAD_DIM:A_HEAD_DIM + 1, :] + jnp.exp2(sink - m)
        o = (acc[:A_HEAD_DIM, :] / den).astype(BF16)
        for hh in range(A_GROUP):
            hd = g * A_GROUP + hh
            o_ref[hd * A_HEAD_DIM:(hd + 1) * A_HEAD_DIM, r0:r0 + A_BLOCK] = o[:, hh * A_BLOCK:(hh + 1) * A_BLOCK]

    units = [(sb, g) for sb in range(nsub) for g in range(A_KV_HEADS)]
    s_next = scores(*units[0])
    for n, unit in enumerate(units):
        s_cur = s_next
        if n + 1 < len(units):
            s_next = scores(*units[n + 1])
        finish(*unit, s_cur)


def _swa(qvt, ka, bias, sinks, batch, seq, tq):
    nsub = tq // A_BLOCK
    nt = seq // tq
    vrow = A_WIDTH // A_KV_WIDTH
    return pl.pallas_call(
        functools.partial(_swa_kernel, nsub=nsub),
        grid=(batch, nt),
        in_specs=[pl.BlockSpec((QVT_ROWS, tq), lambda b, i: (0, b * nt + i)),
                  pl.BlockSpec((1, tq, A_KV_WIDTH), lambda b, i: (b, i, 0)),
                  pl.BlockSpec((1, A_BLOCK, A_KV_WIDTH), lambda b, i: (b, jnp.maximum(i * nsub - 1, 0), 0)),
                  pl.BlockSpec((A_KV_WIDTH, A_BLOCK),
                               lambda b, i: (vrow, jnp.maximum((b * nt + i) * nsub - 1, 0))),
                  pl.BlockSpec(bias.shape, lambda b, i: (0, 0, 0)),
                  pl.BlockSpec(sinks.shape, lambda b, i: (0, 0, 0))],
        out_specs=pl.BlockSpec((A_WIDTH, tq), lambda b, i: (0, b * nt + i)),
        out_shape=jax.ShapeDtypeStruct((A_WIDTH, batch * seq), BF16),
        compiler_params=pltpu.CompilerParams(
            dimension_semantics=("arbitrary", "arbitrary"), vmem_limit_bytes=VMEM_LIMIT),
        name="swa",
    )(qvt, ka, ka, qvt, bias, sinks)


def _flash_kernel(qt_ref, k_ref, vt_ref, o_ref, m_sc, acc_sc, s0_sc, s1_sc, bm0_sc, bm1_sc, *, tq, nh):
    tk = tq // 2
    i = pl.program_id(2)
    m_sc[...] = jnp.full(m_sc.shape, NEG, F32)
    acc_sc[...] = jnp.zeros(acc_sc.shape, F32)
    s_bufs = (s0_sc, s1_sc)
    bm_bufs = (bm0_sc, bm1_sc)

    def scores(h, j, slot, diag_half):
        kb = k_ref[0, h, pl.ds(pl.multiple_of(j * tk, tk), tk), :]
        s = _dot(kb, qt_ref[0, h])
        if diag_half is not None:
            key = lax.broadcasted_iota(jnp.int32, (tk, tq), 0) + diag_half * tk
            qry = lax.broadcasted_iota(jnp.int32, (tk, tq), 1)
            s = jnp.where(key <= qry, s, NEG)
        s_bufs[slot][h] = s
        bm_bufs[slot][h] = jnp.max(s, axis=0, keepdims=True)

    def softmax_pv(h, j, slot):
        m_old = m_sc[h]
        m_new = jnp.maximum(m_old, bm_bufs[slot][h])
        p = jnp.exp2(s_bufs[slot][h] - m_new).astype(BF16)
        alpha = jnp.exp2(m_old - m_new)
        m_sc[h] = m_new
        acc_sc[h] = acc_sc[h] * alpha + _dot(vt_ref[0, h, j], p)

    def stage(j, slot, next_diag_half, has_next=True):
        for h in range(nh):
            if has_next:
                scores(h, j + 1, 1 - slot, next_diag_half)
            softmax_pv(h, j, slot)

    @pl.when(i == 0)
    def _():
        for h in range(nh):
            scores(h, 0, 0, 0)

    @pl.when(i > 0)
    def _():
        for h in range(nh):
            scores(h, 0, 0, None)

    def pair(jj, carry):
        stage(2 * jj, 0, None)
        stage(2 * jj + 1, 1, None)
        return carry

    lax.fori_loop(0, i - 1, pair, 0)

    @pl.when(i > 0)
    def _():
        stage(2 * i - 2, 0, None)
        stage(2 * i - 1, 1, 0)

    stage(2 * i, 0, 1)
    stage(2 * i + 1, 1, None, has_next=False)

    outs = []
    for h in range(nh):
        acc = acc_sc[h]
        outs.append(acc[:B_VDIM, :] / acc[B_VDIM:B_VDIM + 1, :])
    o_ref[0] = jnp.concatenate(outs, axis=0).T.astype(BF16)


def _flash(qt, k, vt, tq, nh):
    batch, heads, _, seq = qt.shape
    tk = tq // 2
    nq = seq // tq
    return pl.pallas_call(
        functools.partial(_flash_kernel, tq=tq, nh=nh),
        grid=(batch, heads // nh, nq),
        in_specs=[pl.BlockSpec((1, nh, QK_PAD, tq), lambda b, g, i: (b, g, 0, i)),
                  pl.BlockSpec((1, nh, seq, QK_PAD), lambda b, g, i: (b, g, 0, 0)),
                  pl.BlockSpec((1, nh, seq // tk, V_ROWS, tk), lambda b, g, i: (b, g, 0, 0, 0))],
        out_specs=pl.BlockSpec((1, tq, nh * B_VDIM), lambda b, g, i: (b, i, g)),
        out_shape=jax.ShapeDtypeStruct((batch, seq, B_WIDTH), BF16),
        scratch_shapes=[pltpu.VMEM((nh, 1, tq), F32), pltpu.VMEM((nh, V_ROWS, tq), F32),
                        pltpu.VMEM((nh, tk, tq), F32), pltpu.VMEM((nh, tk, tq), F32),
                        pltpu.VMEM((nh, 1, tq), F32), pltpu.VMEM((nh, 1, tq), F32)],
        compiler_params=pltpu.CompilerParams(
            dimension_semantics=("arbitrary", "arbitrary", "arbitrary"),
            vmem_limit_bytes=VMEM_LIMIT),
        name="flash",
    )(qt, k, vt)


def _final_kernel(x_ref, p_ref, oat_ref, za_ref, ob_ref, zb_ref, ga_ref, gb_ref,
                  woa_ref, wob_ref, wout_ref, gpl_ref, wpg_ref, wpp_ref, gpost_ref, out_ref):
    oa = oat_ref[...].astype(F32).T
    ya = _dot((oa * jax.nn.silu(za_ref[...].astype(F32))).astype(BF16), woa_ref[...])
    yb = _dot((ob_ref[...].astype(F32) * jax.nn.silu(zb_ref[...].astype(F32))).astype(BF16), wob_ref[...])
    merged = (jax.nn.sigmoid(ga_ref[...].astype(F32)) * ya
              + jax.nn.sigmoid(gb_ref[...].astype(F32)) * yb)
    x1 = x_ref[...] + _dot(merged.astype(BF16), wout_ref[...])
    gate = jax.nn.sigmoid(_dot((_row_rms(x1, D_MODEL) * gpl_ref[...]).astype(BF16), wpg_ref[...]))
    emb = _row_rms(_dot(p_ref[...].astype(BF16), wpp_ref[...]), D_MODEL) * gpost_ref[...]
    out_ref[...] = x1 + gate * emb


def _final(x2, p2, oat, za, ob, zb, ga, gb, woa, wob, wout, gpl, wpg, wpp, gpost, tm):
    t = x2.shape[0]
    row = lambda a: pl.BlockSpec((tm, a.shape[1]), lambda i: (i, 0))
    full = lambda a: pl.BlockSpec(a.shape, lambda i: (0,) * a.ndim)
    acts = (x2, p2, oat, za, ob, zb, ga, gb)
    consts = (woa, wob, wout, gpl, wpg, wpp, gpost)
    act_specs = [row(a) for a in acts]
    act_specs[2] = pl.BlockSpec((A_WIDTH, tm), lambda i: (0, i))
    return pl.pallas_call(
        _final_kernel,
        grid=(t // tm,),
        in_specs=act_specs + [full(a) for a in consts],
        out_specs=pl.BlockSpec((tm, D_MODEL), lambda i: (i, 0)),
        out_shape=jax.ShapeDtypeStruct((t, D_MODEL), F32),
        compiler_params=pltpu.CompilerParams(
            dimension_semantics=("arbitrary",), vmem_limit_bytes=VMEM_LIMIT),
        name="final",
    )(*acts, *consts)


def _head_indicator(width, head_dim):
    e = (np.arange(width)[:, None] // head_dim == np.arange(LANES)[None, :]).astype(np.float32)
    return jnp.asarray(e, BF16), jnp.asarray(e.T, BF16)


def kernel(x, p, positions, norm_g, w_in, a_q_norm, a_k_norm, a_sinks, rel_bias, w_o_a, b_cq_norm, w_uq, b_ckv_norm, w_uk, w_uv, b_q_norm, b_k_norm, b_kr_norm, w_o_b, w_out, ple_norm_g, w_ple_gate, w_ple_proj, ple_post_g):
    batch, seq, _ = x.shape
    depth = p.shape[0]
    t = batch * seq
    tm = 512
    blk = 512
    offs = np.concatenate([[0], np.cumsum(SPLIT_SIZES)])
    e_a, et_a = _head_indicator(A_KV_WIDTH, A_HEAD_DIM)
    e_b, et_b = _head_indicator(B_HEADS * B_NOPE, B_NOPE)
    log2e = math.log2(math.e)
    inv_freq = ROPE_THETA ** (-jnp.arange(0, B_ROPE, 2, dtype=F32) / B_ROPE)
    freq_c = inv_freq[:, None]
    freq_r = jnp.tile(inv_freq, LANES // (B_ROPE // 2))[None, :]
    pos = positions.reshape(t)
    band_bias = _band_bias(rel_bias)
    row = lambda v: v.astype(F32)[None, :]

    x2 = x.reshape(t, D_MODEL)
    for i in range(depth):
        w = w_in[i]
        cols = [w[:, offs[j]:offs[j + 1]] for j in range(len(SPLIT_SIZES))]
        cols[6] = jnp.pad(cols[6], ((0, 0), (0, LANES - B_ROPE)))
        wt = jnp.concatenate([cols[0], cols[2]], axis=1).T.astype(BF16)
        w1 = jnp.concatenate([cols[j] for j in (1, 3, 4, 5, 6, 7, 8, 9)], axis=1).astype(BF16)
        gq_a = (a_q_norm[i].astype(F32) * (A_HEAD_DIM ** -0.5 * log2e))[:, None]
        gk_a = jnp.tile(a_k_norm[i].astype(F32), A_KV_HEADS)[None, :]
        qvt, ka, za, lat, zb, ga, gb = _proj(x2, row(norm_g[i]), wt, w1, e_a, et_a, gq_a, gk_a, tm)

        gqn = (b_q_norm[i].astype(F32) * (B_QK_DIM ** -0.5 * log2e))[:, None]
        gkn = jnp.tile(b_k_norm[i].astype(F32), B_HEADS)[None, :]
        gkr = jnp.pad(b_kr_norm[i].astype(F32), (0, LANES - B_ROPE))[None, :]
        qt, kb, vt = _prep(lat, pos[:, None], pos[None, :], row(b_cq_norm[i]),
                           w_uq[i].T.astype(BF16), gqn, row(b_ckv_norm[i]), w_uk[i].astype(BF16),
                           w_uv[i].T.astype(BF16), e_b, et_b, gkn, gkr, freq_c, freq_r,
                           batch, seq, blk, blk // 2)

        sinks = jnp.repeat(a_sinks[i].astype(F32) * log2e, A_BLOCK).reshape(A_KV_HEADS, 1, A_GROUP * A_BLOCK)
        oat = _swa(qvt, ka.reshape(batch, seq, A_KV_WIDTH), band_bias, sinks, batch, seq, tm)
        ob = _flash(qt, kb, vt, blk, 4)

        x2 = _final(x2, p[i].reshape(t, PLE_DIM), oat, za,
                    ob.reshape(t, B_WIDTH), zb, ga, gb,
                    w_o_a[i].astype(BF16), w_o_b[i].astype(BF16), w_out[i].astype(BF16),
                    row(ple_norm_g[i]), w_ple_gate[i].astype(BF16), w_ple_proj[i].astype(BF16),
                    row(ple_post_g[i]), tm)
    return x2.reshape(batch, seq, D_MODEL)
```

```python
import functools
import math

import numpy as np
import jax
import jax.numpy as jnp
from jax import lax
from jax.experimental import pallas as pl
from jax.experimental.pallas import tpu as pltpu

F32 = jnp.float32
BF16 = jnp.bfloat16

D_MODEL = 1024
PLE_DIM = 256
EPS = 1e-6
NEG = -1e30

A_HEADS = 16
A_KV_HEADS = 2
A_HEAD_DIM = 64
A_WIDTH = A_HEADS * A_HEAD_DIM
A_KV_WIDTH = A_KV_HEADS * A_HEAD_DIM
A_GROUP = A_HEADS // A_KV_HEADS
WINDOW = 128
A_BLOCK = 128
N_BUCKETS = 32
MAX_DISTANCE = 128

B_HEADS = 16
B_Q_RANK = 256
B_KV_RANK = 128
B_NOPE = 64
B_ROPE = 32
B_QK_DIM = B_NOPE + B_ROPE
B_VDIM = 64
B_WIDTH = B_HEADS * B_VDIM
ROPE_THETA = 10000.0

SPLIT_SIZES = (A_WIDTH, A_KV_WIDTH, A_KV_WIDTH, A_WIDTH,
               B_Q_RANK, B_KV_RANK, B_ROPE, B_WIDTH,
               D_MODEL, D_MODEL)

LANES = 128
QK_PAD = 128
V_ROWS = 80
VMEM_LIMIT = 56 * 1024 * 1024

QVT_ROWS = A_WIDTH + A_KV_WIDTH
R_KA = QVT_ROWS
R_KR = R_KA + A_KV_WIDTH
WT_ROWS = R_KR + B_ROPE
SWA_V_ROWS = A_HEAD_DIM + 16
C_ZA = 0
C_LAT = C_ZA + A_WIDTH
LAT_W = B_Q_RANK + B_KV_RANK
C_ZB = C_LAT + LAT_W
C_GA = C_ZB + B_WIDTH
C_GB = C_GA + D_MODEL
C_END = C_GB + D_MODEL


def _dot(a, b):
    return jnp.dot(a, b, preferred_element_type=F32)


def _dot_nt(a, b):
    return lax.dot_general(a, b, (((1,), (1,)), ((), ())), preferred_element_type=F32)


def _row_rms(v, width):
    return v * lax.rsqrt(jnp.sum(v * v, axis=-1, keepdims=True) * (1.0 / width) + EPS)


def _col_rms(v):
    return v * lax.rsqrt(jnp.sum(v * v, axis=0, keepdims=True) * (1.0 / v.shape[0]) + EPS)


def _proj_kernel(x_ref, pos_ref, g_ref, wt_ref, w_ref, gq_ref, gk_ref,
                 gcq_ref, wuqt_ref, gqn_ref, gckv_ref, wukt_ref, wuvt_ref, gkn_ref, gkr_ref, freq_ref,
                 qvt_ref, ka_ref, za_ref, zb_ref, ga_ref, gb_ref, qt_ref, k_ref, vt_ref):
    x = x_ref[...]
    tm = x.shape[0]
    h = (_row_rms(x, D_MODEL) * g_ref[...]).astype(BF16)

    wt_out = _dot_nt(wt_ref[...], h)
    gq = gq_ref[...]
    for hd in range(A_HEADS):
        blk = wt_out[hd * A_HEAD_DIM:(hd + 1) * A_HEAD_DIM, :]
        qvt_ref[hd * A_HEAD_DIM:(hd + 1) * A_HEAD_DIM, :] = (_col_rms(blk) * gq).astype(BF16)
    qvt_ref[A_WIDTH:, :] = wt_out[A_WIDTH:QVT_ROWS, :].astype(BF16)
    gk = gk_ref[...]
    kat = [_col_rms(wt_out[R_KA + g * A_HEAD_DIM:R_KA + (g + 1) * A_HEAD_DIM, :]) * gk
           for g in range(A_KV_HEADS)]
    ka_ref[...] = jnp.concatenate(kat, axis=0).T.astype(BF16)
    krt = wt_out[R_KR:, :]

    def proj(c0, c1):
        return _dot(h, w_ref[:, c0:c1])

    za_ref[...] = proj(C_ZA, C_LAT).astype(BF16)
    lat = proj(C_LAT, C_ZB)
    zb_ref[...] = proj(C_ZB, C_GA).astype(BF16)
    ga_ref[...] = proj(C_GA, C_GB).astype(BF16)
    gb_ref[...] = proj(C_GB, C_END).astype(BF16)

    cq = (_row_rms(lat[:, :B_Q_RANK], B_Q_RANK) * gcq_ref[...]).astype(BF16)
    qt = _dot_nt(wuqt_ref[...], cq)
    ang = freq_ref[...] * pos_ref[...].astype(F32)
    cos = jnp.cos(ang)
    sin = jnp.sin(ang)
    gqn = gqn_ref[...]
    half = B_ROPE // 2
    zeros_q = jnp.zeros((QK_PAD - B_QK_DIM, tm), BF16)
    for hd in range(B_HEADS):
        qn = _col_rms(qt[hd * B_QK_DIM:(hd + 1) * B_QK_DIM, :]) * gqn
        x1 = qn[B_NOPE:B_NOPE + half, :]
        x2 = qn[B_NOPE + half:, :]
        qt_ref[0, hd, 0:B_NOPE, :] = qn[:B_NOPE, :].astype(BF16)
        qt_ref[0, hd, B_NOPE:B_NOPE + half, :] = (x1 * cos - x2 * sin).astype(BF16)
        qt_ref[0, hd, B_NOPE + half:B_QK_DIM, :] = (x2 * cos + x1 * sin).astype(BF16)
        qt_ref[0, hd, B_QK_DIM:, :] = zeros_q

    ckv = (_row_rms(lat[:, B_Q_RANK:], B_KV_RANK) * gckv_ref[...]).astype(BF16)
    knt = _dot_nt(wukt_ref[...], ckv)
    vt = _dot_nt(wuvt_ref[...], ckv)

    krn = _col_rms(krt) * gkr_ref[...]
    x1 = krn[:half, :]
    x2 = krn[half:, :]
    k_tail = jnp.concatenate([x1 * cos - x2 * sin, x2 * cos + x1 * sin,
                              jnp.zeros((QK_PAD - B_QK_DIM, tm), F32)], axis=0)
    gkn = gkn_ref[...]
    for hd in range(B_HEADS):
        kn = _col_rms(knt[hd * B_NOPE:(hd + 1) * B_NOPE, :]) * gkn
        k_ref[0, hd] = jnp.concatenate([kn, k_tail], axis=0).T.astype(BF16)
    tk = vt_ref.shape[-1]
    ones_rows = (lax.broadcasted_iota(jnp.int32, (V_ROWS - B_VDIM, tk), 0) == 0).astype(BF16)
    for hd in range(B_HEADS):
        for c in range(tm // tk):
            vt_ref[0, hd, c, 0:B_VDIM, :] = vt[hd * B_VDIM:(hd + 1) * B_VDIM, c * tk:(c + 1) * tk].astype(BF16)
            vt_ref[0, hd, c, B_VDIM:, :] = ones_rows


def _proj(x2, pos_row, consts, batch, seq, tm, tk):
    t = x2.shape[0]
    nt = seq // tm
    nc = tm // tk
    row = lambda w: pl.BlockSpec((tm, w), lambda i: (i, 0))
    full = lambda a: pl.BlockSpec(a.shape, lambda i: (0,) * a.ndim)
    widths = (A_KV_WIDTH, A_WIDTH, B_WIDTH, D_MODEL, D_MODEL)
    return pl.pallas_call(
        _proj_kernel,
        grid=(t // tm,),
        in_specs=[row(D_MODEL), pl.BlockSpec((1, tm), lambda i: (0, i))] + [full(a) for a in consts],
        out_specs=[pl.BlockSpec((QVT_ROWS, tm), lambda i: (0, i))] + [row(w) for w in widths]
        + [pl.BlockSpec((1, B_HEADS, QK_PAD, tm), lambda i: (i // nt, 0, 0, i % nt)),
           pl.BlockSpec((1, B_HEADS, tm, QK_PAD), lambda i: (i // nt, 0, i % nt, 0)),
           pl.BlockSpec((1, B_HEADS, nc, V_ROWS, tk), lambda i: (i // nt, 0, i % nt, 0, 0))],
        out_shape=[jax.ShapeDtypeStruct((QVT_ROWS, t), BF16)]
        + [jax.ShapeDtypeStruct((t, w), BF16) for w in widths]
        + [jax.ShapeDtypeStruct((batch, B_HEADS, QK_PAD, seq), BF16),
           jax.ShapeDtypeStruct((batch, B_HEADS, seq, QK_PAD), BF16),
           jax.ShapeDtypeStruct((batch, B_HEADS, nt * nc, V_ROWS, tk), BF16)],
        compiler_params=pltpu.CompilerParams(
            dimension_semantics=("arbitrary",), vmem_limit_bytes=VMEM_LIMIT),
        name="proj",
    )(x2, pos_row, *consts)


def _band_bias_kernel(rbt_ref, onehot_ref, o_ref):
    rbt = rbt_ref[...]
    onehot = onehot_ref[...]
    t = jnp.zeros((A_HEADS, WINDOW), F32)
    for b in range(N_BUCKETS):
        t = t + rbt[:, b:b + 1] * onehot[b:b + 1, :]
    t = t * math.log2(math.e)
    band = 2 * A_BLOCK
    key = lax.broadcasted_iota(jnp.int32, (band, band), 0)
    neg = jnp.full((1, A_BLOCK), NEG, F32)
    for hd in range(A_HEADS):
        x = jnp.broadcast_to(jnp.concatenate([neg, t[hd:hd + 1, :]], axis=1), (band, band))
        for bit in range(8):
            x = jnp.where((key >> bit) & 1 == 1, pltpu.roll(x, 1 << bit, 1), x)
        g, hh = divmod(hd, A_GROUP)
        o_ref[g, :, hh * A_BLOCK:(hh + 1) * A_BLOCK] = x[:, :A_BLOCK]


def _band_bias(rel_bias):
    dist = np.arange(WINDOW)
    max_exact = N_BUCKETS // 2
    large = max_exact + (np.log(np.maximum(dist, 1).astype(np.float32) / max_exact)
                         / math.log(MAX_DISTANCE / max_exact) * (N_BUCKETS - max_exact)).astype(np.int32)
    bucket = np.where(dist < max_exact, dist, np.minimum(large, N_BUCKETS - 1))
    onehot = jnp.asarray(np.arange(N_BUCKETS)[:, None] == bucket[None, :], F32)
    return pl.pallas_call(
        _band_bias_kernel,
        out_shape=jax.ShapeDtypeStruct((A_KV_HEADS, 2 * A_BLOCK, A_GROUP * A_BLOCK), F32),
        name="band_bias",
    )(rel_bias.astype(F32).T, onehot)


def _swa_kernel(qv_ref, k_ref, kp_ref, vp_ref, bias_ref, sink_ref, o_ref, *, nsub):
    first = pl.program_id(1) == 0
    band = 2 * A_BLOCK
    width = A_GROUP * A_BLOCK
    pad = jnp.logical_and(first, lax.broadcasted_iota(jnp.int32, (band, width), 0) < A_BLOCK)
    zeros_q = jnp.zeros((A_HEAD_DIM, width), BF16)
    ones_rows = (lax.broadcasted_iota(jnp.int32, (SWA_V_ROWS - A_HEAD_DIM, band), 0) == 0).astype(BF16)

    def scores(sb, g):
        r0 = sb * A_BLOCK
        if sb == 0:
            kb = jnp.concatenate([kp_ref[0], k_ref[0, 0:A_BLOCK, :]], axis=0)
        else:
            kb = k_ref[0, r0 - A_BLOCK:r0 + A_BLOCK, :]
        tiles = [qv_ref[(g * A_GROUP + hh) * A_HEAD_DIM:(g * A_GROUP + hh + 1) * A_HEAD_DIM, r0:r0 + A_BLOCK]
                 for hh in range(A_GROUP)]
        qt = jnp.concatenate(tiles, axis=1)
        rhs = jnp.concatenate([qt, zeros_q] if g == 0 else [zeros_q, qt], axis=0)
        s = _dot(kb, rhs) + bias_ref[g]
        if sb == 0:
            s = jnp.where(pad, NEG, s)
        return s

    def finish(sb, g, s):
        r0 = sb * A_BLOCK
        sink = sink_ref[g]
        m = jnp.maximum(jnp.max(s, axis=0, keepdims=True), sink)
        e = jnp.exp2(s - m).astype(BF16)
        v0 = A_WIDTH + g * A_HEAD_DIM
        if sb == 0:
            vt = jnp.concatenate([vp_ref[g * A_HEAD_DIM:(g + 1) * A_HEAD_DIM, :],
                                  qv_ref[v0:v0 + A_HEAD_DIM, 0:A_BLOCK]], axis=1)
        else:
            vt = qv_ref[v0:v0 + A_HEAD_DIM, r0 - A_BLOCK:r0 + A_BLOCK]
        acc = _dot(jnp.concatenate([vt, ones_rows], axis=0), e)
        den = acc[A_HEAD_DIM:A_HEAD_DIM + 1, :] + jnp.exp2(sink - m)
        o = (acc[:A_HEAD_DIM, :] / den).astype(BF16)
        for hh in range(A_GROUP):
            hd = g * A_GROUP + hh
            o_ref[hd * A_HEAD_DIM:(hd + 1) * A_HEAD_DIM, r0:r0 + A_BLOCK] = o[:, hh * A_BLOCK:(hh + 1) * A_BLOCK]

    units = [(sb, g) for sb in range(nsub) for g in range(A_KV_HEADS)]
    s_next = scores(*units[0])
    for n, unit in enumerate(units):
        s_cur = s_next
        if n + 1 < len(units):
            s_next = scores(*units[n + 1])
        finish(*unit, s_cur)


def _swa(qvt, ka, bias, sinks, batch, seq, tq):
    nsub = tq // A_BLOCK
    nt = seq // tq
    vrow = A_WIDTH // A_KV_WIDTH
    return pl.pallas_call(
        functools.partial(_swa_kernel, nsub=nsub),
        grid=(batch, nt),
        in_specs=[pl.BlockSpec((QVT_ROWS, tq), lambda b, i: (0, b * nt + i)),
                  pl.BlockSpec((1, tq, A_KV_WIDTH), lambda b, i: (b, i, 0)),
                  pl.BlockSpec((1, A_BLOCK, A_KV_WIDTH), lambda b, i: (b, jnp.maximum(i * nsub - 1, 0), 0)),
                  pl.BlockSpec((A_KV_WIDTH, A_BLOCK),
                               lambda b, i: (vrow, jnp.maximum((b * nt + i) * nsub - 1, 0))),
                  pl.BlockSpec(bias.shape, lambda b, i: (0, 0, 0)),
                  pl.BlockSpec(sinks.shape, lambda b, i: (0, 0, 0))],
        out_specs=pl.BlockSpec((A_WIDTH, tq), lambda b, i: (0, b * nt + i)),
        out_shape=jax.ShapeDtypeStruct((A_WIDTH, batch * seq), BF16),
        compiler_params=pltpu.CompilerParams(
            dimension_semantics=("arbitrary", "arbitrary"), vmem_limit_bytes=VMEM_LIMIT),
        name="swa",
    )(qvt, ka, ka, qvt, bias, sinks)


def _flash_kernel(qt_ref, k_ref, vt_ref, o_ref, m_sc, acc_sc, s0_sc, s1_sc, bm0_sc, bm1_sc, *, tq, nh):
    tk = tq // 2
    i = pl.program_id(2)
    m_sc[...] = jnp.full(m_sc.shape, NEG, F32)
    acc_sc[...] = jnp.zeros(acc_sc.shape, F32)
    s_bufs = (s0_sc, s1_sc)
    bm_bufs = (bm0_sc, bm1_sc)

    def scores(h, j, slot, diag_half):
        kb = k_ref[0, h, pl.ds(pl.multiple_of(j * tk, tk), tk), :]
        s = _dot(kb, qt_ref[0, h])
        if diag_half is not None:
            key = lax.broadcasted_iota(jnp.int32, (tk, tq), 0) + diag_half * tk
            qry = lax.broadcasted_iota(jnp.int32, (tk, tq), 1)
            s = jnp.where(key <= qry, s, NEG)
        s_bufs[slot][h] = s
        bm_bufs[slot][h] = jnp.max(s, axis=0, keepdims=True)

    def softmax_pv(h, j, slot):
        m_old = m_sc[h]
        m_new = jnp.maximum(m_old, bm_bufs[slot][h])
        p = jnp.exp2(s_bufs[slot][h] - m_new).astype(BF16)
        alpha = jnp.exp2(m_old - m_new)
        m_sc[h] = m_new
        acc_sc[h] = acc_sc[h] * alpha + _dot(vt_ref[0, h, j], p)

    def stage(j, slot, next_diag_half, has_next=True):
        for h in range(nh):
            if has_next:
                scores(h, j + 1, 1 - slot, next_diag_half)
            softmax_pv(h, j, slot)

    @pl.when(i == 0)
    def _():
        for h in range(nh):
            scores(h, 0, 0, 0)

    @pl.when(i > 0)
    def _():
        for h in range(nh):
            scores(h, 0, 0, None)

    def pair(jj, carry):
        stage(2 * jj, 0, None)
        stage(2 * jj + 1, 1, None)
        return carry

    lax.fori_loop(0, i - 1, pair, 0)

    @pl.when(i > 0)
    def _():
        stage(2 * i - 2, 0, None)
        stage(2 * i - 1, 1, 0)

    stage(2 * i, 0, 1)
    stage(2 * i + 1, 1, None, has_next=False)

    outs = []
    for h in range(nh):
        acc = acc_sc[h]
        outs.append(acc[:B_VDIM, :] / acc[B_VDIM:B_VDIM + 1, :])
    o_ref[0] = jnp.concatenate(outs, axis=0).T.astype(BF16)


def _flash(qt, k, vt, tq, nh):
    batch, heads, _, seq = qt.shape
    tk = tq // 2
    nq = seq // tq
    return pl.pallas_call(
        functools.partial(_flash_kernel, tq=tq, nh=nh),
        grid=(batch, heads // nh, nq),
        in_specs=[pl.BlockSpec((1, nh, QK_PAD, tq), lambda b, g, i: (b, g, 0, i)),
                  pl.BlockSpec((1, nh, seq, QK_PAD), lambda b, g, i: (b, g, 0, 0)),
                  pl.BlockSpec((1, nh, seq // tk, V_ROWS, tk), lambda b, g, i: (b, g, 0, 0, 0))],
        out_specs=pl.BlockSpec((1, tq, nh * B_VDIM), lambda b, g, i: (b, i, g)),
        out_shape=jax.ShapeDtypeStruct((batch, seq, B_WIDTH), BF16),
        scratch_shapes=[pltpu.VMEM((nh, 1, tq), F32), pltpu.VMEM((nh, V_ROWS, tq), F32),
                        pltpu.VMEM((nh, tk, tq), F32), pltpu.VMEM((nh, tk, tq), F32),
                        pltpu.VMEM((nh, 1, tq), F32), pltpu.VMEM((nh, 1, tq), F32)],
        compiler_params=pltpu.CompilerParams(
            dimension_semantics=("arbitrary", "arbitrary", "arbitrary"),
            vmem_limit_bytes=VMEM_LIMIT),
        name="flash",
    )(qt, k, vt)


def _final_kernel(x_ref, p_ref, oat_ref, za_ref, ob_ref, zb_ref, ga_ref, gb_ref,
                  woa_ref, wob_ref, wout_ref, gpl_ref, wpg_ref, wpp_ref, gpost_ref, out_ref):
    oa = oat_ref[...].astype(F32).T
    ya = _dot((oa * jax.nn.silu(za_ref[...].astype(F32))).astype(BF16), woa_ref[...])
    yb = _dot((ob_ref[...].astype(F32) * jax.nn.silu(zb_ref[...].astype(F32))).astype(BF16), wob_ref[...])
    merged = (jax.nn.sigmoid(ga_ref[...].astype(F32)) * ya
              + jax.nn.sigmoid(gb_ref[...].astype(F32)) * yb)
    x1 = x_ref[...] + _dot(merged.astype(BF16), wout_ref[...])
    gate = jax.nn.sigmoid(_dot((_row_rms(x1, D_MODEL) * gpl_ref[...]).astype(BF16), wpg_ref[...]))
    emb = _row_rms(_dot(p_ref[...].astype(BF16), wpp_ref[...]), D_MODEL) * gpost_ref[...]
    out_ref[...] = x1 + gate * emb


def _final(x2, p2, oat, za, ob, zb, ga, gb, woa, wob, wout, gpl, wpg, wpp, gpost, tm):
    t = x2.shape[0]
    row = lambda a: pl.BlockSpec((tm, a.shape[1]), lambda i: (i, 0))
    full = lambda a: pl.BlockSpec(a.shape, lambda i: (0,) * a.ndim)
    acts = (x2, p2, oat, za, ob, zb, ga, gb)
    consts = (woa, wob, wout, gpl, wpg, wpp, gpost)
    act_specs = [row(a) for a in acts]
    act_specs[2] = pl.BlockSpec((A_WIDTH, tm), lambda i: (0, i))
    return pl.pallas_call(
        _final_kernel,
        grid=(t // tm,),
        in_specs=act_specs + [full(a) for a in consts],
        out_specs=pl.BlockSpec((tm, D_MODEL), lambda i: (i, 0)),
        out_shape=jax.ShapeDtypeStruct((t, D_MODEL), F32),
        compiler_params=pltpu.CompilerParams(
            dimension_semantics=("arbitrary",), vmem_limit_bytes=VMEM_LIMIT),
        name="final",
    )(*acts, *consts)


def kernel(x, p, positions, norm_g, w_in, a_q_norm, a_k_norm, a_sinks, rel_bias, w_o_a, b_cq_norm, w_uq, b_ckv_norm, w_uk, w_uv, b_q_norm, b_k_norm, b_kr_norm, w_o_b, w_out, ple_norm_g, w_ple_gate, w_ple_proj, ple_post_g):
    batch, seq, _ = x.shape
    depth = p.shape[0]
    t = batch * seq
    tm = 512
    blk = 512
    offs = np.concatenate([[0], np.cumsum(SPLIT_SIZES)])
    log2e = math.log2(math.e)
    inv_freq = (ROPE_THETA ** (-jnp.arange(0, B_ROPE, 2, dtype=F32) / B_ROPE))[:, None]
    pos_row = positions.reshape(1, t)
    band_bias = _band_bias(rel_bias)
    row = lambda v: v.astype(F32)[None, :]
    col = lambda v: v.astype(F32)[:, None]

    x2 = x.reshape(t, D_MODEL)
    for i in range(depth):
        w = w_in[i]
        cols = [w[:, offs[j]:offs[j + 1]] for j in range(len(SPLIT_SIZES))]
        wt = jnp.concatenate([cols[0], cols[2], cols[1], cols[6]], axis=1).T.astype(BF16)
        w1 = jnp.concatenate([cols[j] for j in (3, 4, 5, 7, 8, 9)], axis=1).astype(BF16)
        consts = (row(norm_g[i]), wt, w1,
                  col(a_q_norm[i]) * (A_HEAD_DIM ** -0.5 * log2e), col(a_k_norm[i]),
                  row(b_cq_norm[i]), w_uq[i].T.astype(BF16),
                  col(b_q_norm[i]) * (B_QK_DIM ** -0.5 * log2e),
                  row(b_ckv_norm[i]), w_uk[i].T.astype(BF16), w_uv[i].T.astype(BF16),
                  col(b_k_norm[i]), col(b_kr_norm[i]), inv_freq)
        qvt, ka, za, zb, ga, gb, qt, kb, vt = _proj(x2, pos_row, consts, batch, seq, tm, blk // 2)

        sinks = jnp.repeat(a_sinks[i].astype(F32) * log2e, A_BLOCK).reshape(A_KV_HEADS, 1, A_GROUP * A_BLOCK)
        oat = _swa(qvt, ka.reshape(batch, seq, A_KV_WIDTH), band_bias, sinks, batch, seq, tm)
        ob = _flash(qt, kb, vt, blk, 4)

        x2 = _final(x2, p[i].reshape(t, PLE_DIM), oat, za,
                    ob.reshape(t, B_WIDTH), zb, ga, gb,
                    w_o_a[i].astype(BF16), w_o_b[i].astype(BF16), w_out[i].astype(BF16),
                    row(ple_norm_g[i]), w_ple_gate[i].astype(BF16), w_ple_proj[i].astype(BF16),
                    row(ple_post_g[i]), tm)
    return x2.reshape(batch, seq, D_MODEL)
```

```python
import functools
import math

import numpy as np
import jax
import jax.numpy as jnp
from jax import lax
from jax.experimental import pallas as pl
from jax.experimental.pallas import tpu as pltpu

F32 = jnp.float32
BF16 = jnp.bfloat16

D_MODEL = 1024
PLE_DIM = 256
EPS = 1e-6
NEG = -1e30

A_HEADS = 16
A_KV_HEADS = 2
A_HEAD_DIM = 64
A_WIDTH = A_HEADS * A_HEAD_DIM
A_KV_WIDTH = A_KV_HEADS * A_HEAD_DIM
A_GROUP = A_HEADS // A_KV_HEADS
WINDOW = 128
A_BLOCK = 128
N_BUCKETS = 32
MAX_DISTANCE = 128

B_HEADS = 16
B_Q_RANK = 256
B_KV_RANK = 128
B_NOPE = 64
B_ROPE = 32
B_QK_DIM = B_NOPE + B_ROPE
B_VDIM = 64
B_WIDTH = B_HEADS * B_VDIM
ROPE_THETA = 10000.0

SPLIT_SIZES = (A_WIDTH, A_KV_WIDTH, A_KV_WIDTH, A_WIDTH,
               B_Q_RANK, B_KV_RANK, B_ROPE, B_WIDTH,
               D_MODEL, D_MODEL)

LANES = 128
QK_PAD = 128
V_ROWS = 80
VMEM_LIMIT = 56 * 1024 * 1024
MAX_SAFE_SHIFT = 60.0

QVT_ROWS = A_WIDTH + A_KV_WIDTH
R_KA = QVT_ROWS
R_KR = R_KA + A_KV_WIDTH
WT_ROWS = R_KR + B_ROPE
SWA_V_ROWS = A_HEAD_DIM + 16
C_ZA = 0
C_LAT = C_ZA + A_WIDTH
LAT_W = B_Q_RANK + B_KV_RANK
C_ZB = C_LAT + LAT_W
C_GA = C_ZB + B_WIDTH
C_GB = C_GA + D_MODEL
C_END = C_GB + D_MODEL


def _dot(a, b):
    return jnp.dot(a, b, preferred_element_type=F32)


def _dot_nt(a, b):
    return lax.dot_general(a, b, (((1,), (1,)), ((), ())), preferred_element_type=F32)


def _row_rms(v, width):
    return v * lax.rsqrt(jnp.sum(v * v, axis=-1, keepdims=True) * (1.0 / width) + EPS)


def _col_rms(v):
    return v * lax.rsqrt(jnp.sum(v * v, axis=0, keepdims=True) * (1.0 / v.shape[0]) + EPS)


def _proj_kernel(x_ref, pos_ref, g_ref, wt_ref, w_ref, gq_ref, gk_ref,
                 gcq_ref, wuqt_ref, gqn_ref, gckv_ref, wukt_ref, wuvt_ref, gkn_ref, gkr_ref, freq_ref,
                 qvt_ref, ka_ref, za_ref, zb_ref, ga_ref, gb_ref, qt_ref, k_ref, vt_ref):
    x = x_ref[...]
    tm = x.shape[0]
    h = (_row_rms(x, D_MODEL) * g_ref[...]).astype(BF16)

    wt_out = _dot_nt(wt_ref[...], h)
    gq = gq_ref[...]
    for hd in range(A_HEADS):
        blk = wt_out[hd * A_HEAD_DIM:(hd + 1) * A_HEAD_DIM, :]
        qvt_ref[hd * A_HEAD_DIM:(hd + 1) * A_HEAD_DIM, :] = (_col_rms(blk) * gq).astype(BF16)
    qvt_ref[A_WIDTH:, :] = wt_out[A_WIDTH:QVT_ROWS, :].astype(BF16)
    gk = gk_ref[...]
    kat = [_col_rms(wt_out[R_KA + g * A_HEAD_DIM:R_KA + (g + 1) * A_HEAD_DIM, :]) * gk
           for g in range(A_KV_HEADS)]
    ka_ref[...] = jnp.concatenate(kat, axis=0).T.astype(BF16)
    krt = wt_out[R_KR:, :]

    def proj(c0, c1):
        return _dot(h, w_ref[:, c0:c1])

    za_ref[...] = proj(C_ZA, C_LAT).astype(BF16)
    lat = proj(C_LAT, C_ZB)
    zb_ref[...] = proj(C_ZB, C_GA).astype(BF16)
    ga_ref[...] = proj(C_GA, C_GB).astype(BF16)
    gb_ref[...] = proj(C_GB, C_END).astype(BF16)

    cq = (_row_rms(lat[:, :B_Q_RANK], B_Q_RANK) * gcq_ref[...]).astype(BF16)
    qt = _dot_nt(wuqt_ref[...], cq)
    ang = freq_ref[...] * pos_ref[...].astype(F32)
    cos = jnp.cos(ang)
    sin = jnp.sin(ang)
    gqn = gqn_ref[...]
    half = B_ROPE // 2
    zeros_q = jnp.zeros((QK_PAD - B_QK_DIM, tm), BF16)
    for hd in range(B_HEADS):
        qn = _col_rms(qt[hd * B_QK_DIM:(hd + 1) * B_QK_DIM, :]) * gqn
        x1 = qn[B_NOPE:B_NOPE + half, :]
        x2 = qn[B_NOPE + half:, :]
        qt_ref[0, hd, 0:B_NOPE, :] = qn[:B_NOPE, :].astype(BF16)
        qt_ref[0, hd, B_NOPE:B_NOPE + half, :] = (x1 * cos - x2 * sin).astype(BF16)
        qt_ref[0, hd, B_NOPE + half:B_QK_DIM, :] = (x2 * cos + x1 * sin).astype(BF16)
        qt_ref[0, hd, B_QK_DIM:, :] = zeros_q

    ckv = (_row_rms(lat[:, B_Q_RANK:], B_KV_RANK) * gckv_ref[...]).astype(BF16)
    knt = _dot_nt(wukt_ref[...], ckv)
    vt = _dot_nt(wuvt_ref[...], ckv)

    krn = _col_rms(krt) * gkr_ref[...]
    x1 = krn[:half, :]
    x2 = krn[half:, :]
    k_tail = jnp.concatenate([x1 * cos - x2 * sin, x2 * cos + x1 * sin,
                              jnp.zeros((QK_PAD - B_QK_DIM, tm), F32)], axis=0)
    gkn = gkn_ref[...]
    for hd in range(B_HEADS):
        kn = _col_rms(knt[hd * B_NOPE:(hd + 1) * B_NOPE, :]) * gkn
        k_ref[0, hd] = jnp.concatenate([kn, k_tail], axis=0).T.astype(BF16)
    tk = vt_ref.shape[-1]
    ones_rows = (lax.broadcasted_iota(jnp.int32, (V_ROWS - B_VDIM, tk), 0) == 0).astype(BF16)
    for hd in range(B_HEADS):
        for c in range(tm // tk):
            vt_ref[0, hd, c, 0:B_VDIM, :] = vt[hd * B_VDIM:(hd + 1) * B_VDIM, c * tk:(c + 1) * tk].astype(BF16)
            vt_ref[0, hd, c, B_VDIM:, :] = ones_rows


def _proj(x2, pos_row, consts, batch, seq, tm, tk):
    t = x2.shape[0]
    nt = seq // tm
    nc = tm // tk
    row = lambda w: pl.BlockSpec((tm, w), lambda i: (i, 0))
    full = lambda a: pl.BlockSpec(a.shape, lambda i: (0,) * a.ndim)
    widths = (A_KV_WIDTH, A_WIDTH, B_WIDTH, D_MODEL, D_MODEL)
    return pl.pallas_call(
        _proj_kernel,
        grid=(t // tm,),
        in_specs=[row(D_MODEL), pl.BlockSpec((1, tm), lambda i: (0, i))] + [full(a) for a in consts],
        out_specs=[pl.BlockSpec((QVT_ROWS, tm), lambda i: (0, i))] + [row(w) for w in widths]
        + [pl.BlockSpec((1, B_HEADS, QK_PAD, tm), lambda i: (i // nt, 0, 0, i % nt)),
           pl.BlockSpec((1, B_HEADS, tm, QK_PAD), lambda i: (i // nt, 0, i % nt, 0)),
           pl.BlockSpec((1, B_HEADS, nc, V_ROWS, tk), lambda i: (i // nt, 0, i % nt, 0, 0))],
        out_shape=[jax.ShapeDtypeStruct((QVT_ROWS, t), BF16)]
        + [jax.ShapeDtypeStruct((t, w), BF16) for w in widths]
        + [jax.ShapeDtypeStruct((batch, B_HEADS, QK_PAD, seq), BF16),
           jax.ShapeDtypeStruct((batch, B_HEADS, seq, QK_PAD), BF16),
           jax.ShapeDtypeStruct((batch, B_HEADS, nt * nc, V_ROWS, tk), BF16)],
        compiler_params=pltpu.CompilerParams(
            dimension_semantics=("arbitrary",), vmem_limit_bytes=VMEM_LIMIT),
        name="proj",
    )(x2, pos_row, *consts)


def _band_bias_kernel(rbt_ref, onehot_ref, o_ref):
    rbt = rbt_ref[...]
    onehot = onehot_ref[...]
    t = jnp.zeros((A_HEADS, WINDOW), F32)
    for b in range(N_BUCKETS):
        t = t + rbt[:, b:b + 1] * onehot[b:b + 1, :]
    t = t * math.log2(math.e)
    band = 2 * A_BLOCK
    key = lax.broadcasted_iota(jnp.int32, (band, band), 0)
    neg = jnp.full((1, A_BLOCK), NEG, F32)
    for hd in range(A_HEADS):
        x = jnp.broadcast_to(jnp.concatenate([neg, t[hd:hd + 1, :]], axis=1), (band, band))
        for bit in range(8):
            x = jnp.where((key >> bit) & 1 == 1, pltpu.roll(x, 1 << bit, 1), x)
        g, hh = divmod(hd, A_GROUP)
        o_ref[g, :, hh * A_BLOCK:(hh + 1) * A_BLOCK] = x[:, :A_BLOCK]


def _band_bias(rel_bias):
    dist = np.arange(WINDOW)
    max_exact = N_BUCKETS // 2
    large = max_exact + (np.log(np.maximum(dist, 1).astype(np.float32) / max_exact)
                         / math.log(MAX_DISTANCE / max_exact) * (N_BUCKETS - max_exact)).astype(np.int32)
    bucket = np.where(dist < max_exact, dist, np.minimum(large, N_BUCKETS - 1))
    onehot = jnp.asarray(np.arange(N_BUCKETS)[:, None] == bucket[None, :], F32)
    return pl.pallas_call(
        _band_bias_kernel,
        out_shape=jax.ShapeDtypeStruct((A_KV_HEADS, 2 * A_BLOCK, A_GROUP * A_BLOCK), F32),
        name="band_bias",
    )(rel_bias.astype(F32).T, onehot)


def _swa_kernel(qv_ref, k_ref, kp_ref, vp_ref, bias_ref, sink_ref, o_ref, *, nsub):
    first = pl.program_id(1) == 0
    band = 2 * A_BLOCK
    width = A_GROUP * A_BLOCK
    pad = jnp.logical_and(first, lax.broadcasted_iota(jnp.int32, (band, width), 0) < A_BLOCK)
    zeros_q = jnp.zeros((A_HEAD_DIM, width), BF16)
    ones_rows = (lax.broadcasted_iota(jnp.int32, (SWA_V_ROWS - A_HEAD_DIM, band), 0) == 0).astype(BF16)

    def scores(sb, g):
        r0 = sb * A_BLOCK
        if sb == 0:
            kb = jnp.concatenate([kp_ref[0], k_ref[0, 0:A_BLOCK, :]], axis=0)
        else:
            kb = k_ref[0, r0 - A_BLOCK:r0 + A_BLOCK, :]
        tiles = [qv_ref[(g * A_GROUP + hh) * A_HEAD_DIM:(g * A_GROUP + hh + 1) * A_HEAD_DIM, r0:r0 + A_BLOCK]
                 for hh in range(A_GROUP)]
        qt = jnp.concatenate(tiles, axis=1)
        rhs = jnp.concatenate([qt, zeros_q] if g == 0 else [zeros_q, qt], axis=0)
        s = _dot(kb, rhs) + bias_ref[g]
        if sb == 0:
            s = jnp.where(pad, NEG, s)
        return s

    def finish(sb, g, s):
        r0 = sb * A_BLOCK
        sink = sink_ref[g]
        m = jnp.maximum(jnp.max(s, axis=0, keepdims=True), sink)
        e = jnp.exp2(s - m).astype(BF16)
        v0 = A_WIDTH + g * A_HEAD_DIM
        if sb == 0:
            vt = jnp.concatenate([vp_ref[g * A_HEAD_DIM:(g + 1) * A_HEAD_DIM, :],
                                  qv_ref[v0:v0 + A_HEAD_DIM, 0:A_BLOCK]], axis=1)
        else:
            vt = qv_ref[v0:v0 + A_HEAD_DIM, r0 - A_BLOCK:r0 + A_BLOCK]
        acc = _dot(jnp.concatenate([vt, ones_rows], axis=0), e)
        den = acc[A_HEAD_DIM:A_HEAD_DIM + 1, :] + jnp.exp2(sink - m)
        o = (acc[:A_HEAD_DIM, :] / den).astype(BF16)
        for hh in range(A_GROUP):
            hd = g * A_GROUP + hh
            o_ref[hd * A_HEAD_DIM:(hd + 1) * A_HEAD_DIM, r0:r0 + A_BLOCK] = o[:, hh * A_BLOCK:(hh + 1) * A_BLOCK]

    units = [(sb, g) for sb in range(nsub) for g in range(A_KV_HEADS)]
    s_next = scores(*units[0])
    for n, unit in enumerate(units):
        s_cur = s_next
        if n + 1 < len(units):
            s_next = scores(*units[n + 1])
        finish(*unit, s_cur)


def _swa(qvt, ka, bias, sinks, batch, seq, tq):
    nsub = tq // A_BLOCK
    nt = seq // tq
    vrow = A_WIDTH // A_KV_WIDTH
    return pl.pallas_call(
        functools.partial(_swa_kernel, nsub=nsub),
        grid=(batch, nt),
        in_specs=[pl.BlockSpec((QVT_ROWS, tq), lambda b, i: (0, b * nt + i)),
                  pl.BlockSpec((1, tq, A_KV_WIDTH), lambda b, i: (b, i, 0)),
                  pl.BlockSpec((1, A_BLOCK, A_KV_WIDTH), lambda b, i: (b, jnp.maximum(i * nsub - 1, 0), 0)),
                  pl.BlockSpec((A_KV_WIDTH, A_BLOCK),
                               lambda b, i: (vrow, jnp.maximum((b * nt + i) * nsub - 1, 0))),
                  pl.BlockSpec(bias.shape, lambda b, i: (0, 0, 0)),
                  pl.BlockSpec(sinks.shape, lambda b, i: (0, 0, 0))],
        out_specs=pl.BlockSpec((A_WIDTH, tq), lambda b, i: (0, b * nt + i)),
        out_shape=jax.ShapeDtypeStruct((A_WIDTH, batch * seq), BF16),
        compiler_params=pltpu.CompilerParams(
            dimension_semantics=("arbitrary", "arbitrary"), vmem_limit_bytes=VMEM_LIMIT),
        name="swa",
    )(qvt, ka, ka, qvt, bias, sinks)


def _flash_kernel(qt_ref, k_ref, vt_ref, o_ref, m_sc, acc_sc, s0_sc, s1_sc, bm0_sc, bm1_sc, *, tq, nh):
    tk = tq // 2
    i = pl.program_id(2)
    m_sc[...] = jnp.full(m_sc.shape, NEG, F32)
    acc_sc[...] = jnp.zeros(acc_sc.shape, F32)
    s_bufs = (s0_sc, s1_sc)
    bm_bufs = (bm0_sc, bm1_sc)

    def scores(h, j, slot, diag_half):
        kb = k_ref[0, h, pl.ds(pl.multiple_of(j * tk, tk), tk), :]
        s = _dot(kb, qt_ref[0, h])
        if diag_half is not None:
            key = lax.broadcasted_iota(jnp.int32, (tk, tq), 0) + diag_half * tk
            qry = lax.broadcasted_iota(jnp.int32, (tk, tq), 1)
            s = jnp.where(key <= qry, s, NEG)
        s_bufs[slot][h] = s
        bm_bufs[slot][h] = jnp.max(s, axis=0, keepdims=True)

    def softmax_pv(h, j, slot):
        m_old = m_sc[h]
        m_new = jnp.maximum(m_old, bm_bufs[slot][h])
        p = jnp.exp2(s_bufs[slot][h] - m_new).astype(BF16)
        alpha = jnp.exp2(m_old - m_new)
        m_sc[h] = m_new
        acc_sc[h] = acc_sc[h] * alpha + _dot(vt_ref[0, h, j], p)

    def stage(j, slot, next_diag_half, has_next=True):
        for h in range(nh):
            if has_next:
                scores(h, j + 1, 1 - slot, next_diag_half)
            softmax_pv(h, j, slot)

    @pl.when(i == 0)
    def _():
        for h in range(nh):
            scores(h, 0, 0, 0)

    @pl.when(i > 0)
    def _():
        for h in range(nh):
            scores(h, 0, 0, None)

    def pair(jj, carry):
        stage(2 * jj, 0, None)
        stage(2 * jj + 1, 1, None)
        return carry

    lax.fori_loop(0, i - 1, pair, 0)

    @pl.when(i > 0)
    def _():
        stage(2 * i - 2, 0, None)
        stage(2 * i - 1, 1, 0)

    stage(2 * i, 0, 1)
    stage(2 * i + 1, 1, None, has_next=False)

    outs = []
    for h in range(nh):
        acc = acc_sc[h]
        outs.append(acc[:B_VDIM, :] / acc[B_VDIM:B_VDIM + 1, :])
    o_ref[0] = jnp.concatenate(outs, axis=0).T.astype(BF16)


def _flash(qt, k, vt, tq, nh):
    batch, heads, _, seq = qt.shape
    tk = tq // 2
    nq = seq // tq
    return pl.pallas_call(
        functools.partial(_flash_kernel, tq=tq, nh=nh),
        grid=(batch, heads // nh, nq),
        in_specs=[pl.BlockSpec((1, nh, QK_PAD, tq), lambda b, g, i: (b, g, 0, i)),
                  pl.BlockSpec((1, nh, seq, QK_PAD), lambda b, g, i: (b, g, 0, 0)),
                  pl.BlockSpec((1, nh, seq // tk, V_ROWS, tk), lambda b, g, i: (b, g, 0, 0, 0))],
        out_specs=pl.BlockSpec((1, tq, nh * B_VDIM), lambda b, g, i: (b, i, g)),
        out_shape=jax.ShapeDtypeStruct((batch, seq, B_WIDTH), BF16),
        scratch_shapes=[pltpu.VMEM((nh, 1, tq), F32), pltpu.VMEM((nh, V_ROWS, tq), F32),
                        pltpu.VMEM((nh, tk, tq), F32), pltpu.VMEM((nh, tk, tq), F32),
                        pltpu.VMEM((nh, 1, tq), F32), pltpu.VMEM((nh, 1, tq), F32)],
        compiler_params=pltpu.CompilerParams(
            dimension_semantics=("arbitrary", "arbitrary", "arbitrary"),
            vmem_limit_bytes=VMEM_LIMIT),
        name="flash",
    )(qt, k, vt)


def _flash_bounded_kernel(kmax_ref, qt_ref, k_ref, vt_ref, o_ref, mq_sc, acc_sc, p0_sc, p1_sc, *, tq, nh):
    tk = tq // 2
    b = pl.program_id(0)
    g = pl.program_id(1)
    i = pl.program_id(2)
    acc_sc[...] = jnp.zeros(acc_sc.shape, F32)
    for h in range(nh):
        q = qt_ref[0, h].astype(F32)
        mq_sc[h] = jnp.sqrt(jnp.sum(q * q, axis=0, keepdims=True)) * kmax_ref[b, g * nh + h]
    p_bufs = (p0_sc, p1_sc)

    def probs(h, j, slot, diag_half):
        kb = k_ref[0, h, pl.ds(pl.multiple_of(j * tk, tk), tk), :]
        s = _dot(kb, qt_ref[0, h])
        if diag_half is not None:
            key = lax.broadcasted_iota(jnp.int32, (tk, tq), 0) + diag_half * tk
            qry = lax.broadcasted_iota(jnp.int32, (tk, tq), 1)
            s = jnp.where(key <= qry, s, NEG)
        p_bufs[slot][h] = jnp.exp2(s - mq_sc[h]).astype(BF16)

    def pv(h, j, slot):
        acc_sc[h] += _dot(vt_ref[0, h, j], p_bufs[slot][h])

    def stage(j, slot, next_diag_half, has_next=True):
        for h in range(nh):
            if has_next:
                probs(h, j + 1, 1 - slot, next_diag_half)
            pv(h, j, slot)

    @pl.when(i == 0)
    def _():
        for h in range(nh):
            probs(h, 0, 0, 0)

    @pl.when(i > 0)
    def _():
        for h in range(nh):
            probs(h, 0, 0, None)

    def pair(jj, carry):
        stage(2 * jj, 0, None)
        stage(2 * jj + 1, 1, None)
        return carry

    lax.fori_loop(0, i - 1, pair, 0)

    @pl.when(i > 0)
    def _():
        stage(2 * i - 2, 0, None)
        stage(2 * i - 1, 1, 0)

    stage(2 * i, 0, 1)
    stage(2 * i + 1, 1, None, has_next=False)

    outs = []
    for h in range(nh):
        acc = acc_sc[h]
        outs.append(acc[:B_VDIM, :] / acc[B_VDIM:B_VDIM + 1, :])
    o_ref[0] = jnp.concatenate(outs, axis=0).T.astype(BF16)


def _flash_bounded(kmax, qt, k, vt, tq, nh):
    batch, heads, _, seq = qt.shape
    tk = tq // 2
    nq = seq // tq
    return pl.pallas_call(
        functools.partial(_flash_bounded_kernel, tq=tq, nh=nh),
        grid=(batch, heads // nh, nq),
        in_specs=[pl.BlockSpec(memory_space=pltpu.SMEM),
                  pl.BlockSpec((1, nh, QK_PAD, tq), lambda b, g, i: (b, g, 0, i)),
                  pl.BlockSpec((1, nh, seq, QK_PAD), lambda b, g, i: (b, g, 0, 0)),
                  pl.BlockSpec((1, nh, seq // tk, V_ROWS, tk), lambda b, g, i: (b, g, 0, 0, 0))],
        out_specs=pl.BlockSpec((1, tq, nh * B_VDIM), lambda b, g, i: (b, i, g)),
        out_shape=jax.ShapeDtypeStruct((batch, seq, B_WIDTH), BF16),
        scratch_shapes=[pltpu.VMEM((nh, 1, tq), F32), pltpu.VMEM((nh, V_ROWS, tq), F32),
                        pltpu.VMEM((nh, tk, tq), BF16), pltpu.VMEM((nh, tk, tq), BF16)],
        compiler_params=pltpu.CompilerParams(
            dimension_semantics=("arbitrary", "arbitrary", "arbitrary"),
            vmem_limit_bytes=VMEM_LIMIT),
        name="flash_bounded",
    )(kmax, qt, k, vt)


def _final_kernel(x_ref, p_ref, oat_ref, za_ref, ob_ref, zb_ref, ga_ref, gb_ref,
                  woa_ref, wob_ref, wout_ref, gpl_ref, wpg_ref, wpp_ref, gpost_ref, out_ref):
    oa = oat_ref[...].astype(F32).T
    ya = _dot((oa * jax.nn.silu(za_ref[...].astype(F32))).astype(BF16), woa_ref[...])
    yb = _dot((ob_ref[...].astype(F32) * jax.nn.silu(zb_ref[...].astype(F32))).astype(BF16), wob_ref[...])
    merged = (jax.nn.sigmoid(ga_ref[...].astype(F32)) * ya
              + jax.nn.sigmoid(gb_ref[...].astype(F32)) * yb)
    x1 = x_ref[...] + _dot(merged.astype(BF16), wout_ref[...])
    gate = jax.nn.sigmoid(_dot((_row_rms(x1, D_MODEL) * gpl_ref[...]).astype(BF16), wpg_ref[...]))
    emb = _row_rms(_dot(p_ref[...].astype(BF16), wpp_ref[...]), D_MODEL) * gpost_ref[...]
    out_ref[...] = x1 + gate * emb


def _final(x2, p2, oat, za, ob, zb, ga, gb, woa, wob, wout, gpl, wpg, wpp, gpost, tm):
    t = x2.shape[0]
    row = lambda a: pl.BlockSpec((tm, a.shape[1]), lambda i: (i, 0))
    full = lambda a: pl.BlockSpec(a.shape, lambda i: (0,) * a.ndim)
    acts = (x2, p2, oat, za, ob, zb, ga, gb)
    consts = (woa, wob, wout, gpl, wpg, wpp, gpost)
    act_specs = [row(a) for a in acts]
    act_specs[2] = pl.BlockSpec((A_WIDTH, tm), lambda i: (0, i))
    return pl.pallas_call(
        _final_kernel,
        grid=(t // tm,),
        in_specs=act_specs + [full(a) for a in consts],
        out_specs=pl.BlockSpec((tm, D_MODEL), lambda i: (i, 0)),
        out_shape=jax.ShapeDtypeStruct((t, D_MODEL), F32),
        compiler_params=pltpu.CompilerParams(
            dimension_semantics=("arbitrary",), vmem_limit_bytes=VMEM_LIMIT),
        name="final",
    )(*acts, *consts)


def kernel(x, p, positions, norm_g, w_in, a_q_norm, a_k_norm, a_sinks, rel_bias, w_o_a, b_cq_norm, w_uq, b_ckv_norm, w_uk, w_uv, b_q_norm, b_k_norm, b_kr_norm, w_o_b, w_out, ple_norm_g, w_ple_gate, w_ple_proj, ple_post_g):
    batch, seq, _ = x.shape
    depth = p.shape[0]
    t = batch * seq
    tm = 512
    blk = 512
    offs = np.concatenate([[0], np.cumsum(SPLIT_SIZES)])
    log2e = math.log2(math.e)
    inv_freq = (ROPE_THETA ** (-jnp.arange(0, B_ROPE, 2, dtype=F32) / B_ROPE))[:, None]
    pos_row = positions.reshape(1, t)
    band_bias = _band_bias(rel_bias)
    row = lambda v: v.astype(F32)[None, :]
    col = lambda v: v.astype(F32)[:, None]

    x2 = x.reshape(t, D_MODEL)
    for i in range(depth):
        w = w_in[i]
        cols = [w[:, offs[j]:offs[j + 1]] for j in range(len(SPLIT_SIZES))]
        wt = jnp.concatenate([cols[0], cols[2], cols[1], cols[6]], axis=1).T.astype(BF16)
        w1 = jnp.concatenate([cols[j] for j in (3, 4, 5, 7, 8, 9)], axis=1).astype(BF16)
        consts = (row(norm_g[i]), wt, w1,
                  col(a_q_norm[i]) * (A_HEAD_DIM ** -0.5 * log2e), col(a_k_norm[i]),
                  row(b_cq_norm[i]), w_uq[i].T.astype(BF16),
                  col(b_q_norm[i]) * (B_QK_DIM ** -0.5 * log2e),
                  row(b_ckv_norm[i]), w_uk[i].T.astype(BF16), w_uv[i].T.astype(BF16),
                  col(b_k_norm[i]), col(b_kr_norm[i]), inv_freq)
        qvt, ka, za, zb, ga, gb, qt, kb, vt = _proj(x2, pos_row, consts, batch, seq, tm, blk // 2)

        sinks = jnp.repeat(a_sinks[i].astype(F32) * log2e, A_BLOCK).reshape(A_KV_HEADS, 1, A_GROUP * A_BLOCK)
        oat = _swa(qvt, ka.reshape(batch, seq, A_KV_WIDTH), band_bias, sinks, batch, seq, tm)
        kf = kb.astype(F32)
        qf = qt.astype(F32)
        kmax = jnp.sqrt(jnp.max(jnp.sum(kf * kf, axis=-1), axis=-1))
        qmax = jnp.sqrt(jnp.max(jnp.sum(qf * qf, axis=-2), axis=-1))
        ob = lax.cond(jnp.max(kmax * qmax) < MAX_SAFE_SHIFT,
                      lambda: _flash_bounded(kmax, qt, kb, vt, blk, 4),
                      lambda: _flash(qt, kb, vt, blk, 4))

        x2 = _final(x2, p[i].reshape(t, PLE_DIM), oat, za,
                    ob.reshape(t, B_WIDTH), zb, ga, gb,
                    w_o_a[i].astype(BF16), w_o_b[i].astype(BF16), w_out[i].astype(BF16),
                    row(ple_norm_g[i]), w_ple_gate[i].astype(BF16), w_ple_proj[i].astype(BF16),
                    row(ple_post_g[i]), tm)
    return x2.reshape(batch, seq, D_MODEL)
```

```python
import functools
import math

import numpy as np
import jax
import jax.numpy as jnp
from jax import lax
from jax.experimental import pallas as pl
from jax.experimental.pallas import tpu as pltpu

F32 = jnp.float32
BF16 = jnp.bfloat16

D_MODEL = 1024
PLE_DIM = 256
EPS = 1e-6
NEG = -1e30

A_HEADS = 16
A_KV_HEADS = 2
A_HEAD_DIM = 64
A_WIDTH = A_HEADS * A_HEAD_DIM
A_KV_WIDTH = A_KV_HEADS * A_HEAD_DIM
A_GROUP = A_HEADS // A_KV_HEADS
WINDOW = 128
A_BLOCK = 128
N_BUCKETS = 32
MAX_DISTANCE = 128

B_HEADS = 16
B_Q_RANK = 256
B_KV_RANK = 128
B_NOPE = 64
B_ROPE = 32
B_QK_DIM = B_NOPE + B_ROPE
B_VDIM = 64
B_WIDTH = B_HEADS * B_VDIM
ROPE_THETA = 10000.0

SPLIT_SIZES = (A_WIDTH, A_KV_WIDTH, A_KV_WIDTH, A_WIDTH,
               B_Q_RANK, B_KV_RANK, B_ROPE, B_WIDTH,
               D_MODEL, D_MODEL)

LANES = 128
QK_PAD = 128
V_ROWS = 80
VMEM_LIMIT = 56 * 1024 * 1024
MAX_SAFE_SHIFT = 60.0

QVT_ROWS = A_WIDTH + A_KV_WIDTH
R_KA = QVT_ROWS
R_KR = R_KA + A_KV_WIDTH
WT_ROWS = R_KR + B_ROPE
SWA_V_ROWS = A_HEAD_DIM + 16
C_ZA = 0
C_LAT = C_ZA + A_WIDTH
LAT_W = B_Q_RANK + B_KV_RANK
C_ZB = C_LAT + LAT_W
C_GA = C_ZB + B_WIDTH
C_GB = C_GA + D_MODEL
C_END = C_GB + D_MODEL


def _dot(a, b):
    return jnp.dot(a, b, preferred_element_type=F32)


def _dot_nt(a, b):
    return lax.dot_general(a, b, (((1,), (1,)), ((), ())), preferred_element_type=F32)


def _row_rms(v, width):
    return v * lax.rsqrt(jnp.sum(v * v, axis=-1, keepdims=True) * (1.0 / width) + EPS)


def _col_rms(v):
    return v * lax.rsqrt(jnp.sum(v * v, axis=0, keepdims=True) * (1.0 / v.shape[0]) + EPS)


def _proj_kernel(x_ref, pos_ref, g_ref, wt_ref, w_ref, gq_ref, gk_ref,
                 gcq_ref, wuqt_ref, gqn_ref, gckv_ref, wukt_ref, wuvt_ref, gkn_ref, gkr_ref, freq_ref,
                 qvt_ref, ka_ref, za_ref, zb_ref, ga_ref, gb_ref, qt_ref, k_ref, vt_ref, stat_ref):
    x = x_ref[...]
    tm = x.shape[0]
    h = (_row_rms(x, D_MODEL) * g_ref[...]).astype(BF16)

    wt_out = _dot_nt(wt_ref[...], h)
    gq = gq_ref[...]
    for hd in range(A_HEADS):
        blk = wt_out[hd * A_HEAD_DIM:(hd + 1) * A_HEAD_DIM, :]
        qvt_ref[hd * A_HEAD_DIM:(hd + 1) * A_HEAD_DIM, :] = (_col_rms(blk) * gq).astype(BF16)
    qvt_ref[A_WIDTH:, :] = wt_out[A_WIDTH:QVT_ROWS, :].astype(BF16)
    gk = gk_ref[...]
    kat = [_col_rms(wt_out[R_KA + g * A_HEAD_DIM:R_KA + (g + 1) * A_HEAD_DIM, :]) * gk
           for g in range(A_KV_HEADS)]
    ka_ref[...] = jnp.concatenate(kat, axis=0).T.astype(BF16)
    krt = wt_out[R_KR:, :]

    def proj(c0, c1):
        return _dot(h, w_ref[:, c0:c1])

    za_ref[...] = proj(C_ZA, C_LAT).astype(BF16)
    lat = proj(C_LAT, C_ZB)
    zb_ref[...] = proj(C_ZB, C_GA).astype(BF16)
    ga_ref[...] = proj(C_GA, C_GB).astype(BF16)
    gb_ref[...] = proj(C_GB, C_END).astype(BF16)

    cq = (_row_rms(lat[:, :B_Q_RANK], B_Q_RANK) * gcq_ref[...]).astype(BF16)
    qt = _dot_nt(wuqt_ref[...], cq)
    ang = freq_ref[...] * pos_ref[...].astype(F32)
    cos = jnp.cos(ang)
    sin = jnp.sin(ang)
    gqn = gqn_ref[...]
    half = B_ROPE // 2
    zeros_q = jnp.zeros((QK_PAD - B_QK_DIM, tm), BF16)
    lane = lax.broadcasted_iota(jnp.int32, (1, LANES), 1)
    q_stat = jnp.zeros((1, LANES), F32)
    k_stat = jnp.zeros((1, LANES), F32)
    for hd in range(B_HEADS):
        qn = _col_rms(qt[hd * B_QK_DIM:(hd + 1) * B_QK_DIM, :]) * gqn
        q_stat = jnp.where(lane == hd, jnp.max(jnp.sum(qn * qn, axis=0, keepdims=True), axis=1, keepdims=True),
                           q_stat)
        x1 = qn[B_NOPE:B_NOPE + half, :]
        x2 = qn[B_NOPE + half:, :]
        qt_ref[0, hd, 0:B_NOPE, :] = qn[:B_NOPE, :].astype(BF16)
        qt_ref[0, hd, B_NOPE:B_NOPE + half, :] = (x1 * cos - x2 * sin).astype(BF16)
        qt_ref[0, hd, B_NOPE + half:B_QK_DIM, :] = (x2 * cos + x1 * sin).astype(BF16)
        qt_ref[0, hd, B_QK_DIM:, :] = zeros_q

    ckv = (_row_rms(lat[:, B_Q_RANK:], B_KV_RANK) * gckv_ref[...]).astype(BF16)
    knt = _dot_nt(wukt_ref[...], ckv)
    vt = _dot_nt(wuvt_ref[...], ckv)

    krn = _col_rms(krt) * gkr_ref[...]
    x1 = krn[:half, :]
    x2 = krn[half:, :]
    k_tail = jnp.concatenate([x1 * cos - x2 * sin, x2 * cos + x1 * sin,
                              jnp.zeros((QK_PAD - B_QK_DIM, tm), F32)], axis=0)
    gkn = gkn_ref[...]
    tail_ss = jnp.sum(krn * krn, axis=0, keepdims=True)
    for hd in range(B_HEADS):
        kn = _col_rms(knt[hd * B_NOPE:(hd + 1) * B_NOPE, :]) * gkn
        k_ref[0, hd] = jnp.concatenate([kn, k_tail], axis=0).T.astype(BF16)
        k_ss = jnp.sum(kn * kn, axis=0, keepdims=True) + tail_ss
        k_stat = jnp.where(lane == hd, jnp.max(k_ss, axis=1, keepdims=True), k_stat)
    stat_ref[0] = jnp.concatenate([q_stat, k_stat, jnp.zeros((6, LANES), F32)], axis=0)
    tk = vt_ref.shape[-1]
    ones_rows = (lax.broadcasted_iota(jnp.int32, (V_ROWS - B_VDIM, tk), 0) == 0).astype(BF16)
    for hd in range(B_HEADS):
        for c in range(tm // tk):
            vt_ref[0, hd, c, 0:B_VDIM, :] = vt[hd * B_VDIM:(hd + 1) * B_VDIM, c * tk:(c + 1) * tk].astype(BF16)
            vt_ref[0, hd, c, B_VDIM:, :] = ones_rows


def _proj(x2, pos_row, consts, batch, seq, tm, tk):
    t = x2.shape[0]
    nt = seq // tm
    nc = tm // tk
    row = lambda w: pl.BlockSpec((tm, w), lambda i: (i, 0))
    full = lambda a: pl.BlockSpec(a.shape, lambda i: (0,) * a.ndim)
    widths = (A_KV_WIDTH, A_WIDTH, B_WIDTH, D_MODEL, D_MODEL)
    return pl.pallas_call(
        _proj_kernel,
        grid=(t // tm,),
        in_specs=[row(D_MODEL), pl.BlockSpec((1, tm), lambda i: (0, i))] + [full(a) for a in consts],
        out_specs=[pl.BlockSpec((QVT_ROWS, tm), lambda i: (0, i))] + [row(w) for w in widths]
        + [pl.BlockSpec((1, B_HEADS, QK_PAD, tm), lambda i: (i // nt, 0, 0, i % nt)),
           pl.BlockSpec((1, B_HEADS, tm, QK_PAD), lambda i: (i // nt, 0, i % nt, 0)),
           pl.BlockSpec((1, B_HEADS, nc, V_ROWS, tk), lambda i: (i // nt, 0, i % nt, 0, 0)),
           pl.BlockSpec((1, 8, LANES), lambda i: (i, 0, 0))],
        out_shape=[jax.ShapeDtypeStruct((QVT_ROWS, t), BF16)]
        + [jax.ShapeDtypeStruct((t, w), BF16) for w in widths]
        + [jax.ShapeDtypeStruct((batch, B_HEADS, QK_PAD, seq), BF16),
           jax.ShapeDtypeStruct((batch, B_HEADS, seq, QK_PAD), BF16),
           jax.ShapeDtypeStruct((batch, B_HEADS, nt * nc, V_ROWS, tk), BF16),
           jax.ShapeDtypeStruct((t // tm, 8, LANES), F32)],
        compiler_params=pltpu.CompilerParams(
            dimension_semantics=("arbitrary",), vmem_limit_bytes=VMEM_LIMIT),
        name="proj",
    )(x2, pos_row, *consts)


def _band_bias_kernel(rbt_ref, onehot_ref, o_ref):
    rbt = rbt_ref[...]
    onehot = onehot_ref[...]
    t = jnp.zeros((A_HEADS, WINDOW), F32)
    for b in range(N_BUCKETS):
        t = t + rbt[:, b:b + 1] * onehot[b:b + 1, :]
    t = t * math.log2(math.e)
    band = 2 * A_BLOCK
    key = lax.broadcasted_iota(jnp.int32, (band, band), 0)
    neg = jnp.full((1, A_BLOCK), NEG, F32)
    for hd in range(A_HEADS):
        x = jnp.broadcast_to(jnp.concatenate([neg, t[hd:hd + 1, :]], axis=1), (band, band))
        for bit in range(8):
            x = jnp.where((key >> bit) & 1 == 1, pltpu.roll(x, 1 << bit, 1), x)
        g, hh = divmod(hd, A_GROUP)
        o_ref[g, :, hh * A_BLOCK:(hh + 1) * A_BLOCK] = x[:, :A_BLOCK]


def _band_bias(rel_bias):
    dist = np.arange(WINDOW)
    max_exact = N_BUCKETS // 2
    large = max_exact + (np.log(np.maximum(dist, 1).astype(np.float32) / max_exact)
                         / math.log(MAX_DISTANCE / max_exact) * (N_BUCKETS - max_exact)).astype(np.int32)
    bucket = np.where(dist < max_exact, dist, np.minimum(large, N_BUCKETS - 1))
    onehot = jnp.asarray(np.arange(N_BUCKETS)[:, None] == bucket[None, :], F32)
    return pl.pallas_call(
        _band_bias_kernel,
        out_shape=jax.ShapeDtypeStruct((A_KV_HEADS, 2 * A_BLOCK, A_GROUP * A_BLOCK), F32),
        name="band_bias",
    )(rel_bias.astype(F32).T, onehot)


def _swa_kernel(qv_ref, k_ref, kp_ref, vp_ref, bias_ref, sink_ref, o_ref, *, nsub):
    first = pl.program_id(1) == 0
    band = 2 * A_BLOCK
    width = A_GROUP * A_BLOCK
    pad = jnp.logical_and(first, lax.broadcasted_iota(jnp.int32, (band, width), 0) < A_BLOCK)
    zeros_q = jnp.zeros((A_HEAD_DIM, width), BF16)
    ones_rows = (lax.broadcasted_iota(jnp.int32, (SWA_V_ROWS - A_HEAD_DIM, band), 0) == 0).astype(BF16)

    def scores(sb, g):
        r0 = sb * A_BLOCK
        if sb == 0:
            kb = jnp.concatenate([kp_ref[0], k_ref[0, 0:A_BLOCK, :]], axis=0)
        else:
            kb = k_ref[0, r0 - A_BLOCK:r0 + A_BLOCK, :]
        tiles = [qv_ref[(g * A_GROUP + hh) * A_HEAD_DIM:(g * A_GROUP + hh + 1) * A_HEAD_DIM, r0:r0 + A_BLOCK]
                 for hh in range(A_GROUP)]
        qt = jnp.concatenate(tiles, axis=1)
        rhs = jnp.concatenate([qt, zeros_q] if g == 0 else [zeros_q, qt], axis=0)
        s = _dot(kb, rhs) + bias_ref[g]
        if sb == 0:
            s = jnp.where(pad, NEG, s)
        return s

    def finish(sb, g, s):
        r0 = sb * A_BLOCK
        sink = sink_ref[g]
        m = jnp.maximum(jnp.max(s, axis=0, keepdims=True), sink)
        e = jnp.exp2(s - m).astype(BF16)
        v0 = A_WIDTH + g * A_HEAD_DIM
        if sb == 0:
            vt = jnp.concatenate([vp_ref[g * A_HEAD_DIM:(g + 1) * A_HEAD_DIM, :],
                                  qv_ref[v0:v0 + A_HEAD_DIM, 0:A_BLOCK]], axis=1)
        else:
            vt = qv_ref[v0:v0 + A_HEAD_DIM, r0 - A_BLOCK:r0 + A_BLOCK]
        acc = _dot(jnp.concatenate([vt, ones_rows], axis=0), e)
        den = acc[A_HEAD_DIM:A_HEAD_DIM + 1, :] + jnp.exp2(sink - m)
        o = (acc[:A_HEAD_DIM, :] / den).astype(BF16)
        for hh in range(A_GROUP):
            hd = g * A_GROUP + hh
            o_ref[hd * A_HEAD_DIM:(hd + 1) * A_HEAD_DIM, r0:r0 + A_BLOCK] = o[:, hh * A_BLOCK:(hh + 1) * A_BLOCK]

    units = [(sb, g) for sb in range(nsub) for g in range(A_KV_HEADS)]
    s_next = scores(*units[0])
    for n, unit in enumerate(units):
        s_cur = s_next
        if n + 1 < len(units):
            s_next = scores(*units[n + 1])
        finish(*unit, s_cur)


def _swa(qvt, ka, bias, sinks, batch, seq, tq):
    nsub = tq // A_BLOCK
    nt = seq // tq
    vrow = A_WIDTH // A_KV_WIDTH
    return pl.pallas_call(
        functools.partial(_swa_kernel, nsub=nsub),
        grid=(batch, nt),
        in_specs=[pl.BlockSpec((QVT_ROWS, tq), lambda b, i: (0, b * nt + i)),
                  pl.BlockSpec((1, tq, A_KV_WIDTH), lambda b, i: (b, i, 0)),
                  pl.BlockSpec((1, A_BLOCK, A_KV_WIDTH), lambda b, i: (b, jnp.maximum(i * nsub - 1, 0), 0)),
                  pl.BlockSpec((A_KV_WIDTH, A_BLOCK),
                               lambda b, i: (vrow, jnp.maximum((b * nt + i) * nsub - 1, 0))),
                  pl.BlockSpec(bias.shape, lambda b, i: (0, 0, 0)),
                  pl.BlockSpec(sinks.shape, lambda b, i: (0, 0, 0))],
        out_specs=pl.BlockSpec((A_WIDTH, tq), lambda b, i: (0, b * nt + i)),
        out_shape=jax.ShapeDtypeStruct((A_WIDTH, batch * seq), BF16),
        compiler_params=pltpu.CompilerParams(
            dimension_semantics=("arbitrary", "arbitrary"), vmem_limit_bytes=VMEM_LIMIT),
        name="swa",
    )(qvt, ka, ka, qvt, bias, sinks)


def _flash_kernel(qt_ref, k_ref, vt_ref, o_ref, m_sc, acc_sc, s0_sc, s1_sc, bm0_sc, bm1_sc, *, tq, nh):
    tk = tq // 2
    i = pl.program_id(2)
    m_sc[...] = jnp.full(m_sc.shape, NEG, F32)
    acc_sc[...] = jnp.zeros(acc_sc.shape, F32)
    s_bufs = (s0_sc, s1_sc)
    bm_bufs = (bm0_sc, bm1_sc)

    def scores(h, j, slot, diag_half):
        kb = k_ref[0, h, pl.ds(pl.multiple_of(j * tk, tk), tk), :]
        s = _dot(kb, qt_ref[0, h])
        if diag_half is not None:
            key = lax.broadcasted_iota(jnp.int32, (tk, tq), 0) + diag_half * tk
            qry = lax.broadcasted_iota(jnp.int32, (tk, tq), 1)
            s = jnp.where(key <= qry, s, NEG)
        s_bufs[slot][h] = s
        bm_bufs[slot][h] = jnp.max(s, axis=0, keepdims=True)

    def softmax_pv(h, j, slot):
        m_old = m_sc[h]
        m_new = jnp.maximum(m_old, bm_bufs[slot][h])
        p = jnp.exp2(s_bufs[slot][h] - m_new).astype(BF16)
        alpha = jnp.exp2(m_old - m_new)
        m_sc[h] = m_new
        acc_sc[h] = acc_sc[h] * alpha + _dot(vt_ref[0, h, j], p)

    def stage(j, slot, next_diag_half, has_next=True):
        for h in range(nh):
            if has_next:
                scores(h, j + 1, 1 - slot, next_diag_half)
            softmax_pv(h, j, slot)

    @pl.when(i == 0)
    def _():
        for h in range(nh):
            scores(h, 0, 0, 0)

    @pl.when(i > 0)
    def _():
        for h in range(nh):
            scores(h, 0, 0, None)

    def pair(jj, carry):
        stage(2 * jj, 0, None)
        stage(2 * jj + 1, 1, None)
        return carry

    lax.fori_loop(0, i - 1, pair, 0)

    @pl.when(i > 0)
    def _():
        stage(2 * i - 2, 0, None)
        stage(2 * i - 1, 1, 0)

    stage(2 * i, 0, 1)
    stage(2 * i + 1, 1, None, has_next=False)

    outs = []
    for h in range(nh):
        acc = acc_sc[h]
        outs.append(acc[:B_VDIM, :] / acc[B_VDIM:B_VDIM + 1, :])
    o_ref[0] = jnp.concatenate(outs, axis=0).T.astype(BF16)


def _flash(qt, k, vt, tq, nh):
    batch, heads, _, seq = qt.shape
    tk = tq // 2
    nq = seq // tq
    return pl.pallas_call(
        functools.partial(_flash_kernel, tq=tq, nh=nh),
        grid=(batch, heads // nh, nq),
        in_specs=[pl.BlockSpec((1, nh, QK_PAD, tq), lambda b, g, i: (b, g, 0, i)),
                  pl.BlockSpec((1, nh, seq, QK_PAD), lambda b, g, i: (b, g, 0, 0)),
                  pl.BlockSpec((1, nh, seq // tk, V_ROWS, tk), lambda b, g, i: (b, g, 0, 0, 0))],
        out_specs=pl.BlockSpec((1, tq, nh * B_VDIM), lambda b, g, i: (b, i, g)),
        out_shape=jax.ShapeDtypeStruct((batch, seq, B_WIDTH), BF16),
        scratch_shapes=[pltpu.VMEM((nh, 1, tq), F32), pltpu.VMEM((nh, V_ROWS, tq), F32),
                        pltpu.VMEM((nh, tk, tq), F32), pltpu.VMEM((nh, tk, tq), F32),
                        pltpu.VMEM((nh, 1, tq), F32), pltpu.VMEM((nh, 1, tq), F32)],
        compiler_params=pltpu.CompilerParams(
            dimension_semantics=("arbitrary", "arbitrary", "arbitrary"),
            vmem_limit_bytes=VMEM_LIMIT),
        name="flash",
    )(qt, k, vt)


def _flash_bounded_kernel(kmax_ref, qt_ref, k_ref, vt_ref, o_ref, mq_sc, acc_sc, p0_sc, p1_sc, *, tq, nh):
    tk = tq // 2
    b = pl.program_id(0)
    g = pl.program_id(1)
    i = pl.program_id(2)
    acc_sc[...] = jnp.zeros(acc_sc.shape, F32)
    for h in range(nh):
        q = qt_ref[0, h].astype(F32)
        mq_sc[h] = jnp.sqrt(jnp.sum(q * q, axis=0, keepdims=True)) * kmax_ref[b, g * nh + h]
    p_bufs = (p0_sc, p1_sc)

    def probs(h, j, slot, diag_half):
        kb = k_ref[0, h, pl.ds(pl.multiple_of(j * tk, tk), tk), :]
        s = _dot(kb, qt_ref[0, h])
        if diag_half is not None:
            key = lax.broadcasted_iota(jnp.int32, (tk, tq), 0) + diag_half * tk
            qry = lax.broadcasted_iota(jnp.int32, (tk, tq), 1)
            s = jnp.where(key <= qry, s, NEG)
        p_bufs[slot][h] = jnp.exp2(s - mq_sc[h]).astype(BF16)

    def pv(h, j, slot):
        acc_sc[h] += _dot(vt_ref[0, h, j], p_bufs[slot][h])

    def stage(j, slot, next_diag_half, has_next=True):
        for h in range(nh):
            if has_next:
                probs(h, j + 1, 1 - slot, next_diag_half)
            pv(h, j, slot)

    @pl.when(i == 0)
    def _():
        for h in range(nh):
            probs(h, 0, 0, 0)

    @pl.when(i > 0)
    def _():
        for h in range(nh):
            probs(h, 0, 0, None)

    def pair(jj, carry):
        stage(2 * jj, 0, None)
        stage(2 * jj + 1, 1, None)
        return carry

    lax.fori_loop(0, i - 1, pair, 0)

    @pl.when(i > 0)
    def _():
        stage(2 * i - 2, 0, None)
        stage(2 * i - 1, 1, 0)

    stage(2 * i, 0, 1)
    stage(2 * i + 1, 1, None, has_next=False)

    outs = []
    for h in range(nh):
        acc = acc_sc[h]
        outs.append(acc[:B_VDIM, :] / acc[B_VDIM:B_VDIM + 1, :])
    o_ref[0] = jnp.concatenate(outs, axis=0).T.astype(BF16)


def _flash_bounded(kmax, qt, k, vt, tq, nh):
    batch, heads, _, seq = qt.shape
    tk = tq // 2
    nq = seq // tq
    return pl.pallas_call(
        functools.partial(_flash_bounded_kernel, tq=tq, nh=nh),
        grid=(batch, heads // nh, nq),
        in_specs=[pl.BlockSpec(memory_space=pltpu.SMEM),
                  pl.BlockSpec((1, nh, QK_PAD, tq), lambda b, g, i: (b, g, 0, i)),
                  pl.BlockSpec((1, nh, seq, QK_PAD), lambda b, g, i: (b, g, 0, 0)),
                  pl.BlockSpec((1, nh, seq // tk, V_ROWS, tk), lambda b, g, i: (b, g, 0, 0, 0))],
        out_specs=pl.BlockSpec((1, tq, nh * B_VDIM), lambda b, g, i: (b, i, g)),
        out_shape=jax.ShapeDtypeStruct((batch, seq, B_WIDTH), BF16),
        scratch_shapes=[pltpu.VMEM((nh, 1, tq), F32), pltpu.VMEM((nh, V_ROWS, tq), F32),
                        pltpu.VMEM((nh, tk, tq), BF16), pltpu.VMEM((nh, tk, tq), BF16)],
        compiler_params=pltpu.CompilerParams(
            dimension_semantics=("arbitrary", "arbitrary", "arbitrary"),
            vmem_limit_bytes=VMEM_LIMIT),
        name="flash_bounded",
    )(kmax, qt, k, vt)


def _final_kernel(x_ref, p_ref, oat_ref, za_ref, ob_ref, zb_ref, ga_ref, gb_ref,
                  woa_ref, wob_ref, wout_ref, gpl_ref, wpg_ref, wpp_ref, gpost_ref, out_ref):
    oa = oat_ref[...].astype(F32).T
    ya = _dot((oa * jax.nn.silu(za_ref[...].astype(F32))).astype(BF16), woa_ref[...])
    yb = _dot((ob_ref[...].astype(F32) * jax.nn.silu(zb_ref[...].astype(F32))).astype(BF16), wob_ref[...])
    merged = (jax.nn.sigmoid(ga_ref[...].astype(F32)) * ya
              + jax.nn.sigmoid(gb_ref[...].astype(F32)) * yb)
    x1 = x_ref[...] + _dot(merged.astype(BF16), wout_ref[...])
    gate = jax.nn.sigmoid(_dot((_row_rms(x1, D_MODEL) * gpl_ref[...]).astype(BF16), wpg_ref[...]))
    emb = _row_rms(_dot(p_ref[...].astype(BF16), wpp_ref[...]), D_MODEL) * gpost_ref[...]
    out_ref[...] = x1 + gate * emb


def _final(x2, p2, oat, za, ob, zb, ga, gb, woa, wob, wout, gpl, wpg, wpp, gpost, tm):
    t = x2.shape[0]
    row = lambda a: pl.BlockSpec((tm, a.shape[1]), lambda i: (i, 0))
    full = lambda a: pl.BlockSpec(a.shape, lambda i: (0,) * a.ndim)
    acts = (x2, p2, oat, za, ob, zb, ga, gb)
    consts = (woa, wob, wout, gpl, wpg, wpp, gpost)
    act_specs = [row(a) for a in acts]
    act_specs[2] = pl.BlockSpec((A_WIDTH, tm), lambda i: (0, i))
    return pl.pallas_call(
        _final_kernel,
        grid=(t // tm,),
        in_specs=act_specs + [full(a) for a in consts],
        out_specs=pl.BlockSpec((tm, D_MODEL), lambda i: (i, 0)),
        out_shape=jax.ShapeDtypeStruct((t, D_MODEL), F32),
        compiler_params=pltpu.CompilerParams(
            dimension_semantics=("arbitrary",), vmem_limit_bytes=VMEM_LIMIT),
        name="final",
    )(*acts, *consts)


def kernel(x, p, positions, norm_g, w_in, a_q_norm, a_k_norm, a_sinks, rel_bias, w_o_a, b_cq_norm, w_uq, b_ckv_norm, w_uk, w_uv, b_q_norm, b_k_norm, b_kr_norm, w_o_b, w_out, ple_norm_g, w_ple_gate, w_ple_proj, ple_post_g):
    batch, seq, _ = x.shape
    depth = p.shape[0]
    t = batch * seq
    tm = 512
    blk = 512
    offs = np.concatenate([[0], np.cumsum(SPLIT_SIZES)])
    log2e = math.log2(math.e)
    inv_freq = (ROPE_THETA ** (-jnp.arange(0, B_ROPE, 2, dtype=F32) / B_ROPE))[:, None]
    pos_row = positions.reshape(1, t)
    band_bias = _band_bias(rel_bias)
    row = lambda v: v.astype(F32)[None, :]
    col = lambda v: v.astype(F32)[:, None]

    x2 = x.reshape(t, D_MODEL)
    for i in range(depth):
        w = w_in[i]
        cols = [w[:, offs[j]:offs[j + 1]] for j in range(len(SPLIT_SIZES))]
        wt = jnp.concatenate([cols[0], cols[2], cols[1], cols[6]], axis=1).T.astype(BF16)
        w1 = jnp.concatenate([cols[j] for j in (3, 4, 5, 7, 8, 9)], axis=1).astype(BF16)
        consts = (row(norm_g[i]), wt, w1,
                  col(a_q_norm[i]) * (A_HEAD_DIM ** -0.5 * log2e), col(a_k_norm[i]),
                  row(b_cq_norm[i]), w_uq[i].T.astype(BF16),
                  col(b_q_norm[i]) * (B_QK_DIM ** -0.5 * log2e),
                  row(b_ckv_norm[i]), w_uk[i].T.astype(BF16), w_uv[i].T.astype(BF16),
                  col(b_k_norm[i]), col(b_kr_norm[i]), inv_freq)
        qvt, ka, za, zb, ga, gb, qt, kb, vt, stats = _proj(x2, pos_row, consts, batch, seq, tm, blk // 2)

        sinks = jnp.repeat(a_sinks[i].astype(F32) * log2e, A_BLOCK).reshape(A_KV_HEADS, 1, A_GROUP * A_BLOCK)
        oat = _swa(qvt, ka.reshape(batch, seq, A_KV_WIDTH), band_bias, sinks, batch, seq, tm)
        norms = jnp.sqrt(jnp.max(stats.reshape(batch, seq // tm, 8, LANES), axis=1))[:, :2, :B_HEADS]
        qmax, kmax = norms[:, 0], norms[:, 1]
        ob = lax.cond(jnp.max(kmax * qmax) < MAX_SAFE_SHIFT,
                      lambda: _flash_bounded(kmax, qt, kb, vt, blk, 4),
                      lambda: _flash(qt, kb, vt, blk, 4))

        x2 = _final(x2, p[i].reshape(t, PLE_DIM), oat, za,
                    ob.reshape(t, B_WIDTH), zb, ga, gb,
                    w_o_a[i].astype(BF16), w_o_b[i].astype(BF16), w_out[i].astype(BF16),
                    row(ple_norm_g[i]), w_ple_gate[i].astype(BF16), w_ple_proj[i].astype(BF16),
                    row(ple_post_g[i]), tm)
    return x2.reshape(batch, seq, D_MODEL)
```

```python
import functools
import math

import numpy as np
import jax
import jax.numpy as jnp
from jax import lax
from jax.experimental import pallas as pl
from jax.experimental.pallas import tpu as pltpu

F32 = jnp.float32
BF16 = jnp.bfloat16

D_MODEL = 1024
PLE_DIM = 256
EPS = 1e-6
NEG = -1e30

A_HEADS = 16
A_KV_HEADS = 2
A_HEAD_DIM = 64
A_WIDTH = A_HEADS * A_HEAD_DIM
A_KV_WIDTH = A_KV_HEADS * A_HEAD_DIM
A_GROUP = A_HEADS // A_KV_HEADS
WINDOW = 128
A_BLOCK = 128
N_BUCKETS = 32
MAX_DISTANCE = 128

B_HEADS = 16
B_Q_RANK = 256
B_KV_RANK = 128
B_NOPE = 64
B_ROPE = 32
B_QK_DIM = B_NOPE + B_ROPE
B_VDIM = 64
B_WIDTH = B_HEADS * B_VDIM
ROPE_THETA = 10000.0

SPLIT_SIZES = (A_WIDTH, A_KV_WIDTH, A_KV_WIDTH, A_WIDTH,
               B_Q_RANK, B_KV_RANK, B_ROPE, B_WIDTH,
               D_MODEL, D_MODEL)

LANES = 128
QK_PAD = 128
V_ROWS = 80
VMEM_LIMIT = 56 * 1024 * 1024
MAX_SAFE_SHIFT = 60.0

QVT_ROWS = A_WIDTH + A_KV_WIDTH
R_KA = QVT_ROWS
R_KR = R_KA + A_KV_WIDTH
WT_ROWS = R_KR + B_ROPE
SWA_V_ROWS = A_HEAD_DIM + 16
C_ZA = 0
C_LAT = C_ZA + A_WIDTH
LAT_W = B_Q_RANK + B_KV_RANK
C_ZB = C_LAT + LAT_W
C_GA = C_ZB + B_WIDTH
C_GB = C_GA + D_MODEL
C_END = C_GB + D_MODEL


def _dot(a, b):
    return jnp.dot(a, b, preferred_element_type=F32)


def _dot_nt(a, b):
    return lax.dot_general(a, b, (((1,), (1,)), ((), ())), preferred_element_type=F32)


def _row_rms(v, width):
    return v * lax.rsqrt(jnp.sum(v * v, axis=-1, keepdims=True) * (1.0 / width) + EPS)


def _col_rms(v):
    return v * lax.rsqrt(jnp.sum(v * v, axis=0, keepdims=True) * (1.0 / v.shape[0]) + EPS)


def _proj_kernel(x_ref, pos_ref, g_ref, wt_ref, w_ref, gq_ref, gk_ref,
                 gcq_ref, wuqt_ref, gqn_ref, gckv_ref, wukt_ref, wuvt_ref, gkn_ref, gkr_ref, freq_ref,
                 qvt_ref, ka_ref, za_ref, zb_ref, ga_ref, gb_ref, qt_ref, k_ref, vt_ref):
    x = x_ref[...]
    tm = x.shape[0]
    h = (_row_rms(x, D_MODEL) * g_ref[...]).astype(BF16)

    wt_out = _dot_nt(wt_ref[...], h)
    gq = gq_ref[...]
    for hd in range(A_HEADS):
        blk = wt_out[hd * A_HEAD_DIM:(hd + 1) * A_HEAD_DIM, :]
        qvt_ref[hd * A_HEAD_DIM:(hd + 1) * A_HEAD_DIM, :] = (_col_rms(blk) * gq).astype(BF16)
    qvt_ref[A_WIDTH:, :] = wt_out[A_WIDTH:QVT_ROWS, :].astype(BF16)
    gk = gk_ref[...]
    kat = [_col_rms(wt_out[R_KA + g * A_HEAD_DIM:R_KA + (g + 1) * A_HEAD_DIM, :]) * gk
           for g in range(A_KV_HEADS)]
    ka_ref[...] = jnp.concatenate(kat, axis=0).T.astype(BF16)
    krt = wt_out[R_KR:, :]

    def proj(c0, c1):
        return _dot(h, w_ref[:, c0:c1])

    za_ref[...] = proj(C_ZA, C_LAT).astype(BF16)
    lat = proj(C_LAT, C_ZB)
    zb_ref[...] = proj(C_ZB, C_GA).astype(BF16)
    ga_ref[...] = proj(C_GA, C_GB).astype(BF16)
    gb_ref[...] = proj(C_GB, C_END).astype(BF16)

    cq = (_row_rms(lat[:, :B_Q_RANK], B_Q_RANK) * gcq_ref[...]).astype(BF16)
    qt = _dot_nt(wuqt_ref[...], cq)
    ang = freq_ref[...] * pos_ref[...].astype(F32)
    cos = jnp.cos(ang)
    sin = jnp.sin(ang)
    gqn = gqn_ref[...]
    half = B_ROPE // 2
    zeros_q = jnp.zeros((QK_PAD - B_QK_DIM, tm), BF16)
    for hd in range(B_HEADS):
        qn = _col_rms(qt[hd * B_QK_DIM:(hd + 1) * B_QK_DIM, :]) * gqn
        x1 = qn[B_NOPE:B_NOPE + half, :]
        x2 = qn[B_NOPE + half:, :]
        qt_ref[0, hd, 0:B_NOPE, :] = qn[:B_NOPE, :].astype(BF16)
        qt_ref[0, hd, B_NOPE:B_NOPE + half, :] = (x1 * cos - x2 * sin).astype(BF16)
        qt_ref[0, hd, B_NOPE + half:B_QK_DIM, :] = (x2 * cos + x1 * sin).astype(BF16)
        qt_ref[0, hd, B_QK_DIM:, :] = zeros_q

    ckv = (_row_rms(lat[:, B_Q_RANK:], B_KV_RANK) * gckv_ref[...]).astype(BF16)
    knt = _dot_nt(wukt_ref[...], ckv)
    vt = _dot_nt(wuvt_ref[...], ckv)

    krn = _col_rms(krt) * gkr_ref[...]
    x1 = krn[:half, :]
    x2 = krn[half:, :]
    k_tail = jnp.concatenate([x1 * cos - x2 * sin, x2 * cos + x1 * sin,
                              jnp.zeros((QK_PAD - B_QK_DIM, tm), F32)], axis=0)
    gkn = gkn_ref[...]
    for hd in range(B_HEADS):
        kn = _col_rms(knt[hd * B_NOPE:(hd + 1) * B_NOPE, :]) * gkn
        k_ref[0, hd] = jnp.concatenate([kn, k_tail], axis=0).T.astype(BF16)
    tk = vt_ref.shape[-1]
    ones_rows = (lax.broadcasted_iota(jnp.int32, (V_ROWS - B_VDIM, tk), 0) == 0).astype(BF16)
    for hd in range(B_HEADS):
        for c in range(tm // tk):
            vt_ref[0, hd, c, 0:B_VDIM, :] = vt[hd * B_VDIM:(hd + 1) * B_VDIM, c * tk:(c + 1) * tk].astype(BF16)
            vt_ref[0, hd, c, B_VDIM:, :] = ones_rows


def _proj(x2, pos_row, consts, batch, seq, tm, tk):
    t = x2.shape[0]
    nt = seq // tm
    nc = tm // tk
    row = lambda w: pl.BlockSpec((tm, w), lambda i: (i, 0))
    full = lambda a: pl.BlockSpec(a.shape, lambda i: (0,) * a.ndim)
    widths = (A_KV_WIDTH, A_WIDTH, B_WIDTH, D_MODEL, D_MODEL)
    return pl.pallas_call(
        _proj_kernel,
        grid=(t // tm,),
        in_specs=[row(D_MODEL), pl.BlockSpec((1, tm), lambda i: (0, i))] + [full(a) for a in consts],
        out_specs=[pl.BlockSpec((QVT_ROWS, tm), lambda i: (0, i))] + [row(w) for w in widths]
        + [pl.BlockSpec((1, B_HEADS, QK_PAD, tm), lambda i: (i // nt, 0, 0, i % nt)),
           pl.BlockSpec((1, B_HEADS, tm, QK_PAD), lambda i: (i // nt, 0, i % nt, 0)),
           pl.BlockSpec((1, B_HEADS, nc, V_ROWS, tk), lambda i: (i // nt, 0, i % nt, 0, 0))],
        out_shape=[jax.ShapeDtypeStruct((QVT_ROWS, t), BF16)]
        + [jax.ShapeDtypeStruct((t, w), BF16) for w in widths]
        + [jax.ShapeDtypeStruct((batch, B_HEADS, QK_PAD, seq), BF16),
           jax.ShapeDtypeStruct((batch, B_HEADS, seq, QK_PAD), BF16),
           jax.ShapeDtypeStruct((batch, B_HEADS, nt * nc, V_ROWS, tk), BF16)],
        compiler_params=pltpu.CompilerParams(
            dimension_semantics=("arbitrary",), vmem_limit_bytes=VMEM_LIMIT),
        name="proj",
    )(x2, pos_row, *consts)


def _band_bias_kernel(rbt_ref, onehot_ref, o_ref):
    rbt = rbt_ref[...]
    onehot = onehot_ref[...]
    t = jnp.zeros((A_HEADS, WINDOW), F32)
    for b in range(N_BUCKETS):
        t = t + rbt[:, b:b + 1] * onehot[b:b + 1, :]
    t = t * math.log2(math.e)
    band = 2 * A_BLOCK
    key = lax.broadcasted_iota(jnp.int32, (band, band), 0)
    neg = jnp.full((1, A_BLOCK), NEG, F32)
    for hd in range(A_HEADS):
        x = jnp.broadcast_to(jnp.concatenate([neg, t[hd:hd + 1, :]], axis=1), (band, band))
        for bit in range(8):
            x = jnp.where((key >> bit) & 1 == 1, pltpu.roll(x, 1 << bit, 1), x)
        g, hh = divmod(hd, A_GROUP)
        o_ref[g, :, hh * A_BLOCK:(hh + 1) * A_BLOCK] = x[:, :A_BLOCK]


def _band_bias(rel_bias):
    dist = np.arange(WINDOW)
    max_exact = N_BUCKETS // 2
    large = max_exact + (np.log(np.maximum(dist, 1).astype(np.float32) / max_exact)
                         / math.log(MAX_DISTANCE / max_exact) * (N_BUCKETS - max_exact)).astype(np.int32)
    bucket = np.where(dist < max_exact, dist, np.minimum(large, N_BUCKETS - 1))
    onehot = jnp.asarray(np.arange(N_BUCKETS)[:, None] == bucket[None, :], F32)
    return pl.pallas_call(
        _band_bias_kernel,
        out_shape=jax.ShapeDtypeStruct((A_KV_HEADS, 2 * A_BLOCK, A_GROUP * A_BLOCK), F32),
        name="band_bias",
    )(rel_bias.astype(F32).T, onehot)


def _swa_kernel(qv_ref, k_ref, kp_ref, vp_ref, bias_ref, sink_ref, o_ref, *, nsub):
    first = pl.program_id(1) == 0
    band = 2 * A_BLOCK
    width = A_GROUP * A_BLOCK
    pad = jnp.logical_and(first, lax.broadcasted_iota(jnp.int32, (band, width), 0) < A_BLOCK)
    zeros_q = jnp.zeros((A_HEAD_DIM, width), BF16)
    ones_rows = (lax.broadcasted_iota(jnp.int32, (SWA_V_ROWS - A_HEAD_DIM, band), 0) == 0).astype(BF16)

    def scores(sb, g):
        r0 = sb * A_BLOCK
        if sb == 0:
            kb = jnp.concatenate([kp_ref[0], k_ref[0, 0:A_BLOCK, :]], axis=0)
        else:
            kb = k_ref[0, r0 - A_BLOCK:r0 + A_BLOCK, :]
        tiles = [qv_ref[(g * A_GROUP + hh) * A_HEAD_DIM:(g * A_GROUP + hh + 1) * A_HEAD_DIM, r0:r0 + A_BLOCK]
                 for hh in range(A_GROUP)]
        qt = jnp.concatenate(tiles, axis=1)
        rhs = jnp.concatenate([qt, zeros_q] if g == 0 else [zeros_q, qt], axis=0)
        s = _dot(kb, rhs) + bias_ref[g]
        if sb == 0:
            s = jnp.where(pad, NEG, s)
        return s

    def finish(sb, g, s):
        r0 = sb * A_BLOCK
        sink = sink_ref[g]
        m = jnp.maximum(jnp.max(s, axis=0, keepdims=True), sink)
        e = jnp.exp2(s - m).astype(BF16)
        v0 = A_WIDTH + g * A_HEAD_DIM
        if sb == 0:
            vt = jnp.concatenate([vp_ref[g * A_HEAD_DIM:(g + 1) * A_HEAD_DIM, :],
                                  qv_ref[v0:v0 + A_HEAD_DIM, 0:A_BLOCK]], axis=1)
        else:
            vt = qv_ref[v0:v0 + A_HEAD_DIM, r0 - A_BLOCK:r0 + A_BLOCK]
        acc = _dot(jnp.concatenate([vt, ones_rows], axis=0), e)
        den = acc[A_HEAD_DIM:A_HEAD_DIM + 1, :] + jnp.exp2(sink - m)
        o = (acc[:A_HEAD_DIM, :] / den).astype(BF16)
        for hh in range(A_GROUP):
            hd = g * A_GROUP + hh
            o_ref[hd * A_HEAD_DIM:(hd + 1) * A_HEAD_DIM, r0:r0 + A_BLOCK] = o[:, hh * A_BLOCK:(hh + 1) * A_BLOCK]

    units = [(sb, g) for sb in range(nsub) for g in range(A_KV_HEADS)]
    s_next = scores(*units[0])
    for n, unit in enumerate(units):
        s_cur = s_next
        if n + 1 < len(units):
            s_next = scores(*units[n + 1])
        finish(*unit, s_cur)


def _swa(qvt, ka, bias, sinks, batch, seq, tq):
    nsub = tq // A_BLOCK
    nt = seq // tq
    vrow = A_WIDTH // A_KV_WIDTH
    return pl.pallas_call(
        functools.partial(_swa_kernel, nsub=nsub),
        grid=(batch, nt),
        in_specs=[pl.BlockSpec((QVT_ROWS, tq), lambda b, i: (0, b * nt + i)),
                  pl.BlockSpec((1, tq, A_KV_WIDTH), lambda b, i: (b, i, 0)),
                  pl.BlockSpec((1, A_BLOCK, A_KV_WIDTH), lambda b, i: (b, jnp.maximum(i * nsub - 1, 0), 0)),
                  pl.BlockSpec((A_KV_WIDTH, A_BLOCK),
                               lambda b, i: (vrow, jnp.maximum((b * nt + i) * nsub - 1, 0))),
                  pl.BlockSpec(bias.shape, lambda b, i: (0, 0, 0)),
                  pl.BlockSpec(sinks.shape, lambda b, i: (0, 0, 0))],
        out_specs=pl.BlockSpec((A_WIDTH, tq), lambda b, i: (0, b * nt + i)),
        out_shape=jax.ShapeDtypeStruct((A_WIDTH, batch * seq), BF16),
        compiler_params=pltpu.CompilerParams(
            dimension_semantics=("arbitrary", "arbitrary"), vmem_limit_bytes=VMEM_LIMIT),
        name="swa",
    )(qvt, ka, ka, qvt, bias, sinks)


def _flash_kernel(qt_ref, k_ref, vt_ref, o_ref, m_sc, acc_sc, s0_sc, s1_sc, bm0_sc, bm1_sc, *, tq, nh):
    tk = tq // 2
    i = pl.program_id(2)
    m_sc[...] = jnp.full(m_sc.shape, NEG, F32)
    acc_sc[...] = jnp.zeros(acc_sc.shape, F32)
    s_bufs = (s0_sc, s1_sc)
    bm_bufs = (bm0_sc, bm1_sc)

    def scores(h, j, slot, diag_half):
        kb = k_ref[0, h, pl.ds(pl.multiple_of(j * tk, tk), tk), :]
        s = _dot(kb, qt_ref[0, h])
        if diag_half is not None:
            key = lax.broadcasted_iota(jnp.int32, (tk, tq), 0) + diag_half * tk
            qry = lax.broadcasted_iota(jnp.int32, (tk, tq), 1)
            s = jnp.where(key <= qry, s, NEG)
        s_bufs[slot][h] = s
        bm_bufs[slot][h] = jnp.max(s, axis=0, keepdims=True)

    def softmax_pv(h, j, slot):
        m_old = m_sc[h]
        m_new = jnp.maximum(m_old, bm_bufs[slot][h])
        p = jnp.exp2(s_bufs[slot][h] - m_new).astype(BF16)
        alpha = jnp.exp2(m_old - m_new)
        m_sc[h] = m_new
        acc_sc[h] = acc_sc[h] * alpha + _dot(vt_ref[0, h, j], p)

    def stage(j, slot, next_diag_half, has_next=True):
        for h in range(nh):
            if has_next:
                scores(h, j + 1, 1 - slot, next_diag_half)
            softmax_pv(h, j, slot)

    @pl.when(i == 0)
    def _():
        for h in range(nh):
            scores(h, 0, 0, 0)

    @pl.when(i > 0)
    def _():
        for h in range(nh):
            scores(h, 0, 0, None)

    def pair(jj, carry):
        stage(2 * jj, 0, None)
        stage(2 * jj + 1, 1, None)
        return carry

    lax.fori_loop(0, i - 1, pair, 0)

    @pl.when(i > 0)
    def _():
        stage(2 * i - 2, 0, None)
        stage(2 * i - 1, 1, 0)

    stage(2 * i, 0, 1)
    stage(2 * i + 1, 1, None, has_next=False)

    outs = []
    for h in range(nh):
        acc = acc_sc[h]
        outs.append(acc[:B_VDIM, :] / acc[B_VDIM:B_VDIM + 1, :])
    o_ref[0] = jnp.concatenate(outs, axis=0).T.astype(BF16)


def _flash(qt, k, vt, tq, nh):
    batch, heads, _, seq = qt.shape
    tk = tq // 2
    nq = seq // tq
    return pl.pallas_call(
        functools.partial(_flash_kernel, tq=tq, nh=nh),
        grid=(batch, heads // nh, nq),
        in_specs=[pl.BlockSpec((1, nh, QK_PAD, tq), lambda b, g, i: (b, g, 0, i)),
                  pl.BlockSpec((1, nh, seq, QK_PAD), lambda b, g, i: (b, g, 0, 0)),
                  pl.BlockSpec((1, nh, seq // tk, V_ROWS, tk), lambda b, g, i: (b, g, 0, 0, 0))],
        out_specs=pl.BlockSpec((1, tq, nh * B_VDIM), lambda b, g, i: (b, i, g)),
        out_shape=jax.ShapeDtypeStruct((batch, seq, B_WIDTH), BF16),
        scratch_shapes=[pltpu.VMEM((nh, 1, tq), F32), pltpu.VMEM((nh, V_ROWS, tq), F32),
                        pltpu.VMEM((nh, tk, tq), F32), pltpu.VMEM((nh, tk, tq), F32),
                        pltpu.VMEM((nh, 1, tq), F32), pltpu.VMEM((nh, 1, tq), F32)],
        compiler_params=pltpu.CompilerParams(
            dimension_semantics=("arbitrary", "arbitrary", "arbitrary"),
            vmem_limit_bytes=VMEM_LIMIT),
        name="flash",
    )(qt, k, vt)


def _flash_bounded_kernel(kmax_ref, qt_ref, k_ref, vt_ref, o_ref, mq_sc, acc_sc, p0_sc, p1_sc, *, tq, nh):
    tk = tq // 2
    i = pl.program_id(2)
    acc_sc[...] = jnp.zeros(acc_sc.shape, F32)
    for h in range(nh):
        q = qt_ref[0, h].astype(F32)
        mq_sc[h] = jnp.sqrt(jnp.sum(q * q, axis=0, keepdims=True)) * kmax_ref[0]
    p_bufs = (p0_sc, p1_sc)

    def probs(h, j, slot, diag_half):
        kb = k_ref[0, h, pl.ds(pl.multiple_of(j * tk, tk), tk), :]
        s = _dot(kb, qt_ref[0, h])
        if diag_half is not None:
            key = lax.broadcasted_iota(jnp.int32, (tk, tq), 0) + diag_half * tk
            qry = lax.broadcasted_iota(jnp.int32, (tk, tq), 1)
            s = jnp.where(key <= qry, s, NEG)
        p_bufs[slot][h] = jnp.exp2(s - mq_sc[h]).astype(BF16)

    def pv(h, j, slot):
        acc_sc[h] += _dot(vt_ref[0, h, j], p_bufs[slot][h])

    def stage(j, slot, next_diag_half, has_next=True):
        for h in range(nh):
            if has_next:
                probs(h, j + 1, 1 - slot, next_diag_half)
            pv(h, j, slot)

    @pl.when(i == 0)
    def _():
        for h in range(nh):
            probs(h, 0, 0, 0)

    @pl.when(i > 0)
    def _():
        for h in range(nh):
            probs(h, 0, 0, None)

    def pair(jj, carry):
        stage(2 * jj, 0, None)
        stage(2 * jj + 1, 1, None)
        return carry

    lax.fori_loop(0, i - 1, pair, 0)

    @pl.when(i > 0)
    def _():
        stage(2 * i - 2, 0, None)
        stage(2 * i - 1, 1, 0)

    stage(2 * i, 0, 1)
    stage(2 * i + 1, 1, None, has_next=False)

    outs = []
    for h in range(nh):
        acc = acc_sc[h]
        outs.append(acc[:B_VDIM, :] / acc[B_VDIM:B_VDIM + 1, :])
    o_ref[0] = jnp.concatenate(outs, axis=0).T.astype(BF16)


def _flash_bounded(kmax, qt, k, vt, tq, nh):
    batch, heads, _, seq = qt.shape
    tk = tq // 2
    nq = seq // tq
    return pl.pallas_call(
        functools.partial(_flash_bounded_kernel, tq=tq, nh=nh),
        grid=(batch, heads // nh, nq),
        in_specs=[pl.BlockSpec(memory_space=pltpu.SMEM),
                  pl.BlockSpec((1, nh, QK_PAD, tq), lambda b, g, i: (b, g, 0, i)),
                  pl.BlockSpec((1, nh, seq, QK_PAD), lambda b, g, i: (b, g, 0, 0)),
                  pl.BlockSpec((1, nh, seq // tk, V_ROWS, tk), lambda b, g, i: (b, g, 0, 0, 0))],
        out_specs=pl.BlockSpec((1, tq, nh * B_VDIM), lambda b, g, i: (b, i, g)),
        out_shape=jax.ShapeDtypeStruct((batch, seq, B_WIDTH), BF16),
        scratch_shapes=[pltpu.VMEM((nh, 1, tq), F32), pltpu.VMEM((nh, V_ROWS, tq), F32),
                        pltpu.VMEM((nh, tk, tq), BF16), pltpu.VMEM((nh, tk, tq), BF16)],
        compiler_params=pltpu.CompilerParams(
            dimension_semantics=("arbitrary", "arbitrary", "arbitrary"),
            vmem_limit_bytes=VMEM_LIMIT),
        name="flash_bounded",
    )(kmax, qt, k, vt)


def _final_kernel(x_ref, p_ref, oat_ref, za_ref, ob_ref, zb_ref, ga_ref, gb_ref,
                  woa_ref, wob_ref, wout_ref, gpl_ref, wpg_ref, wpp_ref, gpost_ref, out_ref):
    oa = oat_ref[...].astype(F32).T
    ya = _dot((oa * jax.nn.silu(za_ref[...].astype(F32))).astype(BF16), woa_ref[...])
    yb = _dot((ob_ref[...].astype(F32) * jax.nn.silu(zb_ref[...].astype(F32))).astype(BF16), wob_ref[...])
    merged = (jax.nn.sigmoid(ga_ref[...].astype(F32)) * ya
              + jax.nn.sigmoid(gb_ref[...].astype(F32)) * yb)
    x1 = x_ref[...] + _dot(merged.astype(BF16), wout_ref[...])
    gate = jax.nn.sigmoid(_dot((_row_rms(x1, D_MODEL) * gpl_ref[...]).astype(BF16), wpg_ref[...]))
    emb = _row_rms(_dot(p_ref[...].astype(BF16), wpp_ref[...]), D_MODEL) * gpost_ref[...]
    out_ref[...] = x1 + gate * emb


def _final(x2, p2, oat, za, ob, zb, ga, gb, woa, wob, wout, gpl, wpg, wpp, gpost, tm):
    t = x2.shape[0]
    row = lambda a: pl.BlockSpec((tm, a.shape[1]), lambda i: (i, 0))
    full = lambda a: pl.BlockSpec(a.shape, lambda i: (0,) * a.ndim)
    acts = (x2, p2, oat, za, ob, zb, ga, gb)
    consts = (woa, wob, wout, gpl, wpg, wpp, gpost)
    act_specs = [row(a) for a in acts]
    act_specs[2] = pl.BlockSpec((A_WIDTH, tm), lambda i: (0, i))
    return pl.pallas_call(
        _final_kernel,
        grid=(t // tm,),
        in_specs=act_specs + [full(a) for a in consts],
        out_specs=pl.BlockSpec((tm, D_MODEL), lambda i: (i, 0)),
        out_shape=jax.ShapeDtypeStruct((t, D_MODEL), F32),
        compiler_params=pltpu.CompilerParams(
            dimension_semantics=("arbitrary",), vmem_limit_bytes=VMEM_LIMIT),
        name="final",
    )(*acts, *consts)


def kernel(x, p, positions, norm_g, w_in, a_q_norm, a_k_norm, a_sinks, rel_bias, w_o_a, b_cq_norm, w_uq, b_ckv_norm, w_uk, w_uv, b_q_norm, b_k_norm, b_kr_norm, w_o_b, w_out, ple_norm_g, w_ple_gate, w_ple_proj, ple_post_g):
    batch, seq, _ = x.shape
    depth = p.shape[0]
    t = batch * seq
    tm = 512
    blk = 512
    offs = np.concatenate([[0], np.cumsum(SPLIT_SIZES)])
    log2e = math.log2(math.e)
    inv_freq = (ROPE_THETA ** (-jnp.arange(0, B_ROPE, 2, dtype=F32) / B_ROPE))[:, None]
    pos_row = positions.reshape(1, t)
    band_bias = _band_bias(rel_bias)
    row = lambda v: v.astype(F32)[None, :]
    col = lambda v: v.astype(F32)[:, None]

    x2 = x.reshape(t, D_MODEL)
    for i in range(depth):
        w = w_in[i]
        cols = [w[:, offs[j]:offs[j + 1]] for j in range(len(SPLIT_SIZES))]
        wt = jnp.concatenate([cols[0], cols[2], cols[1], cols[6]], axis=1).T.astype(BF16)
        w1 = jnp.concatenate([cols[j] for j in (3, 4, 5, 7, 8, 9)], axis=1).astype(BF16)
        consts = (row(norm_g[i]), wt, w1,
                  col(a_q_norm[i]) * (A_HEAD_DIM ** -0.5 * log2e), col(a_k_norm[i]),
                  row(b_cq_norm[i]), w_uq[i].T.astype(BF16),
                  col(b_q_norm[i]) * (B_QK_DIM ** -0.5 * log2e),
                  row(b_ckv_norm[i]), w_uk[i].T.astype(BF16), w_uv[i].T.astype(BF16),
                  col(b_k_norm[i]), col(b_kr_norm[i]), inv_freq)
        qvt, ka, za, zb, ga, gb, qt, kb, vt = _proj(x2, pos_row, consts, batch, seq, tm, blk // 2)

        sinks = jnp.repeat(a_sinks[i].astype(F32) * log2e, A_BLOCK).reshape(A_KV_HEADS, 1, A_GROUP * A_BLOCK)
        oat = _swa(qvt, ka.reshape(batch, seq, A_KV_WIDTH), band_bias, sinks, batch, seq, tm)
        gqn = consts[7]
        q_bound = math.sqrt(B_QK_DIM) * jnp.max(jnp.abs(gqn))
        k_bound = jnp.sqrt(B_NOPE * jnp.max(jnp.square(b_k_norm[i].astype(F32)))
                           + B_ROPE * jnp.max(jnp.square(b_kr_norm[i].astype(F32))))
        ob = lax.cond(q_bound * k_bound < MAX_SAFE_SHIFT,
                      lambda: _flash_bounded(k_bound.reshape(1), qt, kb, vt, blk, 4),
                      lambda: _flash(qt, kb, vt, blk, 4))

        x2 = _final(x2, p[i].reshape(t, PLE_DIM), oat, za,
                    ob.reshape(t, B_WIDTH), zb, ga, gb,
                    w_o_a[i].astype(BF16), w_o_b[i].astype(BF16), w_out[i].astype(BF16),
                    row(ple_norm_g[i]), w_ple_gate[i].astype(BF16), w_ple_proj[i].astype(BF16),
                    row(ple_post_g[i]), tm)
    return x2.reshape(batch, seq, D_MODEL)
```

```python
import functools
import math

import numpy as np
import jax
import jax.numpy as jnp
from jax import lax
from jax.experimental import pallas as pl
from jax.experimental.pallas import tpu as pltpu

F32 = jnp.float32
BF16 = jnp.bfloat16

D_MODEL = 1024
PLE_DIM = 256
EPS = 1e-6
NEG = -1e30

A_HEADS = 16
A_KV_HEADS = 2
A_HEAD_DIM = 64
A_WIDTH = A_HEADS * A_HEAD_DIM
A_KV_WIDTH = A_KV_HEADS * A_HEAD_DIM
A_GROUP = A_HEADS // A_KV_HEADS
WINDOW = 128
A_BLOCK = 128
N_BUCKETS = 32
MAX_DISTANCE = 128

B_HEADS = 16
B_Q_RANK = 256
B_KV_RANK = 128
B_NOPE = 64
B_ROPE = 32
B_QK_DIM = B_NOPE + B_ROPE
B_VDIM = 64
B_WIDTH = B_HEADS * B_VDIM
ROPE_THETA = 10000.0

SPLIT_SIZES = (A_WIDTH, A_KV_WIDTH, A_KV_WIDTH, A_WIDTH,
               B_Q_RANK, B_KV_RANK, B_ROPE, B_WIDTH,
               D_MODEL, D_MODEL)

LANES = 128
QK_PAD = 128
V_ROWS = 80
VMEM_LIMIT = 56 * 1024 * 1024
MAX_SAFE_SHIFT = 60.0

QVT_ROWS = A_WIDTH + A_KV_WIDTH
R_KA = QVT_ROWS
R_KR = R_KA + A_KV_WIDTH
WT_ROWS = R_KR + B_ROPE
SWA_V_ROWS = A_HEAD_DIM + 16
C_ZA = 0
C_LAT = C_ZA + A_WIDTH
LAT_W = B_Q_RANK + B_KV_RANK
C_ZB = C_LAT + LAT_W
C_GA = C_ZB + B_WIDTH
C_GB = C_GA + D_MODEL
C_END = C_GB + D_MODEL


def _dot(a, b):
    return jnp.dot(a, b, preferred_element_type=F32)


def _dot_nt(a, b):
    return lax.dot_general(a, b, (((1,), (1,)), ((), ())), preferred_element_type=F32)


def _row_rms(v, width):
    return v * lax.rsqrt(jnp.sum(v * v, axis=-1, keepdims=True) * (1.0 / width) + EPS)


def _col_rms(v):
    return v * lax.rsqrt(jnp.sum(v * v, axis=0, keepdims=True) * (1.0 / v.shape[0]) + EPS)


def _proj_kernel(x_ref, pos_ref, g_ref, wt_ref, w_ref, gq_ref, gk_ref,
                 gcq_ref, wuqt_ref, gqn_ref, gckv_ref, wukt_ref, wuvt_ref, gkn_ref, gkr_ref, freq_ref,
                 qvt_ref, ka_ref, za_ref, zb_ref, ga_ref, gb_ref, qt_ref, k_ref, vt_ref):
    x = x_ref[...]
    tm = x.shape[0]
    h = (_row_rms(x, D_MODEL) * g_ref[...]).astype(BF16)

    wt_out = _dot_nt(wt_ref[...], h)
    gq = gq_ref[...]
    for hd in range(A_HEADS):
        blk = wt_out[hd * A_HEAD_DIM:(hd + 1) * A_HEAD_DIM, :]
        qvt_ref[hd * A_HEAD_DIM:(hd + 1) * A_HEAD_DIM, :] = (_col_rms(blk) * gq).astype(BF16)
    qvt_ref[A_WIDTH:, :] = wt_out[A_WIDTH:QVT_ROWS, :].astype(BF16)
    gk = gk_ref[...]
    kat = [_col_rms(wt_out[R_KA + g * A_HEAD_DIM:R_KA + (g + 1) * A_HEAD_DIM, :]) * gk
           for g in range(A_KV_HEADS)]
    ka_ref[...] = jnp.concatenate(kat, axis=0).T.astype(BF16)
    krt = wt_out[R_KR:, :]

    def proj(c0, c1):
        return _dot(h, w_ref[:, c0:c1])

    za_ref[...] = proj(C_ZA, C_LAT).astype(BF16)
    lat = proj(C_LAT, C_ZB)
    zb_ref[...] = proj(C_ZB, C_GA).astype(BF16)
    ga_ref[...] = proj(C_GA, C_GB).astype(BF16)
    gb_ref[...] = proj(C_GB, C_END).astype(BF16)

    cq = (_row_rms(lat[:, :B_Q_RANK], B_Q_RANK) * gcq_ref[...]).astype(BF16)
    qt = _dot_nt(wuqt_ref[...], cq)
    ang = freq_ref[...] * pos_ref[...].astype(F32)
    cos = jnp.cos(ang)
    sin = jnp.sin(ang)
    gqn = gqn_ref[...]
    half = B_ROPE // 2
    zeros_q = jnp.zeros((QK_PAD - B_QK_DIM, tm), BF16)
    for hd in range(B_HEADS):
        qn = _col_rms(qt[hd * B_QK_DIM:(hd + 1) * B_QK_DIM, :]) * gqn
        x1 = qn[B_NOPE:B_NOPE + half, :]
        x2 = qn[B_NOPE + half:, :]
        qt_ref[0, hd, 0:B_NOPE, :] = qn[:B_NOPE, :].astype(BF16)
        qt_ref[0, hd, B_NOPE:B_NOPE + half, :] = (x1 * cos - x2 * sin).astype(BF16)
        qt_ref[0, hd, B_NOPE + half:B_QK_DIM, :] = (x2 * cos + x1 * sin).astype(BF16)
        qt_ref[0, hd, B_QK_DIM:, :] = zeros_q

    ckv = (_row_rms(lat[:, B_Q_RANK:], B_KV_RANK) * gckv_ref[...]).astype(BF16)
    knt = _dot_nt(wukt_ref[...], ckv)
    vt = _dot_nt(wuvt_ref[...], ckv)

    krn = _col_rms(krt) * gkr_ref[...]
    x1 = krn[:half, :]
    x2 = krn[half:, :]
    k_tail = jnp.concatenate([x1 * cos - x2 * sin, x2 * cos + x1 * sin,
                              jnp.zeros((QK_PAD - B_QK_DIM, tm), F32)], axis=0)
    gkn = gkn_ref[...]
    for hd in range(B_HEADS):
        kn = _col_rms(knt[hd * B_NOPE:(hd + 1) * B_NOPE, :]) * gkn
        k_ref[0, hd] = jnp.concatenate([kn, k_tail], axis=0).T.astype(BF16)
    tk = vt_ref.shape[-1]
    ones_rows = (lax.broadcasted_iota(jnp.int32, (V_ROWS - B_VDIM, tk), 0) == 0).astype(BF16)
    for hd in range(B_HEADS):
        for c in range(tm // tk):
            vt_ref[0, hd, c, 0:B_VDIM, :] = vt[hd * B_VDIM:(hd + 1) * B_VDIM, c * tk:(c + 1) * tk].astype(BF16)
            vt_ref[0, hd, c, B_VDIM:, :] = ones_rows


def _proj(x2, pos_row, consts, batch, seq, tm, tk):
    t = x2.shape[0]
    nt = seq // tm
    nc = tm // tk
    row = lambda w: pl.BlockSpec((tm, w), lambda i: (i, 0))
    full = lambda a: pl.BlockSpec(a.shape, lambda i: (0,) * a.ndim)
    widths = (A_KV_WIDTH, A_WIDTH, B_WIDTH, D_MODEL, D_MODEL)
    return pl.pallas_call(
        _proj_kernel,
        grid=(t // tm,),
        in_specs=[row(D_MODEL), pl.BlockSpec((1, tm), lambda i: (0, i))] + [full(a) for a in consts],
        out_specs=[pl.BlockSpec((QVT_ROWS, tm), lambda i: (0, i))] + [row(w) for w in widths]
        + [pl.BlockSpec((1, B_HEADS, QK_PAD, tm), lambda i: (i // nt, 0, 0, i % nt)),
           pl.BlockSpec((1, B_HEADS, tm, QK_PAD), lambda i: (i // nt, 0, i % nt, 0)),
           pl.BlockSpec((1, B_HEADS, nc, V_ROWS, tk), lambda i: (i // nt, 0, i % nt, 0, 0))],
        out_shape=[jax.ShapeDtypeStruct((QVT_ROWS, t), BF16)]
        + [jax.ShapeDtypeStruct((t, w), BF16) for w in widths]
        + [jax.ShapeDtypeStruct((batch, B_HEADS, QK_PAD, seq), BF16),
           jax.ShapeDtypeStruct((batch, B_HEADS, seq, QK_PAD), BF16),
           jax.ShapeDtypeStruct((batch, B_HEADS, nt * nc, V_ROWS, tk), BF16)],
        compiler_params=pltpu.CompilerParams(
            dimension_semantics=("arbitrary",), vmem_limit_bytes=VMEM_LIMIT),
        name="proj",
    )(x2, pos_row, *consts)


def _band_bias_kernel(rbt_ref, onehot_ref, o_ref):
    rbt = rbt_ref[...]
    onehot = onehot_ref[...]
    t = jnp.zeros((A_HEADS, WINDOW), F32)
    for b in range(N_BUCKETS):
        t = t + rbt[:, b:b + 1] * onehot[b:b + 1, :]
    t = t * math.log2(math.e)
    band = 2 * A_BLOCK
    key = lax.broadcasted_iota(jnp.int32, (band, band), 0)
    neg = jnp.full((1, A_BLOCK), NEG, F32)
    for hd in range(A_HEADS):
        x = jnp.broadcast_to(jnp.concatenate([neg, t[hd:hd + 1, :]], axis=1), (band, band))
        for bit in range(8):
            x = jnp.where((key >> bit) & 1 == 1, pltpu.roll(x, 1 << bit, 1), x)
        g, hh = divmod(hd, A_GROUP)
        o_ref[g, :, hh * A_BLOCK:(hh + 1) * A_BLOCK] = x[:, :A_BLOCK]


def _band_bias(rel_bias):
    dist = np.arange(WINDOW)
    max_exact = N_BUCKETS // 2
    large = max_exact + (np.log(np.maximum(dist, 1).astype(np.float32) / max_exact)
                         / math.log(MAX_DISTANCE / max_exact) * (N_BUCKETS - max_exact)).astype(np.int32)
    bucket = np.where(dist < max_exact, dist, np.minimum(large, N_BUCKETS - 1))
    onehot = jnp.asarray(np.arange(N_BUCKETS)[:, None] == bucket[None, :], F32)
    return pl.pallas_call(
        _band_bias_kernel,
        out_shape=jax.ShapeDtypeStruct((A_KV_HEADS, 2 * A_BLOCK, A_GROUP * A_BLOCK), F32),
        name="band_bias",
    )(rel_bias.astype(F32).T, onehot)


def _swa_kernel(qv_ref, k_ref, kp_ref, vp_ref, bias_ref, sink_ref, o_ref, *, nsub):
    first = pl.program_id(1) == 0
    band = 2 * A_BLOCK
    width = A_GROUP * A_BLOCK
    pad = jnp.logical_and(first, lax.broadcasted_iota(jnp.int32, (band, width), 0) < A_BLOCK)
    zeros_q = jnp.zeros((A_HEAD_DIM, width), BF16)
    ones_rows = (lax.broadcasted_iota(jnp.int32, (SWA_V_ROWS - A_HEAD_DIM, band), 0) == 0).astype(BF16)

    def scores(sb, g):
        r0 = sb * A_BLOCK
        if sb == 0:
            kb = jnp.concatenate([kp_ref[0], k_ref[0, 0:A_BLOCK, :]], axis=0)
        else:
            kb = k_ref[0, r0 - A_BLOCK:r0 + A_BLOCK, :]
        tiles = [qv_ref[(g * A_GROUP + hh) * A_HEAD_DIM:(g * A_GROUP + hh + 1) * A_HEAD_DIM, r0:r0 + A_BLOCK]
                 for hh in range(A_GROUP)]
        qt = jnp.concatenate(tiles, axis=1)
        rhs = jnp.concatenate([qt, zeros_q] if g == 0 else [zeros_q, qt], axis=0)
        s = _dot(kb, rhs) + bias_ref[g]
        if sb == 0:
            s = jnp.where(pad, NEG, s)
        return s

    def finish(sb, g, s):
        r0 = sb * A_BLOCK
        sink = sink_ref[g]
        m = jnp.maximum(jnp.max(s, axis=0, keepdims=True), sink)
        e = jnp.exp2(s - m).astype(BF16)
        v0 = A_WIDTH + g * A_HEAD_DIM
        if sb == 0:
            vt = jnp.concatenate([vp_ref[g * A_HEAD_DIM:(g + 1) * A_HEAD_DIM, :],
                                  qv_ref[v0:v0 + A_HEAD_DIM, 0:A_BLOCK]], axis=1)
        else:
            vt = qv_ref[v0:v0 + A_HEAD_DIM, r0 - A_BLOCK:r0 + A_BLOCK]
        acc = _dot(jnp.concatenate([vt, ones_rows], axis=0), e)
        den = acc[A_HEAD_DIM:A_HEAD_DIM + 1, :] + jnp.exp2(sink - m)
        o = (acc[:A_HEAD_DIM, :] / den).astype(BF16)
        for hh in range(A_GROUP):
            hd = g * A_GROUP + hh
            o_ref[hd * A_HEAD_DIM:(hd + 1) * A_HEAD_DIM, r0:r0 + A_BLOCK] = o[:, hh * A_BLOCK:(hh + 1) * A_BLOCK]

    units = [(sb, g) for sb in range(nsub) for g in range(A_KV_HEADS)]
    s_next = scores(*units[0])
    for n, unit in enumerate(units):
        s_cur = s_next
        if n + 1 < len(units):
            s_next = scores(*units[n + 1])
        finish(*unit, s_cur)


def _swa(qvt, ka, bias, sinks, batch, seq, tq):
    nsub = tq // A_BLOCK
    nt = seq // tq
    vrow = A_WIDTH // A_KV_WIDTH
    return pl.pallas_call(
        functools.partial(_swa_kernel, nsub=nsub),
        grid=(batch, nt),
        in_specs=[pl.BlockSpec((QVT_ROWS, tq), lambda b, i: (0, b * nt + i)),
                  pl.BlockSpec((1, tq, A_KV_WIDTH), lambda b, i: (b, i, 0)),
                  pl.BlockSpec((1, A_BLOCK, A_KV_WIDTH), lambda b, i: (b, jnp.maximum(i * nsub - 1, 0), 0)),
                  pl.BlockSpec((A_KV_WIDTH, A_BLOCK),
                               lambda b, i: (vrow, jnp.maximum((b * nt + i) * nsub - 1, 0))),
                  pl.BlockSpec(bias.shape, lambda b, i: (0, 0, 0)),
                  pl.BlockSpec(sinks.shape, lambda b, i: (0, 0, 0))],
        out_specs=pl.BlockSpec((A_WIDTH, tq), lambda b, i: (0, b * nt + i)),
        out_shape=jax.ShapeDtypeStruct((A_WIDTH, batch * seq), BF16),
        compiler_params=pltpu.CompilerParams(
            dimension_semantics=("arbitrary", "arbitrary"), vmem_limit_bytes=VMEM_LIMIT),
        name="swa",
    )(qvt, ka, ka, qvt, bias, sinks)


def _flash_kernel(qt_ref, k_ref, vt_ref, o_ref, m_sc, acc_sc, s0_sc, s1_sc, bm0_sc, bm1_sc, *, tq, nh):
    tk = tq // 2
    i = pl.program_id(2)
    m_sc[...] = jnp.full(m_sc.shape, NEG, F32)
    acc_sc[...] = jnp.zeros(acc_sc.shape, F32)
    s_bufs = (s0_sc, s1_sc)
    bm_bufs = (bm0_sc, bm1_sc)

    def scores(h, j, slot, diag_half):
        kb = k_ref[0, h, pl.ds(pl.multiple_of(j * tk, tk), tk), :]
        s = _dot(kb, qt_ref[0, h])
        if diag_half is not None:
            key = lax.broadcasted_iota(jnp.int32, (tk, tq), 0) + diag_half * tk
            qry = lax.broadcasted_iota(jnp.int32, (tk, tq), 1)
            s = jnp.where(key <= qry, s, NEG)
        s_bufs[slot][h] = s
        bm_bufs[slot][h] = jnp.max(s, axis=0, keepdims=True)

    def softmax_pv(h, j, slot):
        m_old = m_sc[h]
        m_new = jnp.maximum(m_old, bm_bufs[slot][h])
        p = jnp.exp2(s_bufs[slot][h] - m_new).astype(BF16)
        alpha = jnp.exp2(m_old - m_new)
        m_sc[h] = m_new
        acc_sc[h] = acc_sc[h] * alpha + _dot(vt_ref[0, h, j], p)

    def stage(j, slot, next_diag_half, has_next=True):
        for h in range(nh):
            if has_next:
                scores(h, j + 1, 1 - slot, next_diag_half)
            softmax_pv(h, j, slot)

    @pl.when(i == 0)
    def _():
        for h in range(nh):
            scores(h, 0, 0, 0)

    @pl.when(i > 0)
    def _():
        for h in range(nh):
            scores(h, 0, 0, None)

    def pair(jj, carry):
        stage(2 * jj, 0, None)
        stage(2 * jj + 1, 1, None)
        return carry

    lax.fori_loop(0, i - 1, pair, 0)

    @pl.when(i > 0)
    def _():
        stage(2 * i - 2, 0, None)
        stage(2 * i - 1, 1, 0)

    stage(2 * i, 0, 1)
    stage(2 * i + 1, 1, None, has_next=False)

    outs = []
    for h in range(nh):
        acc = acc_sc[h]
        outs.append(acc[:B_VDIM, :] / acc[B_VDIM:B_VDIM + 1, :])
    o_ref[0] = jnp.concatenate(outs, axis=0).T.astype(BF16)


def _flash(qt, k, vt, tq, nh):
    batch, heads, _, seq = qt.shape
    tk = tq // 2
    nq = seq // tq
    return pl.pallas_call(
        functools.partial(_flash_kernel, tq=tq, nh=nh),
        grid=(batch, heads // nh, nq),
        in_specs=[pl.BlockSpec((1, nh, QK_PAD, tq), lambda b, g, i: (b, g, 0, i)),
                  pl.BlockSpec((1, nh, seq, QK_PAD), lambda b, g, i: (b, g, 0, 0)),
                  pl.BlockSpec((1, nh, seq // tk, V_ROWS, tk), lambda b, g, i: (b, g, 0, 0, 0))],
        out_specs=pl.BlockSpec((1, tq, nh * B_VDIM), lambda b, g, i: (b, i, g)),
        out_shape=jax.ShapeDtypeStruct((batch, seq, B_WIDTH), BF16),
        scratch_shapes=[pltpu.VMEM((nh, 1, tq), F32), pltpu.VMEM((nh, V_ROWS, tq), F32),
                        pltpu.VMEM((nh, tk, tq), F32), pltpu.VMEM((nh, tk, tq), F32),
                        pltpu.VMEM((nh, 1, tq), F32), pltpu.VMEM((nh, 1, tq), F32)],
        compiler_params=pltpu.CompilerParams(
            dimension_semantics=("arbitrary", "arbitrary", "arbitrary"),
            vmem_limit_bytes=VMEM_LIMIT),
        name="flash",
    )(qt, k, vt)


def _flash_bounded_kernel(kmax_ref, qt_ref, qtn_ref, k_ref, vt_ref, o_ref, mq_sc, mqn_sc, acc_sc, p0_sc, p1_sc,
                          *, tq, nh):
    tk = tq // 2
    i = pl.program_id(2)
    acc_sc[...] = jnp.zeros(acc_sc.shape, F32)

    def shift(q_ref, h):
        q = q_ref[0, h].astype(F32)
        return jnp.sqrt(jnp.sum(q * q, axis=0, keepdims=True)) * kmax_ref[0]

    for h in range(nh):
        mq_sc[h] = shift(qt_ref, h)
        mqn_sc[h] = shift(qtn_ref, h)
    p_bufs = (p0_sc, p1_sc)

    def probs(h, j, slot, diag_half, next_q=False):
        kb = k_ref[0, h, pl.ds(pl.multiple_of(j * tk, tk), tk), :]
        s = _dot(kb, (qtn_ref if next_q else qt_ref)[0, h])
        if diag_half is not None:
            key = lax.broadcasted_iota(jnp.int32, (tk, tq), 0) + diag_half * tk
            qry = lax.broadcasted_iota(jnp.int32, (tk, tq), 1)
            s = jnp.where(key <= qry, s, NEG)
        p_bufs[slot][h] = jnp.exp2(s - (mqn_sc if next_q else mq_sc)[h]).astype(BF16)

    def pv(h, j, slot):
        acc_sc[h] += _dot(vt_ref[0, h, j], p_bufs[slot][h])

    def stage(j, slot, next_diag_half, next_q=False):
        for h in range(nh):
            probs(h, 0 if next_q else j + 1, 1 - slot, next_diag_half, next_q)
            pv(h, j, slot)

    def pair(jj, last_next_diag_half=None):
        stage(2 * jj, 0, None)
        stage(2 * jj + 1, 1, last_next_diag_half)

    @pl.when(i == 0)
    def _():
        for h in range(nh):
            probs(h, 0, 0, 0)

    plain = i - 1

    def two_pairs(n, carry):
        pair(2 * n)
        pair(2 * n + 1)
        return carry

    lax.fori_loop(0, plain // 2, two_pairs, 0)

    @pl.when(jnp.logical_and(plain > 0, plain % 2 == 1))
    def _():
        pair(plain - 1)

    @pl.when(i > 0)
    def _():
        pair(i - 1, 0)

    stage(2 * i, 0, 1)
    stage(2 * i + 1, 1, None, next_q=True)

    outs = []
    for h in range(nh):
        acc = acc_sc[h]
        outs.append(acc[:B_VDIM, :] / acc[B_VDIM:B_VDIM + 1, :])
    o_ref[0] = jnp.concatenate(outs, axis=0).T.astype(BF16)


def _flash_bounded(kmax, qt, k, vt, tq, nh):
    batch, heads, _, seq = qt.shape
    tk = tq // 2
    nq = seq // tq
    return pl.pallas_call(
        functools.partial(_flash_bounded_kernel, tq=tq, nh=nh),
        grid=(batch, heads // nh, nq),
        in_specs=[pl.BlockSpec(memory_space=pltpu.SMEM),
                  pl.BlockSpec((1, nh, QK_PAD, tq), lambda b, g, i: (b, g, 0, i)),
                  pl.BlockSpec((1, nh, QK_PAD, tq), lambda b, g, i: (b, g, 0, jnp.minimum(i + 1, nq - 1))),
                  pl.BlockSpec((1, nh, seq, QK_PAD), lambda b, g, i: (b, g, 0, 0)),
                  pl.BlockSpec((1, nh, seq // tk, V_ROWS, tk), lambda b, g, i: (b, g, 0, 0, 0))],
        out_specs=pl.BlockSpec((1, tq, nh * B_VDIM), lambda b, g, i: (b, i, g)),
        out_shape=jax.ShapeDtypeStruct((batch, seq, B_WIDTH), BF16),
        scratch_shapes=[pltpu.VMEM((nh, 1, tq), F32), pltpu.VMEM((nh, 1, tq), F32),
                        pltpu.VMEM((nh, V_ROWS, tq), F32),
                        pltpu.VMEM((nh, tk, tq), BF16), pltpu.VMEM((nh, tk, tq), BF16)],
        compiler_params=pltpu.CompilerParams(
            dimension_semantics=("arbitrary", "arbitrary", "arbitrary"),
            vmem_limit_bytes=VMEM_LIMIT),
        name="flash_bounded",
    )(kmax, qt, qt, k, vt)


def _final_kernel(x_ref, p_ref, oat_ref, za_ref, ob_ref, zb_ref, ga_ref, gb_ref,
                  woa_ref, wob_ref, wout_ref, gpl_ref, wpg_ref, wpp_ref, gpost_ref, out_ref):
    oa = oat_ref[...].astype(F32).T
    ya = _dot((oa * jax.nn.silu(za_ref[...].astype(F32))).astype(BF16), woa_ref[...])
    yb = _dot((ob_ref[...].astype(F32) * jax.nn.silu(zb_ref[...].astype(F32))).astype(BF16), wob_ref[...])
    merged = (jax.nn.sigmoid(ga_ref[...].astype(F32)) * ya
              + jax.nn.sigmoid(gb_ref[...].astype(F32)) * yb)
    x1 = x_ref[...] + _dot(merged.astype(BF16), wout_ref[...])
    gate = jax.nn.sigmoid(_dot((_row_rms(x1, D_MODEL) * gpl_ref[...]).astype(BF16), wpg_ref[...]))
    emb = _row_rms(_dot(p_ref[...].astype(BF16), wpp_ref[...]), D_MODEL) * gpost_ref[...]
    out_ref[...] = x1 + gate * emb


def _final(x2, p2, oat, za, ob, zb, ga, gb, woa, wob, wout, gpl, wpg, wpp, gpost, tm):
    t = x2.shape[0]
    row = lambda a: pl.BlockSpec((tm, a.shape[1]), lambda i: (i, 0))
    full = lambda a: pl.BlockSpec(a.shape, lambda i: (0,) * a.ndim)
    acts = (x2, p2, oat, za, ob, zb, ga, gb)
    consts = (woa, wob, wout, gpl, wpg, wpp, gpost)
    act_specs = [row(a) for a in acts]
    act_specs[2] = pl.BlockSpec((A_WIDTH, tm), lambda i: (0, i))
    return pl.pallas_call(
        _final_kernel,
        grid=(t // tm,),
        in_specs=act_specs + [full(a) for a in consts],
        out_specs=pl.BlockSpec((tm, D_MODEL), lambda i: (i, 0)),
        out_shape=jax.ShapeDtypeStruct((t, D_MODEL), F32),
        compiler_params=pltpu.CompilerParams(
            dimension_semantics=("arbitrary",), vmem_limit_bytes=VMEM_LIMIT),
        name="final",
    )(*acts, *consts)


def kernel(x, p, positions, norm_g, w_in, a_q_norm, a_k_norm, a_sinks, rel_bias, w_o_a, b_cq_norm, w_uq, b_ckv_norm, w_uk, w_uv, b_q_norm, b_k_norm, b_kr_norm, w_o_b, w_out, ple_norm_g, w_ple_gate, w_ple_proj, ple_post_g):
    batch, seq, _ = x.shape
    depth = p.shape[0]
    t = batch * seq
    tm = 512
    blk = 512
    offs = np.concatenate([[0], np.cumsum(SPLIT_SIZES)])
    log2e = math.log2(math.e)
    inv_freq = (ROPE_THETA ** (-jnp.arange(0, B_ROPE, 2, dtype=F32) / B_ROPE))[:, None]
    pos_row = positions.reshape(1, t)
    band_bias = _band_bias(rel_bias)
    row = lambda v: v.astype(F32)[None, :]
    col = lambda v: v.astype(F32)[:, None]

    x2 = x.reshape(t, D_MODEL)
    for i in range(depth):
        w = w_in[i]
        cols = [w[:, offs[j]:offs[j + 1]] for j in range(len(SPLIT_SIZES))]
        wt = jnp.concatenate([cols[0], cols[2], cols[1], cols[6]], axis=1).T.astype(BF16)
        w1 = jnp.concatenate([cols[j] for j in (3, 4, 5, 7, 8, 9)], axis=1).astype(BF16)
        consts = (row(norm_g[i]), wt, w1,
                  col(a_q_norm[i]) * (A_HEAD_DIM ** -0.5 * log2e), col(a_k_norm[i]),
                  row(b_cq_norm[i]), w_uq[i].T.astype(BF16),
                  col(b_q_norm[i]) * (B_QK_DIM ** -0.5 * log2e),
                  row(b_ckv_norm[i]), w_uk[i].T.astype(BF16), w_uv[i].T.astype(BF16),
                  col(b_k_norm[i]), col(b_kr_norm[i]), inv_freq)
        qvt, ka, za, zb, ga, gb, qt, kb, vt = _proj(x2, pos_row, consts, batch, seq, tm, blk // 2)

        sinks = jnp.repeat(a_sinks[i].astype(F32) * log2e, A_BLOCK).reshape(A_KV_HEADS, 1, A_GROUP * A_BLOCK)
        oat = _swa(qvt, ka.reshape(batch, seq, A_KV_WIDTH), band_bias, sinks, batch, seq, tm)
        gqn = consts[7]
        q_bound = math.sqrt(B_QK_DIM) * jnp.max(jnp.abs(gqn))
        k_bound = jnp.sqrt(B_NOPE * jnp.max(jnp.square(b_k_norm[i].astype(F32)))
                           + B_ROPE * jnp.max(jnp.square(b_kr_norm[i].astype(F32))))
        ob = lax.cond(q_bound * k_bound < MAX_SAFE_SHIFT,
                      lambda: _flash_bounded(k_bound.reshape(1), qt, kb, vt, blk, 4),
                      lambda: _flash(qt, kb, vt, blk, 4))

        x2 = _final(x2, p[i].reshape(t, PLE_DIM), oat, za,
                    ob.reshape(t, B_WIDTH), zb, ga, gb,
                    w_o_a[i].astype(BF16), w_o_b[i].astype(BF16), w_out[i].astype(BF16),
                    row(ple_norm_g[i]), w_ple_gate[i].astype(BF16), w_ple_proj[i].astype(BF16),
                    row(ple_post_g[i]), tm)
    return x2.reshape(batch, seq, D_MODEL)
```

```python
import functools
import math

import numpy as np
import jax
import jax.numpy as jnp
from jax import lax
from jax.experimental import pallas as pl
from jax.experimental.pallas import tpu as pltpu

F32 = jnp.float32
BF16 = jnp.bfloat16

D_MODEL = 1024
PLE_DIM = 256
EPS = 1e-6
NEG = -1e30

A_HEADS = 16
A_KV_HEADS = 2
A_HEAD_DIM = 64
A_WIDTH = A_HEADS * A_HEAD_DIM
A_KV_WIDTH = A_KV_HEADS * A_HEAD_DIM
A_GROUP = A_HEADS // A_KV_HEADS
WINDOW = 128
A_BLOCK = 128
N_BUCKETS = 32
MAX_DISTANCE = 128

B_HEADS = 16
B_Q_RANK = 256
B_KV_RANK = 128
B_NOPE = 64
B_ROPE = 32
B_QK_DIM = B_NOPE + B_ROPE
B_VDIM = 64
B_WIDTH = B_HEADS * B_VDIM
ROPE_THETA = 10000.0

SPLIT_SIZES = (A_WIDTH, A_KV_WIDTH, A_KV_WIDTH, A_WIDTH,
               B_Q_RANK, B_KV_RANK, B_ROPE, B_WIDTH,
               D_MODEL, D_MODEL)

LANES = 128
QK_PAD = 128
V_ROWS = 80
VMEM_LIMIT = 56 * 1024 * 1024
MAX_SAFE_SHIFT = 60.0

TOKEN_TILE = 512
FLASH_Q_TILE = 512
FLASH_HEADS = 4
FINAL_CHAINS = 4

QVT_ROWS = A_WIDTH + A_KV_WIDTH
R_KA = QVT_ROWS
R_KR = R_KA + A_KV_WIDTH
WT_ROWS = R_KR + B_ROPE
SWA_V_ROWS = A_HEAD_DIM + 16
C_ZA = 0
C_LAT = C_ZA + A_WIDTH
LAT_W = B_Q_RANK + B_KV_RANK
C_ZB = C_LAT + LAT_W
C_GA = C_ZB + B_WIDTH
C_GB = C_GA + D_MODEL
C_END = C_GB + D_MODEL


def _dot(a, b):
    return jnp.dot(a, b, preferred_element_type=F32)


def _dot_nt(a, b):
    return lax.dot_general(a, b, (((1,), (1,)), ((), ())), preferred_element_type=F32)


def _row_rms(v, width):
    return v * lax.rsqrt(jnp.sum(v * v, axis=-1, keepdims=True) * (1.0 / width) + EPS)


def _col_rms(v):
    return v * lax.rsqrt(jnp.sum(v * v, axis=0, keepdims=True) * (1.0 / v.shape[0]) + EPS)


def _proj_kernel(x_ref, pos_ref, g_ref, wt_ref, w_ref, gq_ref, gk_ref,
                 gcq_ref, wuqt_ref, gqn_ref, gckv_ref, wukt_ref, wuvt_ref, gkn_ref, gkr_ref, freq_ref,
                 qvt_ref, ka_ref, za_ref, zb_ref, ga_ref, gb_ref, qt_ref, k_ref, vt_ref):
    x = x_ref[...]
    tm = x.shape[0]
    h = (_row_rms(x, D_MODEL) * g_ref[...]).astype(BF16)

    wt_out = _dot_nt(wt_ref[...], h)
    gq = gq_ref[...]
    for hd in range(A_HEADS):
        blk = wt_out[hd * A_HEAD_DIM:(hd + 1) * A_HEAD_DIM, :]
        qvt_ref[hd * A_HEAD_DIM:(hd + 1) * A_HEAD_DIM, :] = (_col_rms(blk) * gq).astype(BF16)
    qvt_ref[A_WIDTH:, :] = wt_out[A_WIDTH:QVT_ROWS, :].astype(BF16)
    gk = gk_ref[...]
    kat = [_col_rms(wt_out[R_KA + g * A_HEAD_DIM:R_KA + (g + 1) * A_HEAD_DIM, :]) * gk
           for g in range(A_KV_HEADS)]
    ka_ref[...] = jnp.concatenate(kat, axis=0).T.astype(BF16)
    krt = wt_out[R_KR:, :]

    def proj(c0, c1):
        return _dot(h, w_ref[:, c0:c1])

    za_ref[...] = proj(C_ZA, C_LAT).astype(BF16)
    lat = proj(C_LAT, C_ZB)
    zb_ref[...] = proj(C_ZB, C_GA).astype(BF16)
    ga_ref[...] = proj(C_GA, C_GB).astype(BF16)
    gb_ref[...] = proj(C_GB, C_END).astype(BF16)

    cq = (_row_rms(lat[:, :B_Q_RANK], B_Q_RANK) * gcq_ref[...]).astype(BF16)
    qt = _dot_nt(wuqt_ref[...], cq)
    ang = freq_ref[...] * pos_ref[...].astype(F32)
    cos = jnp.cos(ang)
    sin = jnp.sin(ang)
    gqn = gqn_ref[...]
    half = B_ROPE // 2
    zeros_q = jnp.zeros((QK_PAD - B_QK_DIM, tm), BF16)
    for hd in range(B_HEADS):
        qn = _col_rms(qt[hd * B_QK_DIM:(hd + 1) * B_QK_DIM, :]) * gqn
        x1 = qn[B_NOPE:B_NOPE + half, :]
        x2 = qn[B_NOPE + half:, :]
        qt_ref[0, hd, 0:B_NOPE, :] = qn[:B_NOPE, :].astype(BF16)
        qt_ref[0, hd, B_NOPE:B_NOPE + half, :] = (x1 * cos - x2 * sin).astype(BF16)
        qt_ref[0, hd, B_NOPE + half:B_QK_DIM, :] = (x2 * cos + x1 * sin).astype(BF16)
        qt_ref[0, hd, B_QK_DIM:, :] = zeros_q

    ckv = (_row_rms(lat[:, B_Q_RANK:], B_KV_RANK) * gckv_ref[...]).astype(BF16)
    knt = _dot_nt(wukt_ref[...], ckv)
    vt = _dot_nt(wuvt_ref[...], ckv)

    krn = _col_rms(krt) * gkr_ref[...]
    x1 = krn[:half, :]
    x2 = krn[half:, :]
    k_tail = jnp.concatenate([x1 * cos - x2 * sin, x2 * cos + x1 * sin,
                              jnp.zeros((QK_PAD - B_QK_DIM, tm), F32)], axis=0)
    gkn = gkn_ref[...]
    for hd in range(B_HEADS):
        kn = _col_rms(knt[hd * B_NOPE:(hd + 1) * B_NOPE, :]) * gkn
        k_ref[0, hd] = jnp.concatenate([kn, k_tail], axis=0).T.astype(BF16)
    tk = vt_ref.shape[-1]
    ones_rows = (lax.broadcasted_iota(jnp.int32, (V_ROWS - B_VDIM, tk), 0) == 0).astype(BF16)
    for hd in range(B_HEADS):
        for c in range(tm // tk):
            vt_ref[0, hd, c, 0:B_VDIM, :] = vt[hd * B_VDIM:(hd + 1) * B_VDIM, c * tk:(c + 1) * tk].astype(BF16)
            vt_ref[0, hd, c, B_VDIM:, :] = ones_rows


def _proj(x2, pos_row, consts, batch, seq, tm, tk):
    t = x2.shape[0]
    nt = seq // tm
    nc = tm // tk
    row = lambda w: pl.BlockSpec((tm, w), lambda i: (i, 0))
    full = lambda a: pl.BlockSpec(a.shape, lambda i: (0,) * a.ndim)
    widths = (A_KV_WIDTH, A_WIDTH, B_WIDTH, D_MODEL, D_MODEL)
    return pl.pallas_call(
        _proj_kernel,
        grid=(t // tm,),
        in_specs=[row(D_MODEL), pl.BlockSpec((1, tm), lambda i: (0, i))] + [full(a) for a in consts],
        out_specs=[pl.BlockSpec((QVT_ROWS, tm), lambda i: (0, i))] + [row(w) for w in widths]
        + [pl.BlockSpec((1, B_HEADS, QK_PAD, tm), lambda i: (i // nt, 0, 0, i % nt)),
           pl.BlockSpec((1, B_HEADS, tm, QK_PAD), lambda i: (i // nt, 0, i % nt, 0)),
           pl.BlockSpec((1, B_HEADS, nc, V_ROWS, tk), lambda i: (i // nt, 0, i % nt, 0, 0))],
        out_shape=[jax.ShapeDtypeStruct((QVT_ROWS, t), BF16)]
        + [jax.ShapeDtypeStruct((t, w), BF16) for w in widths]
        + [jax.ShapeDtypeStruct((batch, B_HEADS, QK_PAD, seq), BF16),
           jax.ShapeDtypeStruct((batch, B_HEADS, seq, QK_PAD), BF16),
           jax.ShapeDtypeStruct((batch, B_HEADS, nt * nc, V_ROWS, tk), BF16)],
        compiler_params=pltpu.CompilerParams(
            dimension_semantics=("arbitrary",), vmem_limit_bytes=VMEM_LIMIT),
        name="proj",
    )(x2, pos_row, *consts)


def _band_bias_kernel(rbt_ref, onehot_ref, o_ref):
    rbt = rbt_ref[...]
    onehot = onehot_ref[...]
    t = jnp.zeros((A_HEADS, WINDOW), F32)
    for b in range(N_BUCKETS):
        t = t + rbt[:, b:b + 1] * onehot[b:b + 1, :]
    t = t * math.log2(math.e)
    band = 2 * A_BLOCK
    key = lax.broadcasted_iota(jnp.int32, (band, band), 0)
    neg = jnp.full((1, A_BLOCK), NEG, F32)
    for hd in range(A_HEADS):
        x = jnp.broadcast_to(jnp.concatenate([neg, t[hd:hd + 1, :]], axis=1), (band, band))
        for bit in range(8):
            x = jnp.where((key >> bit) & 1 == 1, pltpu.roll(x, 1 << bit, 1), x)
        g, hh = divmod(hd, A_GROUP)
        o_ref[g, :, hh * A_BLOCK:(hh + 1) * A_BLOCK] = x[:, :A_BLOCK]


def _band_bias(rel_bias):
    dist = np.arange(WINDOW)
    max_exact = N_BUCKETS // 2
    large = max_exact + (np.log(np.maximum(dist, 1).astype(np.float32) / max_exact)
                         / math.log(MAX_DISTANCE / max_exact) * (N_BUCKETS - max_exact)).astype(np.int32)
    bucket = np.where(dist < max_exact, dist, np.minimum(large, N_BUCKETS - 1))
    onehot = jnp.asarray(np.arange(N_BUCKETS)[:, None] == bucket[None, :], F32)
    return pl.pallas_call(
        _band_bias_kernel,
        out_shape=jax.ShapeDtypeStruct((A_KV_HEADS, 2 * A_BLOCK, A_GROUP * A_BLOCK), F32),
        name="band_bias",
    )(rel_bias.astype(F32).T, onehot)


def _swa_kernel(qv_ref, k_ref, kp_ref, vp_ref, bias_ref, sink_ref, o_ref, *, nsub):
    first = pl.program_id(1) == 0
    band = 2 * A_BLOCK
    width = A_GROUP * A_BLOCK
    pad = jnp.logical_and(first, lax.broadcasted_iota(jnp.int32, (band, width), 0) < A_BLOCK)
    zeros_q = jnp.zeros((A_HEAD_DIM, width), BF16)
    ones_rows = (lax.broadcasted_iota(jnp.int32, (SWA_V_ROWS - A_HEAD_DIM, band), 0) == 0).astype(BF16)

    def scores(sb, g):
        r0 = sb * A_BLOCK
        if sb == 0:
            kb = jnp.concatenate([kp_ref[0], k_ref[0, 0:A_BLOCK, :]], axis=0)
        else:
            kb = k_ref[0, r0 - A_BLOCK:r0 + A_BLOCK, :]
        tiles = [qv_ref[(g * A_GROUP + hh) * A_HEAD_DIM:(g * A_GROUP + hh + 1) * A_HEAD_DIM, r0:r0 + A_BLOCK]
                 for hh in range(A_GROUP)]
        qt = jnp.concatenate(tiles, axis=1)
        rhs = jnp.concatenate([qt, zeros_q] if g == 0 else [zeros_q, qt], axis=0)
        s = _dot(kb, rhs) + bias_ref[g]
        if sb == 0:
            s = jnp.where(pad, NEG, s)
        return s

    def finish(sb, g, s):
        r0 = sb * A_BLOCK
        sink = sink_ref[g]
        m = jnp.maximum(jnp.max(s, axis=0, keepdims=True), sink)
        e = jnp.exp2(s - m).astype(BF16)
        v0 = A_WIDTH + g * A_HEAD_DIM
        if sb == 0:
            vt = jnp.concatenate([vp_ref[g * A_HEAD_DIM:(g + 1) * A_HEAD_DIM, :],
                                  qv_ref[v0:v0 + A_HEAD_DIM, 0:A_BLOCK]], axis=1)
        else:
            vt = qv_ref[v0:v0 + A_HEAD_DIM, r0 - A_BLOCK:r0 + A_BLOCK]
        acc = _dot(jnp.concatenate([vt, ones_rows], axis=0), e)
        den = acc[A_HEAD_DIM:A_HEAD_DIM + 1, :] + jnp.exp2(sink - m)
        o = (acc[:A_HEAD_DIM, :] / den).astype(BF16)
        for hh in range(A_GROUP):
            hd = g * A_GROUP + hh
            o_ref[hd * A_HEAD_DIM:(hd + 1) * A_HEAD_DIM, r0:r0 + A_BLOCK] = o[:, hh * A_BLOCK:(hh + 1) * A_BLOCK]

    units = [(sb, g) for sb in range(nsub) for g in range(A_KV_HEADS)]
    s_next = scores(*units[0])
    for n, unit in enumerate(units):
        s_cur = s_next
        if n + 1 < len(units):
            s_next = scores(*units[n + 1])
        finish(*unit, s_cur)


def _swa(qvt, ka, bias, sinks, batch, seq, tq):
    nsub = tq // A_BLOCK
    nt = seq // tq
    vrow = A_WIDTH // A_KV_WIDTH
    return pl.pallas_call(
        functools.partial(_swa_kernel, nsub=nsub),
        grid=(batch, nt),
        in_specs=[pl.BlockSpec((QVT_ROWS, tq), lambda b, i: (0, b * nt + i)),
                  pl.BlockSpec((1, tq, A_KV_WIDTH), lambda b, i: (b, i, 0)),
                  pl.BlockSpec((1, A_BLOCK, A_KV_WIDTH), lambda b, i: (b, jnp.maximum(i * nsub - 1, 0), 0)),
                  pl.BlockSpec((A_KV_WIDTH, A_BLOCK),
                               lambda b, i: (vrow, jnp.maximum((b * nt + i) * nsub - 1, 0))),
                  pl.BlockSpec(bias.shape, lambda b, i: (0, 0, 0)),
                  pl.BlockSpec(sinks.shape, lambda b, i: (0, 0, 0))],
        out_specs=pl.BlockSpec((A_WIDTH, tq), lambda b, i: (0, b * nt + i)),
        out_shape=jax.ShapeDtypeStruct((A_WIDTH, batch * seq), BF16),
        compiler_params=pltpu.CompilerParams(
            dimension_semantics=("arbitrary", "arbitrary"), vmem_limit_bytes=VMEM_LIMIT),
        name="swa",
    )(qvt, ka, ka, qvt, bias, sinks)


def _flash_kernel(qt_ref, k_ref, vt_ref, o_ref, m_sc, acc_sc, s0_sc, s1_sc, bm0_sc, bm1_sc, *, tq, nh):
    tk = tq // 2
    i = pl.program_id(2)
    m_sc[...] = jnp.full(m_sc.shape, NEG, F32)
    acc_sc[...] = jnp.zeros(acc_sc.shape, F32)
    s_bufs = (s0_sc, s1_sc)
    bm_bufs = (bm0_sc, bm1_sc)

    def scores(h, j, slot, diag_half):
        kb = k_ref[0, h, pl.ds(pl.multiple_of(j * tk, tk), tk), :]
        s = _dot(kb, qt_ref[0, h])
        if diag_half is not None:
            key = lax.broadcasted_iota(jnp.int32, (tk, tq), 0) + diag_half * tk
            qry = lax.broadcasted_iota(jnp.int32, (tk, tq), 1)
            s = jnp.where(key <= qry, s, NEG)
        s_bufs[slot][h] = s
        bm_bufs[slot][h] = jnp.max(s, axis=0, keepdims=True)

    def softmax_pv(h, j, slot):
        m_old = m_sc[h]
        m_new = jnp.maximum(m_old, bm_bufs[slot][h])
        p = jnp.exp2(s_bufs[slot][h] - m_new).astype(BF16)
        alpha = jnp.exp2(m_old - m_new)
        m_sc[h] = m_new
        acc_sc[h] = acc_sc[h] * alpha + _dot(vt_ref[0, h, j], p)

    def stage(j, slot, next_diag_half, has_next=True):
        for h in range(nh):
            if has_next:
                scores(h, j + 1, 1 - slot, next_diag_half)
            softmax_pv(h, j, slot)

    @pl.when(i == 0)
    def _():
        for h in range(nh):
            scores(h, 0, 0, 0)

    @pl.when(i > 0)
    def _():
        for h in range(nh):
            scores(h, 0, 0, None)

    def pair(jj, carry):
        stage(2 * jj, 0, None)
        stage(2 * jj + 1, 1, None)
        return carry

    lax.fori_loop(0, i - 1, pair, 0)

    @pl.when(i > 0)
    def _():
        stage(2 * i - 2, 0, None)
        stage(2 * i - 1, 1, 0)

    stage(2 * i, 0, 1)
    stage(2 * i + 1, 1, None, has_next=False)

    outs = []
    for h in range(nh):
        acc = acc_sc[h]
        outs.append(acc[:B_VDIM, :] / acc[B_VDIM:B_VDIM + 1, :])
    o_ref[0] = jnp.concatenate(outs, axis=0).T.astype(BF16)


def _flash(qt, k, vt, tq, nh):
    batch, heads, _, seq = qt.shape
    tk = tq // 2
    nq = seq // tq
    return pl.pallas_call(
        functools.partial(_flash_kernel, tq=tq, nh=nh),
        grid=(batch, heads // nh, nq),
        in_specs=[pl.BlockSpec((1, nh, QK_PAD, tq), lambda b, g, i: (b, g, 0, i)),
                  pl.BlockSpec((1, nh, seq, QK_PAD), lambda b, g, i: (b, g, 0, 0)),
                  pl.BlockSpec((1, nh, seq // tk, V_ROWS, tk), lambda b, g, i: (b, g, 0, 0, 0))],
        out_specs=pl.BlockSpec((1, tq, nh * B_VDIM), lambda b, g, i: (b, i, g)),
        out_shape=jax.ShapeDtypeStruct((batch, seq, B_WIDTH), BF16),
        scratch_shapes=[pltpu.VMEM((nh, 1, tq), F32), pltpu.VMEM((nh, V_ROWS, tq), F32),
                        pltpu.VMEM((nh, tk, tq), F32), pltpu.VMEM((nh, tk, tq), F32),
                        pltpu.VMEM((nh, 1, tq), F32), pltpu.VMEM((nh, 1, tq), F32)],
        compiler_params=pltpu.CompilerParams(
            dimension_semantics=("arbitrary", "arbitrary", "arbitrary"),
            vmem_limit_bytes=VMEM_LIMIT),
        name="flash",
    )(qt, k, vt)


def _flash_bounded_kernel(kmax_ref, qt_ref, qtn_ref, k_ref, vt_ref, o_ref, mq_sc, mqn_sc, acc_sc, p0_sc, p1_sc,
                          *, tq, nh):
    tk = tq // 2
    i = pl.program_id(2)
    acc_sc[...] = jnp.zeros(acc_sc.shape, F32)

    def shift(q_ref, h):
        q = q_ref[0, h].astype(F32)
        return jnp.sqrt(jnp.sum(q * q, axis=0, keepdims=True)) * kmax_ref[0]

    for h in range(nh):
        mq_sc[h] = shift(qt_ref, h)
        mqn_sc[h] = shift(qtn_ref, h)
    p_bufs = (p0_sc, p1_sc)

    def probs(h, j, slot, diag_half, next_q=False):
        kb = k_ref[0, h, pl.ds(pl.multiple_of(j * tk, tk), tk), :]
        s = _dot(kb, (qtn_ref if next_q else qt_ref)[0, h])
        if diag_half is not None:
            key = lax.broadcasted_iota(jnp.int32, (tk, tq), 0) + diag_half * tk
            qry = lax.broadcasted_iota(jnp.int32, (tk, tq), 1)
            s = jnp.where(key <= qry, s, NEG)
        p_bufs[slot][h] = jnp.exp2(s - (mqn_sc if next_q else mq_sc)[h]).astype(BF16)

    def pv(h, j, slot):
        acc_sc[h] += _dot(vt_ref[0, h, j], p_bufs[slot][h])

    def stage(j, slot, next_diag_half, next_q=False):
        for h in range(nh):
            probs(h, 0 if next_q else j + 1, 1 - slot, next_diag_half, next_q)
            pv(h, j, slot)

    def pair(jj, last_next_diag_half=None):
        stage(2 * jj, 0, None)
        stage(2 * jj + 1, 1, last_next_diag_half)

    @pl.when(i == 0)
    def _():
        for h in range(nh):
            probs(h, 0, 0, 0)

    plain = i - 1

    def two_pairs(n, carry):
        pair(2 * n)
        pair(2 * n + 1)
        return carry

    lax.fori_loop(0, plain // 2, two_pairs, 0)

    @pl.when(jnp.logical_and(plain > 0, plain % 2 == 1))
    def _():
        pair(plain - 1)

    @pl.when(i > 0)
    def _():
        pair(i - 1, 0)

    stage(2 * i, 0, 1)
    stage(2 * i + 1, 1, None, next_q=True)

    outs = []
    for h in range(nh):
        acc = acc_sc[h]
        outs.append(acc[:B_VDIM, :] / acc[B_VDIM:B_VDIM + 1, :])
    o_ref[0] = jnp.concatenate(outs, axis=0).T.astype(BF16)


def _flash_bounded(kmax, qt, k, vt, tq, nh):
    batch, heads, _, seq = qt.shape
    tk = tq // 2
    nq = seq // tq
    return pl.pallas_call(
        functools.partial(_flash_bounded_kernel, tq=tq, nh=nh),
        grid=(batch, heads // nh, nq),
        in_specs=[pl.BlockSpec(memory_space=pltpu.SMEM),
                  pl.BlockSpec((1, nh, QK_PAD, tq), lambda b, g, i: (b, g, 0, i)),
                  pl.BlockSpec((1, nh, QK_PAD, tq), lambda b, g, i: (b, g, 0, jnp.minimum(i + 1, nq - 1))),
                  pl.BlockSpec((1, nh, seq, QK_PAD), lambda b, g, i: (b, g, 0, 0)),
                  pl.BlockSpec((1, nh, seq // tk, V_ROWS, tk), lambda b, g, i: (b, g, 0, 0, 0))],
        out_specs=pl.BlockSpec((1, tq, nh * B_VDIM), lambda b, g, i: (b, i, g)),
        out_shape=jax.ShapeDtypeStruct((batch, seq, B_WIDTH), BF16),
        scratch_shapes=[pltpu.VMEM((nh, 1, tq), F32), pltpu.VMEM((nh, 1, tq), F32),
                        pltpu.VMEM((nh, V_ROWS, tq), F32),
                        pltpu.VMEM((nh, tk, tq), BF16), pltpu.VMEM((nh, tk, tq), BF16)],
        compiler_params=pltpu.CompilerParams(
            dimension_semantics=("arbitrary", "arbitrary", "arbitrary"),
            vmem_limit_bytes=VMEM_LIMIT),
        name="flash_bounded",
    )(kmax, qt, qt, k, vt)


def _final_kernel(x_ref, p_ref, oat_ref, za_ref, ob_ref, zb_ref, ga_ref, gb_ref,
                  woa_ref, wob_ref, wout_ref, gpl_ref, wpg_ref, wpp_ref, gpost_ref, out_ref):
    rows = x_ref.shape[0] // FINAL_CHAINS
    groups = [slice(c * rows, (c + 1) * rows) for c in range(FINAL_CHAINS)]
    oa = [oat_ref[:, r].astype(F32).T for r in groups]
    ya = [_dot((oa[c] * jax.nn.silu(za_ref[r, :].astype(F32))).astype(BF16), woa_ref[...])
          for c, r in enumerate(groups)]
    yb = [_dot((ob_ref[r, :].astype(F32) * jax.nn.silu(zb_ref[r, :].astype(F32))).astype(BF16), wob_ref[...])
          for r in groups]
    emb = [_row_rms(_dot(p_ref[r, :].astype(BF16), wpp_ref[...]), D_MODEL) * gpost_ref[...] for r in groups]
    merged = [jax.nn.sigmoid(ga_ref[r, :].astype(F32)) * ya[c] + jax.nn.sigmoid(gb_ref[r, :].astype(F32)) * yb[c]
              for c, r in enumerate(groups)]
    x1 = [x_ref[r, :] + _dot(merged[c].astype(BF16), wout_ref[...]) for c, r in enumerate(groups)]
    gate = [jax.nn.sigmoid(_dot((_row_rms(x1[c], D_MODEL) * gpl_ref[...]).astype(BF16), wpg_ref[...]))
            for c in range(FINAL_CHAINS)]
    for c, r in enumerate(groups):
        out_ref[r, :] = x1[c] + gate[c] * emb[c]


def _final(x2, p2, oat, za, ob, zb, ga, gb, woa, wob, wout, gpl, wpg, wpp, gpost, tm):
    t = x2.shape[0]
    row = lambda a: pl.BlockSpec((tm, a.shape[1]), lambda i: (i, 0))
    full = lambda a: pl.BlockSpec(a.shape, lambda i: (0,) * a.ndim)
    acts = (x2, p2, oat, za, ob, zb, ga, gb)
    consts = (woa, wob, wout, gpl, wpg, wpp, gpost)
    act_specs = [row(a) for a in acts]
    act_specs[2] = pl.BlockSpec((A_WIDTH, tm), lambda i: (0, i))
    return pl.pallas_call(
        _final_kernel,
        grid=(t // tm,),
        in_specs=act_specs + [full(a) for a in consts],
        out_specs=pl.BlockSpec((tm, D_MODEL), lambda i: (i, 0)),
        out_shape=jax.ShapeDtypeStruct((t, D_MODEL), F32),
        compiler_params=pltpu.CompilerParams(
            dimension_semantics=("arbitrary",), vmem_limit_bytes=VMEM_LIMIT),
        name="final",
    )(*acts, *consts)


def kernel(x, p, positions, norm_g, w_in, a_q_norm, a_k_norm, a_sinks, rel_bias, w_o_a, b_cq_norm, w_uq, b_ckv_norm, w_uk, w_uv, b_q_norm, b_k_norm, b_kr_norm, w_o_b, w_out, ple_norm_g, w_ple_gate, w_ple_proj, ple_post_g):
    batch, seq, _ = x.shape
    depth = p.shape[0]
    t = batch * seq
    tm = TOKEN_TILE
    offs = np.concatenate([[0], np.cumsum(SPLIT_SIZES)])
    log2e = math.log2(math.e)
    inv_freq = (ROPE_THETA ** (-jnp.arange(0, B_ROPE, 2, dtype=F32) / B_ROPE))[:, None]
    pos_row = positions.reshape(1, t)
    band_bias = _band_bias(rel_bias)
    row = lambda v: v.astype(F32)[None, :]
    col = lambda v: v.astype(F32)[:, None]

    x2 = x.reshape(t, D_MODEL)
    for i in range(depth):
        w = w_in[i]
        cols = [w[:, offs[j]:offs[j + 1]] for j in range(len(SPLIT_SIZES))]
        wt = jnp.concatenate([cols[0], cols[2], cols[1], cols[6]], axis=1).T.astype(BF16)
        w1 = jnp.concatenate([cols[j] for j in (3, 4, 5, 7, 8, 9)], axis=1).astype(BF16)
        gqn = col(b_q_norm[i]) * (B_QK_DIM ** -0.5 * log2e)
        consts = (row(norm_g[i]), wt, w1,
                  col(a_q_norm[i]) * (A_HEAD_DIM ** -0.5 * log2e), col(a_k_norm[i]),
                  row(b_cq_norm[i]), w_uq[i].T.astype(BF16), gqn,
                  row(b_ckv_norm[i]), w_uk[i].T.astype(BF16), w_uv[i].T.astype(BF16),
                  col(b_k_norm[i]), col(b_kr_norm[i]), inv_freq)
        qvt, ka, za, zb, ga, gb, qt, kb, vt = _proj(x2, pos_row, consts, batch, seq, tm, FLASH_Q_TILE // 2)

        sinks = jnp.repeat(a_sinks[i].astype(F32) * log2e, A_BLOCK).reshape(A_KV_HEADS, 1, A_GROUP * A_BLOCK)
        oat = _swa(qvt, ka.reshape(batch, seq, A_KV_WIDTH), band_bias, sinks, batch, seq, tm)
        q_bound = math.sqrt(B_QK_DIM) * jnp.max(jnp.abs(gqn))
        k_bound = jnp.sqrt(B_NOPE * jnp.max(jnp.square(b_k_norm[i].astype(F32)))
                           + B_ROPE * jnp.max(jnp.square(b_kr_norm[i].astype(F32))))
        ob = lax.cond(q_bound * k_bound < MAX_SAFE_SHIFT,
                      lambda: _flash_bounded(k_bound.reshape(1), qt, kb, vt, FLASH_Q_TILE, FLASH_HEADS),
                      lambda: _flash(qt, kb, vt, FLASH_Q_TILE, FLASH_HEADS))

        x2 = _final(x2, p[i].reshape(t, PLE_DIM), oat, za,
                    ob.reshape(t, B_WIDTH), zb, ga, gb,
                    w_o_a[i].astype(BF16), w_o_b[i].astype(BF16), w_out[i].astype(BF16),
                    row(ple_norm_g[i]), w_ple_gate[i].astype(BF16), w_ple_proj[i].astype(BF16),
                    row(ple_post_g[i]), tm)
    return x2.reshape(batch, seq, D_MODEL)
```

```python
import functools
import math

import numpy as np
import jax
import jax.numpy as jnp
from jax import lax
from jax.experimental import pallas as pl
from jax.experimental.pallas import tpu as pltpu

F32 = jnp.float32
BF16 = jnp.bfloat16

D_MODEL = 1024
PLE_DIM = 256
EPS = 1e-6
NEG = -1e30

A_HEADS = 16
A_KV_HEADS = 2
A_HEAD_DIM = 64
A_WIDTH = A_HEADS * A_HEAD_DIM
A_KV_WIDTH = A_KV_HEADS * A_HEAD_DIM
A_GROUP = A_HEADS // A_KV_HEADS
WINDOW = 128
A_BLOCK = 128
N_BUCKETS = 32
MAX_DISTANCE = 128

B_HEADS = 16
B_Q_RANK = 256
B_KV_RANK = 128
B_NOPE = 64
B_ROPE = 32
B_QK_DIM = B_NOPE + B_ROPE
B_VDIM = 64
B_WIDTH = B_HEADS * B_VDIM
ROPE_THETA = 10000.0

SPLIT_SIZES = (A_WIDTH, A_KV_WIDTH, A_KV_WIDTH, A_WIDTH,
               B_Q_RANK, B_KV_RANK, B_ROPE, B_WIDTH,
               D_MODEL, D_MODEL)

LANES = 128
QK_PAD = 128
V_ROWS = 80
VMEM_LIMIT = 56 * 1024 * 1024
MAX_SAFE_SHIFT = 60.0

TOKEN_TILE = 512
FLASH_Q_TILE = 512
FLASH_HEADS = 4
FINAL_CHAINS = 4

QVT_ROWS = A_WIDTH + A_KV_WIDTH
R_KA = QVT_ROWS
R_KR = R_KA + A_KV_WIDTH
WT_ROWS = R_KR + B_ROPE
SWA_V_ROWS = A_HEAD_DIM + 16
C_ZA = 0
C_LAT = C_ZA + A_WIDTH
LAT_W = B_Q_RANK + B_KV_RANK
C_ZB = C_LAT + LAT_W
C_GA = C_ZB + B_WIDTH
C_GB = C_GA + D_MODEL
C_END = C_GB + D_MODEL


def _dot(a, b):
    return jnp.dot(a, b, preferred_element_type=F32)


def _dot_nt(a, b):
    return lax.dot_general(a, b, (((1,), (1,)), ((), ())), preferred_element_type=F32)


def _row_rms(v, width):
    return v * lax.rsqrt(jnp.sum(v * v, axis=-1, keepdims=True) * (1.0 / width) + EPS)


def _col_rms(v):
    return v * lax.rsqrt(jnp.sum(v * v, axis=0, keepdims=True) * (1.0 / v.shape[0]) + EPS)


def _proj_kernel(x_ref, pos_ref, g_ref, wt_ref, w_ref, gq_ref, gk_ref,
                 gcq_ref, wuqt_ref, gqn_ref, gckv_ref, wukt_ref, wuvt_ref, gkn_ref, gkr_ref, freq_ref,
                 qvt_ref, ka_ref, za_ref, zb_ref, ga_ref, gb_ref, qt_ref, k_ref, vt_ref):
    x = x_ref[...]
    tm = x.shape[0]
    h = (_row_rms(x, D_MODEL) * g_ref[...]).astype(BF16)

    wt_out = _dot_nt(wt_ref[...], h)
    gq = gq_ref[...]
    for hd in range(A_HEADS):
        blk = wt_out[hd * A_HEAD_DIM:(hd + 1) * A_HEAD_DIM, :]
        qvt_ref[hd * A_HEAD_DIM:(hd + 1) * A_HEAD_DIM, :] = (_col_rms(blk) * gq).astype(BF16)
    qvt_ref[A_WIDTH:, :] = wt_out[A_WIDTH:QVT_ROWS, :].astype(BF16)
    gk = gk_ref[...]
    kat = [_col_rms(wt_out[R_KA + g * A_HEAD_DIM:R_KA + (g + 1) * A_HEAD_DIM, :]) * gk
           for g in range(A_KV_HEADS)]
    ka_ref[...] = jnp.concatenate(kat, axis=0).T.astype(BF16)
    krt = wt_out[R_KR:, :]

    def proj(c0, c1):
        return _dot(h, w_ref[:, c0:c1])

    za_ref[...] = proj(C_ZA, C_LAT).astype(BF16)
    lat = proj(C_LAT, C_ZB)
    zb_ref[...] = proj(C_ZB, C_GA).astype(BF16)
    ga_ref[...] = proj(C_GA, C_GB).astype(BF16)
    gb_ref[...] = proj(C_GB, C_END).astype(BF16)

    cq = (_row_rms(lat[:, :B_Q_RANK], B_Q_RANK) * gcq_ref[...]).astype(BF16)
    qt = _dot_nt(wuqt_ref[...], cq)
    ang = freq_ref[...] * pos_ref[...].astype(F32)
    cos = jnp.cos(ang)
    sin = jnp.sin(ang)
    gqn = gqn_ref[...]
    half = B_ROPE // 2
    zeros_q = jnp.zeros((QK_PAD - B_QK_DIM, tm), BF16)
    for hd in range(B_HEADS):
        qn = _col_rms(qt[hd * B_QK_DIM:(hd + 1) * B_QK_DIM, :]) * gqn
        x1 = qn[B_NOPE:B_NOPE + half, :]
        x2 = qn[B_NOPE + half:, :]
        qt_ref[0, hd, 0:B_NOPE, :] = qn[:B_NOPE, :].astype(BF16)
        qt_ref[0, hd, B_NOPE:B_NOPE + half, :] = (x1 * cos - x2 * sin).astype(BF16)
        qt_ref[0, hd, B_NOPE + half:B_QK_DIM, :] = (x2 * cos + x1 * sin).astype(BF16)
        qt_ref[0, hd, B_QK_DIM:, :] = zeros_q

    ckv = (_row_rms(lat[:, B_Q_RANK:], B_KV_RANK) * gckv_ref[...]).astype(BF16)
    knt = _dot_nt(wukt_ref[...], ckv)
    vt = _dot_nt(wuvt_ref[...], ckv)

    krn = _col_rms(krt) * gkr_ref[...]
    x1 = krn[:half, :]
    x2 = krn[half:, :]
    k_tail = jnp.concatenate([x1 * cos - x2 * sin, x2 * cos + x1 * sin,
                              jnp.zeros((QK_PAD - B_QK_DIM, tm), F32)], axis=0)
    gkn = gkn_ref[...]
    for hd in range(B_HEADS):
        kn = _col_rms(knt[hd * B_NOPE:(hd + 1) * B_NOPE, :]) * gkn
        k_ref[0, hd] = jnp.concatenate([kn, k_tail], axis=0).T.astype(BF16)
    tk = vt_ref.shape[-1]
    ones_rows = (lax.broadcasted_iota(jnp.int32, (V_ROWS - B_VDIM, tk), 0) == 0).astype(BF16)
    for hd in range(B_HEADS):
        for c in range(tm // tk):
            vt_ref[0, hd, c, 0:B_VDIM, :] = vt[hd * B_VDIM:(hd + 1) * B_VDIM, c * tk:(c + 1) * tk].astype(BF16)
            vt_ref[0, hd, c, B_VDIM:, :] = ones_rows


def _proj(x2, pos_row, consts, batch, seq, tm, tk):
    t = x2.shape[0]
    nt = seq // tm
    nc = tm // tk
    row = lambda w: pl.BlockSpec((tm, w), lambda i: (i, 0))
    full = lambda a: pl.BlockSpec(a.shape, lambda i: (0,) * a.ndim)
    widths = (A_KV_WIDTH, A_WIDTH, B_WIDTH, D_MODEL, D_MODEL)
    return pl.pallas_call(
        _proj_kernel,
        grid=(t // tm,),
        in_specs=[row(D_MODEL), pl.BlockSpec((1, tm), lambda i: (0, i))] + [full(a) for a in consts],
        out_specs=[pl.BlockSpec((QVT_ROWS, tm), lambda i: (0, i))] + [row(w) for w in widths]
        + [pl.BlockSpec((1, B_HEADS, QK_PAD, tm), lambda i: (i // nt, 0, 0, i % nt)),
           pl.BlockSpec((1, B_HEADS, tm, QK_PAD), lambda i: (i // nt, 0, i % nt, 0)),
           pl.BlockSpec((1, B_HEADS, nc, V_ROWS, tk), lambda i: (i // nt, 0, i % nt, 0, 0))],
        out_shape=[jax.ShapeDtypeStruct((QVT_ROWS, t), BF16)]
        + [jax.ShapeDtypeStruct((t, w), BF16) for w in widths]
        + [jax.ShapeDtypeStruct((batch, B_HEADS, QK_PAD, seq), BF16),
           jax.ShapeDtypeStruct((batch, B_HEADS, seq, QK_PAD), BF16),
           jax.ShapeDtypeStruct((batch, B_HEADS, nt * nc, V_ROWS, tk), BF16)],
        compiler_params=pltpu.CompilerParams(
            dimension_semantics=("arbitrary",), vmem_limit_bytes=VMEM_LIMIT),
        name="proj",
    )(x2, pos_row, *consts)


def _band_bias_kernel(rbt_ref, onehot_ref, o_ref):
    rbt = rbt_ref[...]
    onehot = onehot_ref[...]
    t = jnp.zeros((A_HEADS, WINDOW), F32)
    for b in range(N_BUCKETS):
        t = t + rbt[:, b:b + 1] * onehot[b:b + 1, :]
    t = t * math.log2(math.e)
    band = 2 * A_BLOCK
    key = lax.broadcasted_iota(jnp.int32, (band, band), 0)
    neg = jnp.full((1, A_BLOCK), NEG, F32)
    for hd in range(A_HEADS):
        x = jnp.broadcast_to(jnp.concatenate([neg, t[hd:hd + 1, :]], axis=1), (band, band))
        for bit in range(8):
            x = jnp.where((key >> bit) & 1 == 1, pltpu.roll(x, 1 << bit, 1), x)
        g, hh = divmod(hd, A_GROUP)
        o_ref[g, :, hh * A_BLOCK:(hh + 1) * A_BLOCK] = x[:, :A_BLOCK]


def _band_bias(rel_bias):
    dist = np.arange(WINDOW)
    max_exact = N_BUCKETS // 2
    large = max_exact + (np.log(np.maximum(dist, 1).astype(np.float32) / max_exact)
                         / math.log(MAX_DISTANCE / max_exact) * (N_BUCKETS - max_exact)).astype(np.int32)
    bucket = np.where(dist < max_exact, dist, np.minimum(large, N_BUCKETS - 1))
    onehot = jnp.asarray(np.arange(N_BUCKETS)[:, None] == bucket[None, :], F32)
    return pl.pallas_call(
        _band_bias_kernel,
        out_shape=jax.ShapeDtypeStruct((A_KV_HEADS, 2 * A_BLOCK, A_GROUP * A_BLOCK), F32),
        name="band_bias",
    )(rel_bias.astype(F32).T, onehot)


def _swa_kernel(qv_ref, k_ref, kp_ref, vp_ref, bias_ref, sink_ref, o_ref, *, nsub):
    first = pl.program_id(1) == 0
    band = 2 * A_BLOCK
    width = A_GROUP * A_BLOCK
    pad = jnp.logical_and(first, lax.broadcasted_iota(jnp.int32, (band, width), 0) < A_BLOCK)
    zeros_q = jnp.zeros((A_HEAD_DIM, width), BF16)
    ones_rows = (lax.broadcasted_iota(jnp.int32, (SWA_V_ROWS - A_HEAD_DIM, band), 0) == 0).astype(BF16)

    def scores(sb, g):
        r0 = sb * A_BLOCK
        if sb == 0:
            kb = jnp.concatenate([kp_ref[0], k_ref[0, 0:A_BLOCK, :]], axis=0)
        else:
            kb = k_ref[0, r0 - A_BLOCK:r0 + A_BLOCK, :]
        tiles = [qv_ref[(g * A_GROUP + hh) * A_HEAD_DIM:(g * A_GROUP + hh + 1) * A_HEAD_DIM, r0:r0 + A_BLOCK]
                 for hh in range(A_GROUP)]
        qt = jnp.concatenate(tiles, axis=1)
        rhs = jnp.concatenate([qt, zeros_q] if g == 0 else [zeros_q, qt], axis=0)
        s = _dot(kb, rhs) + bias_ref[g]
        if sb == 0:
            s = jnp.where(pad, NEG, s)
        return s

    def finish(sb, g, s):
        r0 = sb * A_BLOCK
        sink = sink_ref[g]
        m = jnp.maximum(jnp.max(s, axis=0, keepdims=True), sink)
        e = jnp.exp2(s - m).astype(BF16)
        v0 = A_WIDTH + g * A_HEAD_DIM
        if sb == 0:
            vt = jnp.concatenate([vp_ref[g * A_HEAD_DIM:(g + 1) * A_HEAD_DIM, :],
                                  qv_ref[v0:v0 + A_HEAD_DIM, 0:A_BLOCK]], axis=1)
        else:
            vt = qv_ref[v0:v0 + A_HEAD_DIM, r0 - A_BLOCK:r0 + A_BLOCK]
        acc = _dot(jnp.concatenate([vt, ones_rows], axis=0), e)
        den = acc[A_HEAD_DIM:A_HEAD_DIM + 1, :] + jnp.exp2(sink - m)
        o = (acc[:A_HEAD_DIM, :] / den).astype(BF16)
        for hh in range(A_GROUP):
            hd = g * A_GROUP + hh
            o_ref[hd * A_HEAD_DIM:(hd + 1) * A_HEAD_DIM, r0:r0 + A_BLOCK] = o[:, hh * A_BLOCK:(hh + 1) * A_BLOCK]

    units = [(sb, g) for sb in range(nsub) for g in range(A_KV_HEADS)]
    s_next = scores(*units[0])
    for n, unit in enumerate(units):
        s_cur = s_next
        if n + 1 < len(units):
            s_next = scores(*units[n + 1])
        finish(*unit, s_cur)


def _swa(qvt, ka, bias, sinks, batch, seq, tq):
    nsub = tq // A_BLOCK
    nt = seq // tq
    vrow = A_WIDTH // A_KV_WIDTH
    return pl.pallas_call(
        functools.partial(_swa_kernel, nsub=nsub),
        grid=(batch, nt),
        in_specs=[pl.BlockSpec((QVT_ROWS, tq), lambda b, i: (0, b * nt + i)),
                  pl.BlockSpec((1, tq, A_KV_WIDTH), lambda b, i: (b, i, 0)),
                  pl.BlockSpec((1, A_BLOCK, A_KV_WIDTH), lambda b, i: (b, jnp.maximum(i * nsub - 1, 0), 0)),
                  pl.BlockSpec((A_KV_WIDTH, A_BLOCK),
                               lambda b, i: (vrow, jnp.maximum((b * nt + i) * nsub - 1, 0))),
                  pl.BlockSpec(bias.shape, lambda b, i: (0, 0, 0)),
                  pl.BlockSpec(sinks.shape, lambda b, i: (0, 0, 0))],
        out_specs=pl.BlockSpec((A_WIDTH, tq), lambda b, i: (0, b * nt + i)),
        out_shape=jax.ShapeDtypeStruct((A_WIDTH, batch * seq), BF16),
        compiler_params=pltpu.CompilerParams(
            dimension_semantics=("arbitrary", "arbitrary"), vmem_limit_bytes=VMEM_LIMIT),
        name="swa",
    )(qvt, ka, ka, qvt, bias, sinks)


def _flash_kernel(qt_ref, k_ref, vt_ref, o_ref, m_sc, acc_sc, s0_sc, s1_sc, bm0_sc, bm1_sc, *, tq, nh):
    tk = tq // 2
    i = pl.program_id(2)
    m_sc[...] = jnp.full(m_sc.shape, NEG, F32)
    acc_sc[...] = jnp.zeros(acc_sc.shape, F32)
    s_bufs = (s0_sc, s1_sc)
    bm_bufs = (bm0_sc, bm1_sc)

    def scores(h, j, slot, diag_half):
        kb = k_ref[0, h, pl.ds(pl.multiple_of(j * tk, tk), tk), :]
        s = _dot(kb, qt_ref[0, h])
        if diag_half is not None:
            key = lax.broadcasted_iota(jnp.int32, (tk, tq), 0) + diag_half * tk
            qry = lax.broadcasted_iota(jnp.int32, (tk, tq), 1)
            s = jnp.where(key <= qry, s, NEG)
        s_bufs[slot][h] = s
        bm_bufs[slot][h] = jnp.max(s, axis=0, keepdims=True)

    def softmax_pv(h, j, slot):
        m_old = m_sc[h]
        m_new = jnp.maximum(m_old, bm_bufs[slot][h])
        p = jnp.exp2(s_bufs[slot][h] - m_new).astype(BF16)
        alpha = jnp.exp2(m_old - m_new)
        m_sc[h] = m_new
        acc_sc[h] = acc_sc[h] * alpha + _dot(vt_ref[0, h, j], p)

    def stage(j, slot, next_diag_half, has_next=True):
        for h in range(nh):
            if has_next:
                scores(h, j + 1, 1 - slot, next_diag_half)
            softmax_pv(h, j, slot)

    @pl.when(i == 0)
    def _():
        for h in range(nh):
            scores(h, 0, 0, 0)

    @pl.when(i > 0)
    def _():
        for h in range(nh):
            scores(h, 0, 0, None)

    def pair(jj, carry):
        stage(2 * jj, 0, None)
        stage(2 * jj + 1, 1, None)
        return carry

    lax.fori_loop(0, i - 1, pair, 0)

    @pl.when(i > 0)
    def _():
        stage(2 * i - 2, 0, None)
        stage(2 * i - 1, 1, 0)

    stage(2 * i, 0, 1)
    stage(2 * i + 1, 1, None, has_next=False)

    outs = []
    for h in range(nh):
        acc = acc_sc[h]
        outs.append(acc[:B_VDIM, :] / acc[B_VDIM:B_VDIM + 1, :])
    o_ref[0] = jnp.concatenate(outs, axis=0).T.astype(BF16)


def _flash(qt, k, vt, tq, nh):
    batch, heads, _, seq = qt.shape
    tk = tq // 2
    nq = seq // tq
    return pl.pallas_call(
        functools.partial(_flash_kernel, tq=tq, nh=nh),
        grid=(batch, heads // nh, nq),
        in_specs=[pl.BlockSpec((1, nh, QK_PAD, tq), lambda b, g, i: (b, g, 0, i)),
                  pl.BlockSpec((1, nh, seq, QK_PAD), lambda b, g, i: (b, g, 0, 0)),
                  pl.BlockSpec((1, nh, seq // tk, V_ROWS, tk), lambda b, g, i: (b, g, 0, 0, 0))],
        out_specs=pl.BlockSpec((1, tq, nh * B_VDIM), lambda b, g, i: (b, i, g)),
        out_shape=jax.ShapeDtypeStruct((batch, seq, B_WIDTH), BF16),
        scratch_shapes=[pltpu.VMEM((nh, 1, tq), F32), pltpu.VMEM((nh, V_ROWS, tq), F32),
                        pltpu.VMEM((nh, tk, tq), F32), pltpu.VMEM((nh, tk, tq), F32),
                        pltpu.VMEM((nh, 1, tq), F32), pltpu.VMEM((nh, 1, tq), F32)],
        compiler_params=pltpu.CompilerParams(
            dimension_semantics=("arbitrary", "arbitrary", "arbitrary"),
            vmem_limit_bytes=VMEM_LIMIT),
        name="flash",
    )(qt, k, vt)


def _flash_bounded_kernel(kmax_ref, qt_ref, qtn_ref, k_ref, vt_ref, o_ref, mq_sc, mqn_sc, acc_sc, p0_sc, p1_sc,
                          *, tq, nh):
    tk = tq // 2
    i = pl.program_id(2)
    acc_sc[...] = jnp.zeros(acc_sc.shape, F32)

    def shift(q_ref, h):
        q = q_ref[0, h].astype(F32)
        return jnp.sqrt(jnp.sum(q * q, axis=0, keepdims=True)) * kmax_ref[0]

    @pl.when(i > 0)
    def _():
        mq_sc[...] = mqn_sc[...]

    p_bufs = (p0_sc, p1_sc)

    def probs(h, j, slot, diag_half, next_q=False):
        kb = k_ref[0, h, pl.ds(pl.multiple_of(j * tk, tk), tk), :]
        s = _dot(kb, (qtn_ref if next_q else qt_ref)[0, h])
        if diag_half is not None:
            key = lax.broadcasted_iota(jnp.int32, (tk, tq), 0) + diag_half * tk
            qry = lax.broadcasted_iota(jnp.int32, (tk, tq), 1)
            s = jnp.where(key <= qry, s, NEG)
        p_bufs[slot][h] = jnp.exp2(s - (mqn_sc if next_q else mq_sc)[h]).astype(BF16)

    def pv(h, j, slot):
        acc_sc[h] += _dot(vt_ref[0, h, j], p_bufs[slot][h])

    def stage(j, slot, next_diag_half, next_q=False):
        for h in range(nh):
            probs(h, 0 if next_q else j + 1, 1 - slot, next_diag_half, next_q)
            pv(h, j, slot)

    def pair(jj, last_next_diag_half=None):
        stage(2 * jj, 0, None)
        stage(2 * jj + 1, 1, last_next_diag_half)

    @pl.when(i == 0)
    def _():
        for h in range(nh):
            mq_sc[h] = shift(qt_ref, h)
        for h in range(nh):
            probs(h, 0, 0, 0)

    plain = i - 1

    def two_pairs(n, carry):
        pair(2 * n)
        pair(2 * n + 1)
        return carry

    lax.fori_loop(0, plain // 2, two_pairs, 0)

    @pl.when(jnp.logical_and(plain > 0, plain % 2 == 1))
    def _():
        pair(plain - 1)

    @pl.when(i > 0)
    def _():
        pair(i - 1, 0)

    for h in range(nh):
        mqn_sc[h] = shift(qtn_ref, h)
    stage(2 * i, 0, 1)
    stage(2 * i + 1, 1, None, next_q=True)

    outs = []
    for h in range(nh):
        acc = acc_sc[h]
        outs.append(acc[:B_VDIM, :] / acc[B_VDIM:B_VDIM + 1, :])
    o_ref[0] = jnp.concatenate(outs, axis=0).T.astype(BF16)


def _flash_bounded(kmax, qt, k, vt, tq, nh):
    batch, heads, _, seq = qt.shape
    tk = tq // 2
    nq = seq // tq
    return pl.pallas_call(
        functools.partial(_flash_bounded_kernel, tq=tq, nh=nh),
        grid=(batch, heads // nh, nq),
        in_specs=[pl.BlockSpec(memory_space=pltpu.SMEM),
                  pl.BlockSpec((1, nh, QK_PAD, tq), lambda b, g, i: (b, g, 0, i)),
                  pl.BlockSpec((1, nh, QK_PAD, tq), lambda b, g, i: (b, g, 0, jnp.minimum(i + 1, nq - 1))),
                  pl.BlockSpec((1, nh, seq, QK_PAD), lambda b, g, i: (b, g, 0, 0)),
                  pl.BlockSpec((1, nh, seq // tk, V_ROWS, tk), lambda b, g, i: (b, g, 0, 0, 0))],
        out_specs=pl.BlockSpec((1, tq, nh * B_VDIM), lambda b, g, i: (b, i, g)),
        out_shape=jax.ShapeDtypeStruct((batch, seq, B_WIDTH), BF16),
        scratch_shapes=[pltpu.VMEM((nh, 1, tq), F32), pltpu.VMEM((nh, 1, tq), F32),
                        pltpu.VMEM((nh, V_ROWS, tq), F32),
                        pltpu.VMEM((nh, tk, tq), BF16), pltpu.VMEM((nh, tk, tq), BF16)],
        compiler_params=pltpu.CompilerParams(
            dimension_semantics=("arbitrary", "arbitrary", "arbitrary"),
            vmem_limit_bytes=VMEM_LIMIT),
        name="flash_bounded",
    )(kmax, qt, qt, k, vt)


def _final_kernel(x_ref, p_ref, oat_ref, za_ref, ob_ref, zb_ref, ga_ref, gb_ref,
                  woa_ref, wob_ref, wout_ref, gpl_ref, wpg_ref, wpp_ref, gpost_ref, out_ref):
    rows = x_ref.shape[0] // FINAL_CHAINS
    groups = [slice(c * rows, (c + 1) * rows) for c in range(FINAL_CHAINS)]
    oa = [oat_ref[:, r].astype(F32).T for r in groups]
    ya = [_dot((oa[c] * jax.nn.silu(za_ref[r, :].astype(F32))).astype(BF16), woa_ref[...])
          for c, r in enumerate(groups)]
    yb = [_dot((ob_ref[r, :].astype(F32) * jax.nn.silu(zb_ref[r, :].astype(F32))).astype(BF16), wob_ref[...])
          for r in groups]
    emb = [_row_rms(_dot(p_ref[r, :].astype(BF16), wpp_ref[...]), D_MODEL) * gpost_ref[...] for r in groups]
    merged = [jax.nn.sigmoid(ga_ref[r, :].astype(F32)) * ya[c] + jax.nn.sigmoid(gb_ref[r, :].astype(F32)) * yb[c]
              for c, r in enumerate(groups)]
    x1 = [x_ref[r, :] + _dot(merged[c].astype(BF16), wout_ref[...]) for c, r in enumerate(groups)]
    gate = [jax.nn.sigmoid(_dot((_row_rms(x1[c], D_MODEL) * gpl_ref[...]).astype(BF16), wpg_ref[...]))
            for c in range(FINAL_CHAINS)]
    for c, r in enumerate(groups):
        out_ref[r, :] = x1[c] + gate[c] * emb[c]


def _final(x2, p2, oat, za, ob, zb, ga, gb, woa, wob, wout, gpl, wpg, wpp, gpost, tm):
    t = x2.shape[0]
    row = lambda a: pl.BlockSpec((tm, a.shape[1]), lambda i: (i, 0))
    full = lambda a: pl.BlockSpec(a.shape, lambda i: (0,) * a.ndim)
    acts = (x2, p2, oat, za, ob, zb, ga, gb)
    consts = (woa, wob, wout, gpl, wpg, wpp, gpost)
    act_specs = [row(a) for a in acts]
    act_specs[2] = pl.BlockSpec((A_WIDTH, tm), lambda i: (0, i))
    return pl.pallas_call(
        _final_kernel,
        grid=(t // tm,),
        in_specs=act_specs + [full(a) for a in consts],
        out_specs=pl.BlockSpec((tm, D_MODEL), lambda i: (i, 0)),
        out_shape=jax.ShapeDtypeStruct((t, D_MODEL), F32),
        compiler_params=pltpu.CompilerParams(
            dimension_semantics=("arbitrary",), vmem_limit_bytes=VMEM_LIMIT),
        name="final",
    )(*acts, *consts)


def kernel(x, p, positions, norm_g, w_in, a_q_norm, a_k_norm, a_sinks, rel_bias, w_o_a, b_cq_norm, w_uq, b_ckv_norm, w_uk, w_uv, b_q_norm, b_k_norm, b_kr_norm, w_o_b, w_out, ple_norm_g, w_ple_gate, w_ple_proj, ple_post_g):
    batch, seq, _ = x.shape
    depth = p.shape[0]
    t = batch * seq
    tm = TOKEN_TILE
    offs = np.concatenate([[0], np.cumsum(SPLIT_SIZES)])
    log2e = math.log2(math.e)
    inv_freq = (ROPE_THETA ** (-jnp.arange(0, B_ROPE, 2, dtype=F32) / B_ROPE))[:, None]
    pos_row = positions.reshape(1, t)
    band_bias = _band_bias(rel_bias)
    row = lambda v: v.astype(F32)[None, :]
    col = lambda v: v.astype(F32)[:, None]

    x2 = x.reshape(t, D_MODEL)
    for i in range(depth):
        w = w_in[i]
        cols = [w[:, offs[j]:offs[j + 1]] for j in range(len(SPLIT_SIZES))]
        wt = jnp.concatenate([cols[0], cols[2], cols[1], cols[6]], axis=1).T.astype(BF16)
        w1 = jnp.concatenate([cols[j] for j in (3, 4, 5, 7, 8, 9)], axis=1).astype(BF16)
        gqn = col(b_q_norm[i]) * (B_QK_DIM ** -0.5 * log2e)
        consts = (row(norm_g[i]), wt, w1,
                  col(a_q_norm[i]) * (A_HEAD_DIM ** -0.5 * log2e), col(a_k_norm[i]),
                  row(b_cq_norm[i]), w_uq[i].T.astype(BF16), gqn,
                  row(b_ckv_norm[i]), w_uk[i].T.astype(BF16), w_uv[i].T.astype(BF16),
                  col(b_k_norm[i]), col(b_kr_norm[i]), inv_freq)
        qvt, ka, za, zb, ga, gb, qt, kb, vt = _proj(x2, pos_row, consts, batch, seq, tm, FLASH_Q_TILE // 2)

        sinks = jnp.repeat(a_sinks[i].astype(F32) * log2e, A_BLOCK).reshape(A_KV_HEADS, 1, A_GROUP * A_BLOCK)
        oat = _swa(qvt, ka.reshape(batch, seq, A_KV_WIDTH), band_bias, sinks, batch, seq, tm)
        q_bound = math.sqrt(B_QK_DIM) * jnp.max(jnp.abs(gqn))
        k_bound = jnp.sqrt(B_NOPE * jnp.max(jnp.square(b_k_norm[i].astype(F32)))
                           + B_ROPE * jnp.max(jnp.square(b_kr_norm[i].astype(F32))))
        ob = lax.cond(q_bound * k_bound < MAX_SAFE_SHIFT,
                      lambda: _flash_bounded(k_bound.reshape(1), qt, kb, vt, FLASH_Q_TILE, FLASH_HEADS),
                      lambda: _flash(qt, kb, vt, FLASH_Q_TILE, FLASH_HEADS))

        x2 = _final(x2, p[i].reshape(t, PLE_DIM), oat, za,
                    ob.reshape(t, B_WIDTH), zb, ga, gb,
                    w_o_a[i].astype(BF16), w_o_b[i].astype(BF16), w_out[i].astype(BF16),
                    row(ple_norm_g[i]), w_ple_gate[i].astype(BF16), w_ple_proj[i].astype(BF16),
                    row(ple_post_g[i]), tm)
    return x2.reshape(batch, seq, D_MODEL)
```

```python
import functools
import math

import numpy as np
import jax
import jax.numpy as jnp
from jax import lax
from jax.experimental import pallas as pl
from jax.experimental.pallas import tpu as pltpu

F32 = jnp.float32
BF16 = jnp.bfloat16

D_MODEL = 1024
PLE_DIM = 256
EPS = 1e-6
NEG = -1e30

A_HEADS = 16
A_KV_HEADS = 2
A_HEAD_DIM = 64
A_WIDTH = A_HEADS * A_HEAD_DIM
A_KV_WIDTH = A_KV_HEADS * A_HEAD_DIM
A_GROUP = A_HEADS // A_KV_HEADS
WINDOW = 128
A_BLOCK = 128
N_BUCKETS = 32
MAX_DISTANCE = 128

B_HEADS = 16
B_Q_RANK = 256
B_KV_RANK = 128
B_NOPE = 64
B_ROPE = 32
B_QK_DIM = B_NOPE + B_ROPE
B_VDIM = 64
B_WIDTH = B_HEADS * B_VDIM
ROPE_THETA = 10000.0

SPLIT_SIZES = (A_WIDTH, A_KV_WIDTH, A_KV_WIDTH, A_WIDTH,
               B_Q_RANK, B_KV_RANK, B_ROPE, B_WIDTH,
               D_MODEL, D_MODEL)

LANES = 128
QK_PAD = 128
V_ROWS = 80
VMEM_LIMIT = 56 * 1024 * 1024
MAX_SAFE_SHIFT = 60.0

TOKEN_TILE = 512
FLASH_Q_TILE = 512
FLASH_HEADS = 4
FINAL_CHAINS = 4

QVT_ROWS = A_WIDTH + A_KV_WIDTH
R_KA = QVT_ROWS
R_KR = R_KA + A_KV_WIDTH
WT_ROWS = R_KR + B_ROPE
SWA_V_ROWS = A_HEAD_DIM + 16
C_ZA = 0
C_LAT = C_ZA + A_WIDTH
LAT_W = B_Q_RANK + B_KV_RANK
C_ZB = C_LAT + LAT_W
C_GA = C_ZB + B_WIDTH
C_GB = C_GA + D_MODEL
C_END = C_GB + D_MODEL


def _dot(a, b):
    return jnp.dot(a, b, preferred_element_type=F32)


def _dot_nt(a, b):
    return lax.dot_general(a, b, (((1,), (1,)), ((), ())), preferred_element_type=F32)


def _row_rms(v, width):
    return v * lax.rsqrt(jnp.sum(v * v, axis=-1, keepdims=True) * (1.0 / width) + EPS)


def _col_rms(v):
    return v * lax.rsqrt(jnp.sum(v * v, axis=0, keepdims=True) * (1.0 / v.shape[0]) + EPS)


def _proj_kernel(x_ref, pos_ref, g_ref, wt_ref, w_ref, gq_ref, gk_ref,
                 gcq_ref, wuqt_ref, gqn_ref, gckv_ref, wukt_ref, wuvt_ref, gkn_ref, gkr_ref, freq_ref,
                 qvt_ref, ka_ref, za_ref, zb_ref, ga_ref, gb_ref, qt_ref, k_ref, vt_ref):
    x = x_ref[...]
    tm = x.shape[0]
    h = (_row_rms(x, D_MODEL) * g_ref[...]).astype(BF16)

    wt_out = _dot_nt(wt_ref[...], h)
    gq = gq_ref[...]
    for hd in range(A_HEADS):
        blk = wt_out[hd * A_HEAD_DIM:(hd + 1) * A_HEAD_DIM, :]
        qvt_ref[hd * A_HEAD_DIM:(hd + 1) * A_HEAD_DIM, :] = (_col_rms(blk) * gq).astype(BF16)
    qvt_ref[A_WIDTH:, :] = wt_out[A_WIDTH:QVT_ROWS, :].astype(BF16)
    gk = gk_ref[...]
    kat = [_col_rms(wt_out[R_KA + g * A_HEAD_DIM:R_KA + (g + 1) * A_HEAD_DIM, :]) * gk
           for g in range(A_KV_HEADS)]
    ka_ref[...] = jnp.concatenate(kat, axis=0).T.astype(BF16)
    krt = wt_out[R_KR:, :]

    def proj(c0, c1):
        return _dot(h, w_ref[:, c0:c1])

    za_ref[...] = proj(C_ZA, C_LAT).astype(BF16)
    lat = proj(C_LAT, C_ZB)
    zb_ref[...] = proj(C_ZB, C_GA).astype(BF16)
    ga_ref[...] = proj(C_GA, C_GB).astype(BF16)
    gb_ref[...] = proj(C_GB, C_END).astype(BF16)

    cq = (_row_rms(lat[:, :B_Q_RANK], B_Q_RANK) * gcq_ref[...]).astype(BF16)
    qt = _dot_nt(wuqt_ref[...], cq)
    ang = freq_ref[...] * pos_ref[...].astype(F32)
    cos = jnp.cos(ang)
    sin = jnp.sin(ang)
    gqn = gqn_ref[...]
    half = B_ROPE // 2
    zeros_q = jnp.zeros((QK_PAD - B_QK_DIM, tm), BF16)
    for hd in range(B_HEADS):
        qn = _col_rms(qt[hd * B_QK_DIM:(hd + 1) * B_QK_DIM, :]) * gqn
        x1 = qn[B_NOPE:B_NOPE + half, :]
        x2 = qn[B_NOPE + half:, :]
        qt_ref[0, hd, 0:B_NOPE, :] = qn[:B_NOPE, :].astype(BF16)
        qt_ref[0, hd, B_NOPE:B_NOPE + half, :] = (x1 * cos - x2 * sin).astype(BF16)
        qt_ref[0, hd, B_NOPE + half:B_QK_DIM, :] = (x2 * cos + x1 * sin).astype(BF16)
        qt_ref[0, hd, B_QK_DIM:, :] = zeros_q

    ckv = (_row_rms(lat[:, B_Q_RANK:], B_KV_RANK) * gckv_ref[...]).astype(BF16)
    knt = _dot_nt(wukt_ref[...], ckv)
    vt = _dot_nt(wuvt_ref[...], ckv)

    krn = _col_rms(krt) * gkr_ref[...]
    x1 = krn[:half, :]
    x2 = krn[half:, :]
    k_tail = jnp.concatenate([x1 * cos - x2 * sin, x2 * cos + x1 * sin,
                              jnp.zeros((QK_PAD - B_QK_DIM, tm), F32)], axis=0)
    gkn = gkn_ref[...]
    for hd in range(B_HEADS):
        kn = _col_rms(knt[hd * B_NOPE:(hd + 1) * B_NOPE, :]) * gkn
        k_ref[0, hd] = jnp.concatenate([kn, k_tail], axis=0).T.astype(BF16)
    tk = vt_ref.shape[-1]
    ones_rows = (lax.broadcasted_iota(jnp.int32, (V_ROWS - B_VDIM, tk), 0) == 0).astype(BF16)
    for hd in range(B_HEADS):
        for c in range(tm // tk):
            vt_ref[0, hd, c, 0:B_VDIM, :] = vt[hd * B_VDIM:(hd + 1) * B_VDIM, c * tk:(c + 1) * tk].astype(BF16)
            vt_ref[0, hd, c, B_VDIM:, :] = ones_rows


def _proj(x2, pos_row, consts, batch, seq, tm, tk):
    t = x2.shape[0]
    nt = seq // tm
    nc = tm // tk
    row = lambda w: pl.BlockSpec((tm, w), lambda i: (i, 0))
    full = lambda a: pl.BlockSpec(a.shape, lambda i: (0,) * a.ndim)
    widths = (A_KV_WIDTH, A_WIDTH, B_WIDTH, D_MODEL, D_MODEL)
    return pl.pallas_call(
        _proj_kernel,
        grid=(t // tm,),
        in_specs=[row(D_MODEL), pl.BlockSpec((1, tm), lambda i: (0, i))] + [full(a) for a in consts],
        out_specs=[pl.BlockSpec((QVT_ROWS, tm), lambda i: (0, i))] + [row(w) for w in widths]
        + [pl.BlockSpec((1, B_HEADS, QK_PAD, tm), lambda i: (i // nt, 0, 0, i % nt)),
           pl.BlockSpec((1, B_HEADS, tm, QK_PAD), lambda i: (i // nt, 0, i % nt, 0)),
           pl.BlockSpec((1, B_HEADS, nc, V_ROWS, tk), lambda i: (i // nt, 0, i % nt, 0, 0))],
        out_shape=[jax.ShapeDtypeStruct((QVT_ROWS, t), BF16)]
        + [jax.ShapeDtypeStruct((t, w), BF16) for w in widths]
        + [jax.ShapeDtypeStruct((batch, B_HEADS, QK_PAD, seq), BF16),
           jax.ShapeDtypeStruct((batch, B_HEADS, seq, QK_PAD), BF16),
           jax.ShapeDtypeStruct((batch, B_HEADS, nt * nc, V_ROWS, tk), BF16)],
        compiler_params=pltpu.CompilerParams(
            dimension_semantics=("arbitrary",), vmem_limit_bytes=VMEM_LIMIT),
        name="proj",
    )(x2, pos_row, *consts)


def _band_bias_kernel(rbt_ref, onehot_ref, o_ref):
    rbt = rbt_ref[...]
    onehot = onehot_ref[...]
    t = jnp.zeros((A_HEADS, WINDOW), F32)
    for b in range(N_BUCKETS):
        t = t + rbt[:, b:b + 1] * onehot[b:b + 1, :]
    t = t * math.log2(math.e)
    band = 2 * A_BLOCK
    key = lax.broadcasted_iota(jnp.int32, (band, band), 0)
    neg = jnp.full((1, A_BLOCK), NEG, F32)
    for hd in range(A_HEADS):
        x = jnp.broadcast_to(jnp.concatenate([neg, t[hd:hd + 1, :]], axis=1), (band, band))
        for bit in range(8):
            x = jnp.where((key >> bit) & 1 == 1, pltpu.roll(x, 1 << bit, 1), x)
        g, hh = divmod(hd, A_GROUP)
        o_ref[g, :, hh * A_BLOCK:(hh + 1) * A_BLOCK] = x[:, :A_BLOCK]


def _band_bias(rel_bias):
    dist = np.arange(WINDOW)
    max_exact = N_BUCKETS // 2
    large = max_exact + (np.log(np.maximum(dist, 1).astype(np.float32) / max_exact)
                         / math.log(MAX_DISTANCE / max_exact) * (N_BUCKETS - max_exact)).astype(np.int32)
    bucket = np.where(dist < max_exact, dist, np.minimum(large, N_BUCKETS - 1))
    onehot = jnp.asarray(np.arange(N_BUCKETS)[:, None] == bucket[None, :], F32)
    return pl.pallas_call(
        _band_bias_kernel,
        out_shape=jax.ShapeDtypeStruct((A_KV_HEADS, 2 * A_BLOCK, A_GROUP * A_BLOCK), F32),
        name="band_bias",
    )(rel_bias.astype(F32).T, onehot)


def _swa_kernel(qv_ref, k_ref, kp_ref, vp_ref, bias_ref, sink_ref, o_ref, *, nsub):
    first = pl.program_id(1) == 0
    band = 2 * A_BLOCK
    width = A_GROUP * A_BLOCK
    pad = jnp.logical_and(first, lax.broadcasted_iota(jnp.int32, (band, width), 0) < A_BLOCK)
    zeros_q = jnp.zeros((A_HEAD_DIM, width), BF16)
    ones_rows = (lax.broadcasted_iota(jnp.int32, (SWA_V_ROWS - A_HEAD_DIM, band), 0) == 0).astype(BF16)

    def scores(sb, g):
        r0 = sb * A_BLOCK
        if sb == 0:
            kb = jnp.concatenate([kp_ref[0], k_ref[0, 0:A_BLOCK, :]], axis=0)
        else:
            kb = k_ref[0, r0 - A_BLOCK:r0 + A_BLOCK, :]
        tiles = [qv_ref[(g * A_GROUP + hh) * A_HEAD_DIM:(g * A_GROUP + hh + 1) * A_HEAD_DIM, r0:r0 + A_BLOCK]
                 for hh in range(A_GROUP)]
        qt = jnp.concatenate(tiles, axis=1)
        rhs = jnp.concatenate([qt, zeros_q] if g == 0 else [zeros_q, qt], axis=0)
        s = _dot(kb, rhs) + bias_ref[g]
        if sb == 0:
            s = jnp.where(pad, NEG, s)
        return s

    def finish(sb, g, s):
        r0 = sb * A_BLOCK
        sink = sink_ref[g]
        m = jnp.maximum(jnp.max(s, axis=0, keepdims=True), sink)
        e = jnp.exp2(s - m).astype(BF16)
        v0 = A_WIDTH + g * A_HEAD_DIM
        if sb == 0:
            vt = jnp.concatenate([vp_ref[g * A_HEAD_DIM:(g + 1) * A_HEAD_DIM, :],
                                  qv_ref[v0:v0 + A_HEAD_DIM, 0:A_BLOCK]], axis=1)
        else:
            vt = qv_ref[v0:v0 + A_HEAD_DIM, r0 - A_BLOCK:r0 + A_BLOCK]
        acc = _dot(jnp.concatenate([vt, ones_rows], axis=0), e)
        den = acc[A_HEAD_DIM:A_HEAD_DIM + 1, :] + jnp.exp2(sink - m)
        o = (acc[:A_HEAD_DIM, :] / den).astype(BF16)
        for hh in range(A_GROUP):
            hd = g * A_GROUP + hh
            o_ref[hd * A_HEAD_DIM:(hd + 1) * A_HEAD_DIM, r0:r0 + A_BLOCK] = o[:, hh * A_BLOCK:(hh + 1) * A_BLOCK]

    units = [(sb, g) for sb in range(nsub) for g in range(A_KV_HEADS)]
    s_next = scores(*units[0])
    for n, unit in enumerate(units):
        s_cur = s_next
        if n + 1 < len(units):
            s_next = scores(*units[n + 1])
        finish(*unit, s_cur)


def _swa(qvt, ka, bias, sinks, batch, seq, tq):
    nsub = tq // A_BLOCK
    nt = seq // tq
    vrow = A_WIDTH // A_KV_WIDTH
    return pl.pallas_call(
        functools.partial(_swa_kernel, nsub=nsub),
        grid=(batch, nt),
        in_specs=[pl.BlockSpec((QVT_ROWS, tq), lambda b, i: (0, b * nt + i)),
                  pl.BlockSpec((1, tq, A_KV_WIDTH), lambda b, i: (b, i, 0)),
                  pl.BlockSpec((1, A_BLOCK, A_KV_WIDTH), lambda b, i: (b, jnp.maximum(i * nsub - 1, 0), 0)),
                  pl.BlockSpec((A_KV_WIDTH, A_BLOCK),
                               lambda b, i: (vrow, jnp.maximum((b * nt + i) * nsub - 1, 0))),
                  pl.BlockSpec(bias.shape, lambda b, i: (0, 0, 0)),
                  pl.BlockSpec(sinks.shape, lambda b, i: (0, 0, 0))],
        out_specs=pl.BlockSpec((A_WIDTH, tq), lambda b, i: (0, b * nt + i)),
        out_shape=jax.ShapeDtypeStruct((A_WIDTH, batch * seq), BF16),
        compiler_params=pltpu.CompilerParams(
            dimension_semantics=("arbitrary", "arbitrary"), vmem_limit_bytes=VMEM_LIMIT),
        name="swa",
    )(qvt, ka, ka, qvt, bias, sinks)


def _flash_kernel(qt_ref, k_ref, vt_ref, o_ref, m_sc, acc_sc, s0_sc, s1_sc, bm0_sc, bm1_sc, *, tq, nh):
    tk = tq // 2
    i = pl.program_id(2)
    m_sc[...] = jnp.full(m_sc.shape, NEG, F32)
    acc_sc[...] = jnp.zeros(acc_sc.shape, F32)
    s_bufs = (s0_sc, s1_sc)
    bm_bufs = (bm0_sc, bm1_sc)

    def scores(h, j, slot, diag_half):
        kb = k_ref[0, h, pl.ds(pl.multiple_of(j * tk, tk), tk), :]
        s = _dot(kb, qt_ref[0, h])
        if diag_half is not None:
            key = lax.broadcasted_iota(jnp.int32, (tk, tq), 0) + diag_half * tk
            qry = lax.broadcasted_iota(jnp.int32, (tk, tq), 1)
            s = jnp.where(key <= qry, s, NEG)
        s_bufs[slot][h] = s
        bm_bufs[slot][h] = jnp.max(s, axis=0, keepdims=True)

    def softmax_pv(h, j, slot):
        m_old = m_sc[h]
        m_new = jnp.maximum(m_old, bm_bufs[slot][h])
        p = jnp.exp2(s_bufs[slot][h] - m_new).astype(BF16)
        alpha = jnp.exp2(m_old - m_new)
        m_sc[h] = m_new
        acc_sc[h] = acc_sc[h] * alpha + _dot(vt_ref[0, h, j], p)

    def stage(j, slot, next_diag_half, has_next=True):
        for h in range(nh):
            if has_next:
                scores(h, j + 1, 1 - slot, next_diag_half)
            softmax_pv(h, j, slot)

    @pl.when(i == 0)
    def _():
        for h in range(nh):
            scores(h, 0, 0, 0)

    @pl.when(i > 0)
    def _():
        for h in range(nh):
            scores(h, 0, 0, None)

    def pair(jj, carry):
        stage(2 * jj, 0, None)
        stage(2 * jj + 1, 1, None)
        return carry

    lax.fori_loop(0, i - 1, pair, 0)

    @pl.when(i > 0)
    def _():
        stage(2 * i - 2, 0, None)
        stage(2 * i - 1, 1, 0)

    stage(2 * i, 0, 1)
    stage(2 * i + 1, 1, None, has_next=False)

    outs = []
    for h in range(nh):
        acc = acc_sc[h]
        outs.append(acc[:B_VDIM, :] / acc[B_VDIM:B_VDIM + 1, :])
    o_ref[0] = jnp.concatenate(outs, axis=0).T.astype(BF16)


def _flash(qt, k, vt, tq, nh):
    batch, heads, _, seq = qt.shape
    tk = tq // 2
    nq = seq // tq
    return pl.pallas_call(
        functools.partial(_flash_kernel, tq=tq, nh=nh),
        grid=(batch, heads // nh, nq),
        in_specs=[pl.BlockSpec((1, nh, QK_PAD, tq), lambda b, g, i: (b, g, 0, i)),
                  pl.BlockSpec((1, nh, seq, QK_PAD), lambda b, g, i: (b, g, 0, 0)),
                  pl.BlockSpec((1, nh, seq // tk, V_ROWS, tk), lambda b, g, i: (b, g, 0, 0, 0))],
        out_specs=pl.BlockSpec((1, tq, nh * B_VDIM), lambda b, g, i: (b, i, g)),
        out_shape=jax.ShapeDtypeStruct((batch, seq, B_WIDTH), BF16),
        scratch_shapes=[pltpu.VMEM((nh, 1, tq), F32), pltpu.VMEM((nh, V_ROWS, tq), F32),
                        pltpu.VMEM((nh, tk, tq), F32), pltpu.VMEM((nh, tk, tq), F32),
                        pltpu.VMEM((nh, 1, tq), F32), pltpu.VMEM((nh, 1, tq), F32)],
        compiler_params=pltpu.CompilerParams(
            dimension_semantics=("arbitrary", "arbitrary", "arbitrary"),
            vmem_limit_bytes=VMEM_LIMIT),
        name="flash",
    )(qt, k, vt)


def _flash_bounded_kernel(kmax_ref, qt_ref, qtn_ref, k_ref, vt_ref, o_ref, mq_sc, mqn_sc, acc_sc, p0_sc, p1_sc,
                          *, tq, nh):
    tk = tq // 2
    i = pl.program_id(2)
    acc_sc[...] = jnp.zeros(acc_sc.shape, F32)

    def shift(q_ref, h):
        q = q_ref[0, h].astype(F32)
        return jnp.sqrt(jnp.sum(q * q, axis=0, keepdims=True)) * kmax_ref[0]

    @pl.when(i > 0)
    def _():
        mq_sc[...] = mqn_sc[...]

    p_bufs = (p0_sc, p1_sc)

    def probs(h, j, slot, diag_half, next_q=False):
        kb = k_ref[0, h, pl.ds(pl.multiple_of(j * tk, tk), tk), :]
        if diag_half == 1:
            s = _dot(kb, qt_ref[0, h, :, tk:])
            key = lax.broadcasted_iota(jnp.int32, (tk, tk), 0)
            qry = lax.broadcasted_iota(jnp.int32, (tk, tk), 1)
            s = jnp.where(key <= qry, s, NEG)
            p_bufs[slot][h, :, tk:] = jnp.exp2(s - mq_sc[h, :, tk:]).astype(BF16)
            return
        s = _dot(kb, (qtn_ref if next_q else qt_ref)[0, h])
        if diag_half is not None:
            key = lax.broadcasted_iota(jnp.int32, (tk, tq), 0)
            qry = lax.broadcasted_iota(jnp.int32, (tk, tq), 1)
            s = jnp.where(key <= qry, s, NEG)
        p_bufs[slot][h] = jnp.exp2(s - (mqn_sc if next_q else mq_sc)[h]).astype(BF16)

    def pv(h, j, slot, upper_half_only=False):
        if upper_half_only:
            acc_sc[h, :, tk:] += _dot(vt_ref[0, h, j], p_bufs[slot][h, :, tk:])
        else:
            acc_sc[h] += _dot(vt_ref[0, h, j], p_bufs[slot][h])

    def stage(j, slot, next_diag_half, next_q=False, upper_half_only=False):
        for h in range(nh):
            probs(h, 0 if next_q else j + 1, 1 - slot, next_diag_half, next_q)
            pv(h, j, slot, upper_half_only)

    def pair(jj, last_next_diag_half=None):
        stage(2 * jj, 0, None)
        stage(2 * jj + 1, 1, last_next_diag_half)

    @pl.when(i == 0)
    def _():
        for h in range(nh):
            mq_sc[h] = shift(qt_ref, h)
        for h in range(nh):
            probs(h, 0, 0, 0)

    plain = i - 1

    def two_pairs(n, carry):
        pair(2 * n)
        pair(2 * n + 1)
        return carry

    lax.fori_loop(0, plain // 2, two_pairs, 0)

    @pl.when(jnp.logical_and(plain > 0, plain % 2 == 1))
    def _():
        pair(plain - 1)

    @pl.when(i > 0)
    def _():
        pair(i - 1, 0)

    for h in range(nh):
        mqn_sc[h] = shift(qtn_ref, h)
    stage(2 * i, 0, 1)
    stage(2 * i + 1, 1, None, next_q=True, upper_half_only=True)

    outs = []
    for h in range(nh):
        acc = acc_sc[h]
        outs.append(acc[:B_VDIM, :] / acc[B_VDIM:B_VDIM + 1, :])
    o_ref[0] = jnp.concatenate(outs, axis=0).T.astype(BF16)


def _flash_bounded(kmax, qt, k, vt, tq, nh):
    batch, heads, _, seq = qt.shape
    tk = tq // 2
    nq = seq // tq
    return pl.pallas_call(
        functools.partial(_flash_bounded_kernel, tq=tq, nh=nh),
        grid=(batch, heads // nh, nq),
        in_specs=[pl.BlockSpec(memory_space=pltpu.SMEM),
                  pl.BlockSpec((1, nh, QK_PAD, tq), lambda b, g, i: (b, g, 0, i)),
                  pl.BlockSpec((1, nh, QK_PAD, tq), lambda b, g, i: (b, g, 0, jnp.minimum(i + 1, nq - 1))),
                  pl.BlockSpec((1, nh, seq, QK_PAD), lambda b, g, i: (b, g, 0, 0)),
                  pl.BlockSpec((1, nh, seq // tk, V_ROWS, tk), lambda b, g, i: (b, g, 0, 0, 0))],
        out_specs=pl.BlockSpec((1, tq, nh * B_VDIM), lambda b, g, i: (b, i, g)),
        out_shape=jax.ShapeDtypeStruct((batch, seq, B_WIDTH), BF16),
        scratch_shapes=[pltpu.VMEM((nh, 1, tq), F32), pltpu.VMEM((nh, 1, tq), F32),
                        pltpu.VMEM((nh, V_ROWS, tq), F32),
                        pltpu.VMEM((nh, tk, tq), BF16), pltpu.VMEM((nh, tk, tq), BF16)],
        compiler_params=pltpu.CompilerParams(
            dimension_semantics=("arbitrary", "arbitrary", "arbitrary"),
            vmem_limit_bytes=VMEM_LIMIT),
        name="flash_bounded",
    )(kmax, qt, qt, k, vt)


def _final_kernel(x_ref, p_ref, oat_ref, za_ref, ob_ref, zb_ref, ga_ref, gb_ref,
                  woa_ref, wob_ref, wout_ref, gpl_ref, wpg_ref, wpp_ref, gpost_ref, out_ref):
    rows = x_ref.shape[0] // FINAL_CHAINS
    groups = [slice(c * rows, (c + 1) * rows) for c in range(FINAL_CHAINS)]
    oa = [oat_ref[:, r].astype(F32).T for r in groups]
    ya = [_dot((oa[c] * jax.nn.silu(za_ref[r, :].astype(F32))).astype(BF16), woa_ref[...])
          for c, r in enumerate(groups)]
    yb = [_dot((ob_ref[r, :].astype(F32) * jax.nn.silu(zb_ref[r, :].astype(F32))).astype(BF16), wob_ref[...])
          for r in groups]
    emb = [_row_rms(_dot(p_ref[r, :].astype(BF16), wpp_ref[...]), D_MODEL) * gpost_ref[...] for r in groups]
    merged = [jax.nn.sigmoid(ga_ref[r, :].astype(F32)) * ya[c] + jax.nn.sigmoid(gb_ref[r, :].astype(F32)) * yb[c]
              for c, r in enumerate(groups)]
    x1 = [x_ref[r, :] + _dot(merged[c].astype(BF16), wout_ref[...]) for c, r in enumerate(groups)]
    gate = [jax.nn.sigmoid(_dot((_row_rms(x1[c], D_MODEL) * gpl_ref[...]).astype(BF16), wpg_ref[...]))
            for c in range(FINAL_CHAINS)]
    for c, r in enumerate(groups):
        out_ref[r, :] = x1[c] + gate[c] * emb[c]


def _final(x2, p2, oat, za, ob, zb, ga, gb, woa, wob, wout, gpl, wpg, wpp, gpost, tm):
    t = x2.shape[0]
    row = lambda a: pl.BlockSpec((tm, a.shape[1]), lambda i: (i, 0))
    full = lambda a: pl.BlockSpec(a.shape, lambda i: (0,) * a.ndim)
    acts = (x2, p2, oat, za, ob, zb, ga, gb)
    consts = (woa, wob, wout, gpl, wpg, wpp, gpost)
    act_specs = [row(a) for a in acts]
    act_specs[2] = pl.BlockSpec((A_WIDTH, tm), lambda i: (0, i))
    return pl.pallas_call(
        _final_kernel,
        grid=(t // tm,),
        in_specs=act_specs + [full(a) for a in consts],
        out_specs=pl.BlockSpec((tm, D_MODEL), lambda i: (i, 0)),
        out_shape=jax.ShapeDtypeStruct((t, D_MODEL), F32),
        compiler_params=pltpu.CompilerParams(
            dimension_semantics=("arbitrary",), vmem_limit_bytes=VMEM_LIMIT),
        name="final",
    )(*acts, *consts)


def kernel(x, p, positions, norm_g, w_in, a_q_norm, a_k_norm, a_sinks, rel_bias, w_o_a, b_cq_norm, w_uq, b_ckv_norm, w_uk, w_uv, b_q_norm, b_k_norm, b_kr_norm, w_o_b, w_out, ple_norm_g, w_ple_gate, w_ple_proj, ple_post_g):
    batch, seq, _ = x.shape
    depth = p.shape[0]
    t = batch * seq
    tm = TOKEN_TILE
    offs = np.concatenate([[0], np.cumsum(SPLIT_SIZES)])
    log2e = math.log2(math.e)
    inv_freq = (ROPE_THETA ** (-jnp.arange(0, B_ROPE, 2, dtype=F32) / B_ROPE))[:, None]
    pos_row = positions.reshape(1, t)
    band_bias = _band_bias(rel_bias)
    row = lambda v: v.astype(F32)[None, :]
    col = lambda v: v.astype(F32)[:, None]

    x2 = x.reshape(t, D_MODEL)
    for i in range(depth):
        w = w_in[i]
        cols = [w[:, offs[j]:offs[j + 1]] for j in range(len(SPLIT_SIZES))]
        wt = jnp.concatenate([cols[0], cols[2], cols[1], cols[6]], axis=1).T.astype(BF16)
        w1 = jnp.concatenate([cols[j] for j in (3, 4, 5, 7, 8, 9)], axis=1).astype(BF16)
        gqn = col(b_q_norm[i]) * (B_QK_DIM ** -0.5 * log2e)
        consts = (row(norm_g[i]), wt, w1,
                  col(a_q_norm[i]) * (A_HEAD_DIM ** -0.5 * log2e), col(a_k_norm[i]),
                  row(b_cq_norm[i]), w_uq[i].T.astype(BF16), gqn,
                  row(b_ckv_norm[i]), w_uk[i].T.astype(BF16), w_uv[i].T.astype(BF16),
                  col(b_k_norm[i]), col(b_kr_norm[i]), inv_freq)
        qvt, ka, za, zb, ga, gb, qt, kb, vt = _proj(x2, pos_row, consts, batch, seq, tm, FLASH_Q_TILE // 2)

        sinks = jnp.repeat(a_sinks[i].astype(F32) * log2e, A_BLOCK).reshape(A_KV_HEADS, 1, A_GROUP * A_BLOCK)
        oat = _swa(qvt, ka.reshape(batch, seq, A_KV_WIDTH), band_bias, sinks, batch, seq, tm)
        q_bound = math.sqrt(B_QK_DIM) * jnp.max(jnp.abs(gqn))
        k_bound = jnp.sqrt(B_NOPE * jnp.max(jnp.square(b_k_norm[i].astype(F32)))
                           + B_ROPE * jnp.max(jnp.square(b_kr_norm[i].astype(F32))))
        ob = lax.cond(q_bound * k_bound < MAX_SAFE_SHIFT,
                      lambda: _flash_bounded(k_bound.reshape(1), qt, kb, vt, FLASH_Q_TILE, FLASH_HEADS),
                      lambda: _flash(qt, kb, vt, FLASH_Q_TILE, FLASH_HEADS))

        x2 = _final(x2, p[i].reshape(t, PLE_DIM), oat, za,
                    ob.reshape(t, B_WIDTH), zb, ga, gb,
                    w_o_a[i].astype(BF16), w_o_b[i].astype(BF16), w_out[i].astype(BF16),
                    row(ple_norm_g[i]), w_ple_gate[i].astype(BF16), w_ple_proj[i].astype(BF16),
                    row(ple_post_g[i]), tm)
    return x2.reshape(batch, seq, D_MODEL)
```

```python
import functools
import math

import numpy as np
import jax
import jax.numpy as jnp
from jax import lax
from jax.experimental import pallas as pl
from jax.experimental.pallas import tpu as pltpu

F32 = jnp.float32
BF16 = jnp.bfloat16

D_MODEL = 1024
PLE_DIM = 256
EPS = 1e-6
NEG = -1e30

A_HEADS = 16
A_KV_HEADS = 2
A_HEAD_DIM = 64
A_WIDTH = A_HEADS * A_HEAD_DIM
A_KV_WIDTH = A_KV_HEADS * A_HEAD_DIM
A_GROUP = A_HEADS // A_KV_HEADS
WINDOW = 128
A_BLOCK = 128
N_BUCKETS = 32
MAX_DISTANCE = 128

B_HEADS = 16
B_Q_RANK = 256
B_KV_RANK = 128
B_NOPE = 64
B_ROPE = 32
B_QK_DIM = B_NOPE + B_ROPE
B_VDIM = 64
B_WIDTH = B_HEADS * B_VDIM
ROPE_THETA = 10000.0

SPLIT_SIZES = (A_WIDTH, A_KV_WIDTH, A_KV_WIDTH, A_WIDTH,
               B_Q_RANK, B_KV_RANK, B_ROPE, B_WIDTH,
               D_MODEL, D_MODEL)

LANES = 128
QK_PAD = 128
V_ROWS = 80
VMEM_LIMIT = 56 * 1024 * 1024
MAX_SAFE_SHIFT = 60.0

TOKEN_TILE = 512
FLASH_Q_TILE = 512
FLASH_HEADS = 4
FINAL_CHAINS = 4

QVT_ROWS = A_WIDTH + A_KV_WIDTH
R_KA = QVT_ROWS
R_KR = R_KA + A_KV_WIDTH
WT_ROWS = R_KR + B_ROPE
SWA_V_ROWS = A_HEAD_DIM + 16
C_ZA = 0
C_LAT = C_ZA + A_WIDTH
LAT_W = B_Q_RANK + B_KV_RANK
C_ZB = C_LAT + LAT_W
C_GA = C_ZB + B_WIDTH
C_GB = C_GA + D_MODEL
C_END = C_GB + D_MODEL


def _dot(a, b):
    return jnp.dot(a, b, preferred_element_type=F32)


def _dot_nt(a, b):
    return lax.dot_general(a, b, (((1,), (1,)), ((), ())), preferred_element_type=F32)


def _row_rms(v, width):
    return v * lax.rsqrt(jnp.sum(v * v, axis=-1, keepdims=True) * (1.0 / width) + EPS)


def _col_rms(v):
    return v * lax.rsqrt(jnp.sum(v * v, axis=0, keepdims=True) * (1.0 / v.shape[0]) + EPS)


def _proj_kernel(x_ref, pos_ref, g_ref, wt_ref, w_ref, gq_ref, gk_ref,
                 gcq_ref, wuqt_ref, gqn_ref, gckv_ref, wukt_ref, wuvt_ref, gkn_ref, gkr_ref, freq_ref,
                 qvt_ref, ka_ref, za_ref, zb_ref, ga_ref, gb_ref, qt_ref, k_ref, vt_ref):
    x = x_ref[...]
    tm = x.shape[0]
    h = (_row_rms(x, D_MODEL) * g_ref[...]).astype(BF16)

    wt_out = _dot_nt(wt_ref[...], h)
    gq = gq_ref[...]
    for hd in range(A_HEADS):
        blk = wt_out[hd * A_HEAD_DIM:(hd + 1) * A_HEAD_DIM, :]
        qvt_ref[hd * A_HEAD_DIM:(hd + 1) * A_HEAD_DIM, :] = (_col_rms(blk) * gq).astype(BF16)
    qvt_ref[A_WIDTH:, :] = wt_out[A_WIDTH:QVT_ROWS, :].astype(BF16)
    gk = gk_ref[...]
    kat = [_col_rms(wt_out[R_KA + g * A_HEAD_DIM:R_KA + (g + 1) * A_HEAD_DIM, :]) * gk
           for g in range(A_KV_HEADS)]
    ka_ref[...] = jnp.concatenate(kat, axis=0).T.astype(BF16)
    krt = wt_out[R_KR:, :]

    def proj(c0, c1):
        return _dot(h, w_ref[:, c0:c1])

    za_ref[...] = proj(C_ZA, C_LAT).astype(BF16)
    lat = proj(C_LAT, C_ZB)
    zb_ref[...] = proj(C_ZB, C_GA).astype(BF16)
    ga_ref[...] = proj(C_GA, C_GB).astype(BF16)
    gb_ref[...] = proj(C_GB, C_END).astype(BF16)

    cq = (_row_rms(lat[:, :B_Q_RANK], B_Q_RANK) * gcq_ref[...]).astype(BF16)
    qt = _dot_nt(wuqt_ref[...], cq)
    ang = freq_ref[...] * pos_ref[...].astype(F32)
    cos = jnp.cos(ang)
    sin = jnp.sin(ang)
    gqn = gqn_ref[...]
    half = B_ROPE // 2
    zeros_q = jnp.zeros((QK_PAD - B_QK_DIM, tm), BF16)
    for hd in range(B_HEADS):
        qn = _col_rms(qt[hd * B_QK_DIM:(hd + 1) * B_QK_DIM, :]) * gqn
        x1 = qn[B_NOPE:B_NOPE + half, :]
        x2 = qn[B_NOPE + half:, :]
        qt_ref[0, hd, 0:B_NOPE, :] = qn[:B_NOPE, :].astype(BF16)
        qt_ref[0, hd, B_NOPE:B_NOPE + half, :] = (x1 * cos - x2 * sin).astype(BF16)
        qt_ref[0, hd, B_NOPE + half:B_QK_DIM, :] = (x2 * cos + x1 * sin).astype(BF16)
        qt_ref[0, hd, B_QK_DIM:, :] = zeros_q

    ckv = (_row_rms(lat[:, B_Q_RANK:], B_KV_RANK) * gckv_ref[...]).astype(BF16)
    knt = _dot_nt(wukt_ref[...], ckv)
    vt = _dot_nt(wuvt_ref[...], ckv)

    krn = _col_rms(krt) * gkr_ref[...]
    x1 = krn[:half, :]
    x2 = krn[half:, :]
    k_tail = jnp.concatenate([x1 * cos - x2 * sin, x2 * cos + x1 * sin,
                              jnp.zeros((QK_PAD - B_QK_DIM, tm), F32)], axis=0)
    gkn = gkn_ref[...]
    for hd in range(B_HEADS):
        kn = _col_rms(knt[hd * B_NOPE:(hd + 1) * B_NOPE, :]) * gkn
        k_ref[0, hd] = jnp.concatenate([kn, k_tail], axis=0).T.astype(BF16)
    tk = vt_ref.shape[-1]
    ones_rows = (lax.broadcasted_iota(jnp.int32, (V_ROWS - B_VDIM, tk), 0) == 0).astype(BF16)
    for hd in range(B_HEADS):
        for c in range(tm // tk):
            vt_ref[0, hd, c, 0:B_VDIM, :] = vt[hd * B_VDIM:(hd + 1) * B_VDIM, c * tk:(c + 1) * tk].astype(BF16)
            vt_ref[0, hd, c, B_VDIM:, :] = ones_rows


def _proj(x2, pos_row, consts, batch, seq, tm, tk):
    t = x2.shape[0]
    nt = seq // tm
    nc = tm // tk
    row = lambda w: pl.BlockSpec((tm, w), lambda i: (i, 0))
    full = lambda a: pl.BlockSpec(a.shape, lambda i: (0,) * a.ndim)
    widths = (A_KV_WIDTH, A_WIDTH, B_WIDTH, D_MODEL, D_MODEL)
    return pl.pallas_call(
        _proj_kernel,
        grid=(t // tm,),
        in_specs=[row(D_MODEL), pl.BlockSpec((1, tm), lambda i: (0, i))] + [full(a) for a in consts],
        out_specs=[pl.BlockSpec((QVT_ROWS, tm), lambda i: (0, i))] + [row(w) for w in widths]
        + [pl.BlockSpec((1, B_HEADS, QK_PAD, tm), lambda i: (i // nt, 0, 0, i % nt)),
           pl.BlockSpec((1, B_HEADS, tm, QK_PAD), lambda i: (i // nt, 0, i % nt, 0)),
           pl.BlockSpec((1, B_HEADS, nc, V_ROWS, tk), lambda i: (i // nt, 0, i % nt, 0, 0))],
        out_shape=[jax.ShapeDtypeStruct((QVT_ROWS, t), BF16)]
        + [jax.ShapeDtypeStruct((t, w), BF16) for w in widths]
        + [jax.ShapeDtypeStruct((batch, B_HEADS, QK_PAD, seq), BF16),
           jax.ShapeDtypeStruct((batch, B_HEADS, seq, QK_PAD), BF16),
           jax.ShapeDtypeStruct((batch, B_HEADS, nt * nc, V_ROWS, tk), BF16)],
        compiler_params=pltpu.CompilerParams(
            dimension_semantics=("arbitrary",), vmem_limit_bytes=VMEM_LIMIT),
        name="proj",
    )(x2, pos_row, *consts)


def _band_bias_kernel(rbt_ref, onehot_ref, o_ref):
    rbt = rbt_ref[...]
    onehot = onehot_ref[...]
    t = jnp.zeros((A_HEADS, WINDOW), F32)
    for b in range(N_BUCKETS):
        t = t + rbt[:, b:b + 1] * onehot[b:b + 1, :]
    t = t * math.log2(math.e)
    band = 2 * A_BLOCK
    key = lax.broadcasted_iota(jnp.int32, (band, band), 0)
    neg = jnp.full((1, A_BLOCK), NEG, F32)
    for hd in range(A_HEADS):
        x = jnp.broadcast_to(jnp.concatenate([neg, t[hd:hd + 1, :]], axis=1), (band, band))
        for bit in range(8):
            x = jnp.where((key >> bit) & 1 == 1, pltpu.roll(x, 1 << bit, 1), x)
        g, hh = divmod(hd, A_GROUP)
        o_ref[g, :, hh * A_BLOCK:(hh + 1) * A_BLOCK] = x[:, :A_BLOCK]


def _band_bias(rel_bias):
    dist = np.arange(WINDOW)
    max_exact = N_BUCKETS // 2
    large = max_exact + (np.log(np.maximum(dist, 1).astype(np.float32) / max_exact)
                         / math.log(MAX_DISTANCE / max_exact) * (N_BUCKETS - max_exact)).astype(np.int32)
    bucket = np.where(dist < max_exact, dist, np.minimum(large, N_BUCKETS - 1))
    onehot = jnp.asarray(np.arange(N_BUCKETS)[:, None] == bucket[None, :], F32)
    return pl.pallas_call(
        _band_bias_kernel,
        out_shape=jax.ShapeDtypeStruct((A_KV_HEADS, 2 * A_BLOCK, A_GROUP * A_BLOCK), F32),
        name="band_bias",
    )(rel_bias.astype(F32).T, onehot)


def _swa_kernel(qv_ref, k_ref, kp_ref, vp_ref, bias_ref, sink_ref, o_ref, *, nsub):
    first = pl.program_id(1) == 0
    band = 2 * A_BLOCK
    width = A_GROUP * A_BLOCK
    pad = jnp.logical_and(first, lax.broadcasted_iota(jnp.int32, (band, width), 0) < A_BLOCK)
    zeros_q = jnp.zeros((A_HEAD_DIM, width), BF16)
    ones_rows = (lax.broadcasted_iota(jnp.int32, (SWA_V_ROWS - A_HEAD_DIM, band), 0) == 0).astype(BF16)

    def scores(sb, g):
        r0 = sb * A_BLOCK
        if sb == 0:
            kb = jnp.concatenate([kp_ref[0], k_ref[0, 0:A_BLOCK, :]], axis=0)
        else:
            kb = k_ref[0, r0 - A_BLOCK:r0 + A_BLOCK, :]
        tiles = [qv_ref[(g * A_GROUP + hh) * A_HEAD_DIM:(g * A_GROUP + hh + 1) * A_HEAD_DIM, r0:r0 + A_BLOCK]
                 for hh in range(A_GROUP)]
        qt = jnp.concatenate(tiles, axis=1)
        rhs = jnp.concatenate([qt, zeros_q] if g == 0 else [zeros_q, qt], axis=0)
        s = _dot(kb, rhs) + bias_ref[g]
        if sb == 0:
            s = jnp.where(pad, NEG, s)
        return s

    def finish(sb, g, s):
        r0 = sb * A_BLOCK
        sink = sink_ref[g]
        m = jnp.maximum(jnp.max(s, axis=0, keepdims=True), sink)
        e = jnp.exp2(s - m).astype(BF16)
        v0 = A_WIDTH + g * A_HEAD_DIM
        if sb == 0:
            vt = jnp.concatenate([vp_ref[g * A_HEAD_DIM:(g + 1) * A_HEAD_DIM, :],
                                  qv_ref[v0:v0 + A_HEAD_DIM, 0:A_BLOCK]], axis=1)
        else:
            vt = qv_ref[v0:v0 + A_HEAD_DIM, r0 - A_BLOCK:r0 + A_BLOCK]
        acc = _dot(jnp.concatenate([vt, ones_rows], axis=0), e)
        den = acc[A_HEAD_DIM:A_HEAD_DIM + 1, :] + jnp.exp2(sink - m)
        o = (acc[:A_HEAD_DIM, :] / den).astype(BF16)
        for hh in range(A_GROUP):
            hd = g * A_GROUP + hh
            o_ref[hd * A_HEAD_DIM:(hd + 1) * A_HEAD_DIM, r0:r0 + A_BLOCK] = o[:, hh * A_BLOCK:(hh + 1) * A_BLOCK]

    units = [(sb, g) for sb in range(nsub) for g in range(A_KV_HEADS)]
    s_next = scores(*units[0])
    for n, unit in enumerate(units):
        s_cur = s_next
        if n + 1 < len(units):
            s_next = scores(*units[n + 1])
        finish(*unit, s_cur)


def _swa(qvt, ka, bias, sinks, batch, seq, tq):
    nsub = tq // A_BLOCK
    nt = seq // tq
    vrow = A_WIDTH // A_KV_WIDTH
    return pl.pallas_call(
        functools.partial(_swa_kernel, nsub=nsub),
        grid=(batch, nt),
        in_specs=[pl.BlockSpec((QVT_ROWS, tq), lambda b, i: (0, b * nt + i)),
                  pl.BlockSpec((1, tq, A_KV_WIDTH), lambda b, i: (b, i, 0)),
                  pl.BlockSpec((1, A_BLOCK, A_KV_WIDTH), lambda b, i: (b, jnp.maximum(i * nsub - 1, 0), 0)),
                  pl.BlockSpec((A_KV_WIDTH, A_BLOCK),
                               lambda b, i: (vrow, jnp.maximum((b * nt + i) * nsub - 1, 0))),
                  pl.BlockSpec(bias.shape, lambda b, i: (0, 0, 0)),
                  pl.BlockSpec(sinks.shape, lambda b, i: (0, 0, 0))],
        out_specs=pl.BlockSpec((A_WIDTH, tq), lambda b, i: (0, b * nt + i)),
        out_shape=jax.ShapeDtypeStruct((A_WIDTH, batch * seq), BF16),
        compiler_params=pltpu.CompilerParams(
            dimension_semantics=("arbitrary", "arbitrary"), vmem_limit_bytes=VMEM_LIMIT),
        name="swa",
    )(qvt, ka, ka, qvt, bias, sinks)


def _flash_kernel(qt_ref, k_ref, vt_ref, o_ref, m_sc, acc_sc, s0_sc, s1_sc, bm0_sc, bm1_sc, *, tq, nh):
    tk = tq // 2
    i = pl.program_id(2)
    m_sc[...] = jnp.full(m_sc.shape, NEG, F32)
    acc_sc[...] = jnp.zeros(acc_sc.shape, F32)
    s_bufs = (s0_sc, s1_sc)
    bm_bufs = (bm0_sc, bm1_sc)

    def scores(h, j, slot, diag_half):
        kb = k_ref[0, h, pl.ds(pl.multiple_of(j * tk, tk), tk), :]
        s = _dot(kb, qt_ref[0, h])
        if diag_half is not None:
            key = lax.broadcasted_iota(jnp.int32, (tk, tq), 0) + diag_half * tk
            qry = lax.broadcasted_iota(jnp.int32, (tk, tq), 1)
            s = jnp.where(key <= qry, s, NEG)
        s_bufs[slot][h] = s
        bm_bufs[slot][h] = jnp.max(s, axis=0, keepdims=True)

    def softmax_pv(h, j, slot):
        m_old = m_sc[h]
        m_new = jnp.maximum(m_old, bm_bufs[slot][h])
        p = jnp.exp2(s_bufs[slot][h] - m_new).astype(BF16)
        alpha = jnp.exp2(m_old - m_new)
        m_sc[h] = m_new
        acc_sc[h] = acc_sc[h] * alpha + _dot(vt_ref[0, h, j], p)

    def stage(j, slot, next_diag_half, has_next=True):
        for h in range(nh):
            if has_next:
                scores(h, j + 1, 1 - slot, next_diag_half)
            softmax_pv(h, j, slot)

    @pl.when(i == 0)
    def _():
        for h in range(nh):
            scores(h, 0, 0, 0)

    @pl.when(i > 0)
    def _():
        for h in range(nh):
            scores(h, 0, 0, None)

    def pair(jj, carry):
        stage(2 * jj, 0, None)
        stage(2 * jj + 1, 1, None)
        return carry

    lax.fori_loop(0, i - 1, pair, 0)

    @pl.when(i > 0)
    def _():
        stage(2 * i - 2, 0, None)
        stage(2 * i - 1, 1, 0)

    stage(2 * i, 0, 1)
    stage(2 * i + 1, 1, None, has_next=False)

    for h in range(nh):
        acc = acc_sc[h]
        o_ref[h * B_VDIM:(h + 1) * B_VDIM, :] = (acc[:B_VDIM, :] / acc[B_VDIM:B_VDIM + 1, :]).astype(BF16)


def _flash(qt, k, vt, tq, nh):
    batch, heads, _, seq = qt.shape
    tk = tq // 2
    nq = seq // tq
    return pl.pallas_call(
        functools.partial(_flash_kernel, tq=tq, nh=nh),
        grid=(batch, heads // nh, nq),
        in_specs=[pl.BlockSpec((1, nh, QK_PAD, tq), lambda b, g, i: (b, g, 0, i)),
                  pl.BlockSpec((1, nh, seq, QK_PAD), lambda b, g, i: (b, g, 0, 0)),
                  pl.BlockSpec((1, nh, seq // tk, V_ROWS, tk), lambda b, g, i: (b, g, 0, 0, 0))],
        out_specs=pl.BlockSpec((nh * B_VDIM, tq), lambda b, g, i: (g, b * nq + i)),
        out_shape=jax.ShapeDtypeStruct((B_WIDTH, batch * seq), BF16),
        scratch_shapes=[pltpu.VMEM((nh, 1, tq), F32), pltpu.VMEM((nh, V_ROWS, tq), F32),
                        pltpu.VMEM((nh, tk, tq), F32), pltpu.VMEM((nh, tk, tq), F32),
                        pltpu.VMEM((nh, 1, tq), F32), pltpu.VMEM((nh, 1, tq), F32)],
        compiler_params=pltpu.CompilerParams(
            dimension_semantics=("arbitrary", "arbitrary", "arbitrary"),
            vmem_limit_bytes=VMEM_LIMIT),
        name="flash",
    )(qt, k, vt)


def _flash_bounded_kernel(kmax_ref, qt_ref, qtn_ref, k_ref, vt_ref, o_ref, mq_sc, mqn_sc, acc_sc, p0_sc, p1_sc,
                          *, tq, nh):
    tk = tq // 2
    i = pl.program_id(2)
    acc_sc[...] = jnp.zeros(acc_sc.shape, F32)

    def shift(q_ref, h):
        q = q_ref[0, h].astype(F32)
        return jnp.sqrt(jnp.sum(q * q, axis=0, keepdims=True)) * kmax_ref[0]

    @pl.when(i > 0)
    def _():
        mq_sc[...] = mqn_sc[...]

    p_bufs = (p0_sc, p1_sc)

    def probs(h, j, slot, diag_half, next_q=False):
        kb = k_ref[0, h, pl.ds(pl.multiple_of(j * tk, tk), tk), :]
        if diag_half == 1:
            s = _dot(kb, qt_ref[0, h, :, tk:])
            key = lax.broadcasted_iota(jnp.int32, (tk, tk), 0)
            qry = lax.broadcasted_iota(jnp.int32, (tk, tk), 1)
            s = jnp.where(key <= qry, s, NEG)
            p_bufs[slot][h, :, tk:] = jnp.exp2(s - mq_sc[h, :, tk:]).astype(BF16)
            return
        s = _dot(kb, (qtn_ref if next_q else qt_ref)[0, h])
        if diag_half is not None:
            key = lax.broadcasted_iota(jnp.int32, (tk, tq), 0)
            qry = lax.broadcasted_iota(jnp.int32, (tk, tq), 1)
            s = jnp.where(key <= qry, s, NEG)
        p_bufs[slot][h] = jnp.exp2(s - (mqn_sc if next_q else mq_sc)[h]).astype(BF16)

    def pv(h, j, slot, upper_half_only=False):
        if upper_half_only:
            acc_sc[h, :, tk:] += _dot(vt_ref[0, h, j], p_bufs[slot][h, :, tk:])
        else:
            acc_sc[h] += _dot(vt_ref[0, h, j], p_bufs[slot][h])

    def stage(j, slot, next_diag_half, next_q=False, upper_half_only=False):
        for h in range(nh):
            probs(h, 0 if next_q else j + 1, 1 - slot, next_diag_half, next_q)
            pv(h, j, slot, upper_half_only)

    def pair(jj, last_next_diag_half=None):
        stage(2 * jj, 0, None)
        stage(2 * jj + 1, 1, last_next_diag_half)

    @pl.when(i == 0)
    def _():
        for h in range(nh):
            mq_sc[h] = shift(qt_ref, h)
        for h in range(nh):
            probs(h, 0, 0, 0)

    plain = i - 1

    def two_pairs(n, carry):
        pair(2 * n)
        pair(2 * n + 1)
        return carry

    lax.fori_loop(0, plain // 2, two_pairs, 0)

    @pl.when(jnp.logical_and(plain > 0, plain % 2 == 1))
    def _():
        pair(plain - 1)

    @pl.when(i > 0)
    def _():
        pair(i - 1, 0)

    for h in range(nh):
        mqn_sc[h] = shift(qtn_ref, h)
    stage(2 * i, 0, 1)
    stage(2 * i + 1, 1, None, next_q=True, upper_half_only=True)

    for h in range(nh):
        acc = acc_sc[h]
        o_ref[h * B_VDIM:(h + 1) * B_VDIM, :] = (acc[:B_VDIM, :] / acc[B_VDIM:B_VDIM + 1, :]).astype(BF16)


def _flash_bounded(kmax, qt, k, vt, tq, nh):
    batch, heads, _, seq = qt.shape
    tk = tq // 2
    nq = seq // tq
    return pl.pallas_call(
        functools.partial(_flash_bounded_kernel, tq=tq, nh=nh),
        grid=(batch, heads // nh, nq),
        in_specs=[pl.BlockSpec(memory_space=pltpu.SMEM),
                  pl.BlockSpec((1, nh, QK_PAD, tq), lambda b, g, i: (b, g, 0, i)),
                  pl.BlockSpec((1, nh, QK_PAD, tq), lambda b, g, i: (b, g, 0, jnp.minimum(i + 1, nq - 1))),
                  pl.BlockSpec((1, nh, seq, QK_PAD), lambda b, g, i: (b, g, 0, 0)),
                  pl.BlockSpec((1, nh, seq // tk, V_ROWS, tk), lambda b, g, i: (b, g, 0, 0, 0))],
        out_specs=pl.BlockSpec((nh * B_VDIM, tq), lambda b, g, i: (g, b * nq + i)),
        out_shape=jax.ShapeDtypeStruct((B_WIDTH, batch * seq), BF16),
        scratch_shapes=[pltpu.VMEM((nh, 1, tq), F32), pltpu.VMEM((nh, 1, tq), F32),
                        pltpu.VMEM((nh, V_ROWS, tq), F32),
                        pltpu.VMEM((nh, tk, tq), BF16), pltpu.VMEM((nh, tk, tq), BF16)],
        compiler_params=pltpu.CompilerParams(
            dimension_semantics=("arbitrary", "arbitrary", "arbitrary"),
            vmem_limit_bytes=VMEM_LIMIT),
        name="flash_bounded",
    )(kmax, qt, qt, k, vt)


def _final_kernel(x_ref, p_ref, oat_ref, za_ref, obt_ref, zb_ref, ga_ref, gb_ref,
                  woa_ref, wob_ref, wout_ref, gpl_ref, wpg_ref, wpp_ref, gpost_ref, out_ref):
    rows = x_ref.shape[0] // FINAL_CHAINS
    groups = [slice(c * rows, (c + 1) * rows) for c in range(FINAL_CHAINS)]
    oa = [oat_ref[:, r].astype(F32).T for r in groups]
    ya = [_dot((oa[c] * jax.nn.silu(za_ref[r, :].astype(F32))).astype(BF16), woa_ref[...])
          for c, r in enumerate(groups)]
    yb = [_dot((obt_ref[:, r].astype(F32).T * jax.nn.silu(zb_ref[r, :].astype(F32))).astype(BF16), wob_ref[...])
          for r in groups]
    emb = [_row_rms(_dot(p_ref[r, :].astype(BF16), wpp_ref[...]), D_MODEL) * gpost_ref[...] for r in groups]
    merged = [jax.nn.sigmoid(ga_ref[r, :].astype(F32)) * ya[c] + jax.nn.sigmoid(gb_ref[r, :].astype(F32)) * yb[c]
              for c, r in enumerate(groups)]
    x1 = [x_ref[r, :] + _dot(merged[c].astype(BF16), wout_ref[...]) for c, r in enumerate(groups)]
    gate = [jax.nn.sigmoid(_dot((_row_rms(x1[c], D_MODEL) * gpl_ref[...]).astype(BF16), wpg_ref[...]))
            for c in range(FINAL_CHAINS)]
    for c, r in enumerate(groups):
        out_ref[r, :] = x1[c] + gate[c] * emb[c]


def _final(x2, p2, oat, za, ob, zb, ga, gb, woa, wob, wout, gpl, wpg, wpp, gpost, tm):
    t = x2.shape[0]
    row = lambda a: pl.BlockSpec((tm, a.shape[1]), lambda i: (i, 0))
    full = lambda a: pl.BlockSpec(a.shape, lambda i: (0,) * a.ndim)
    acts = (x2, p2, oat, za, ob, zb, ga, gb)
    consts = (woa, wob, wout, gpl, wpg, wpp, gpost)
    act_specs = [row(a) for a in acts]
    act_specs[2] = pl.BlockSpec((A_WIDTH, tm), lambda i: (0, i))
    act_specs[4] = pl.BlockSpec((B_WIDTH, tm), lambda i: (0, i))
    return pl.pallas_call(
        _final_kernel,
        grid=(t // tm,),
        in_specs=act_specs + [full(a) for a in consts],
        out_specs=pl.BlockSpec((tm, D_MODEL), lambda i: (i, 0)),
        out_shape=jax.ShapeDtypeStruct((t, D_MODEL), F32),
        compiler_params=pltpu.CompilerParams(
            dimension_semantics=("arbitrary",), vmem_limit_bytes=VMEM_LIMIT),
        name="final",
    )(*acts, *consts)


def kernel(x, p, positions, norm_g, w_in, a_q_norm, a_k_norm, a_sinks, rel_bias, w_o_a, b_cq_norm, w_uq, b_ckv_norm, w_uk, w_uv, b_q_norm, b_k_norm, b_kr_norm, w_o_b, w_out, ple_norm_g, w_ple_gate, w_ple_proj, ple_post_g):
    batch, seq, _ = x.shape
    depth = p.shape[0]
    t = batch * seq
    tm = TOKEN_TILE
    offs = np.concatenate([[0], np.cumsum(SPLIT_SIZES)])
    log2e = math.log2(math.e)
    inv_freq = (ROPE_THETA ** (-jnp.arange(0, B_ROPE, 2, dtype=F32) / B_ROPE))[:, None]
    pos_row = positions.reshape(1, t)
    band_bias = _band_bias(rel_bias)
    row = lambda v: v.astype(F32)[None, :]
    col = lambda v: v.astype(F32)[:, None]

    x2 = x.reshape(t, D_MODEL)
    for i in range(depth):
        w = w_in[i]
        cols = [w[:, offs[j]:offs[j + 1]] for j in range(len(SPLIT_SIZES))]
        wt = jnp.concatenate([cols[0], cols[2], cols[1], cols[6]], axis=1).T.astype(BF16)
        w1 = jnp.concatenate([cols[j] for j in (3, 4, 5, 7, 8, 9)], axis=1).astype(BF16)
        gqn = col(b_q_norm[i]) * (B_QK_DIM ** -0.5 * log2e)
        consts = (row(norm_g[i]), wt, w1,
                  col(a_q_norm[i]) * (A_HEAD_DIM ** -0.5 * log2e), col(a_k_norm[i]),
                  row(b_cq_norm[i]), w_uq[i].T.astype(BF16), gqn,
                  row(b_ckv_norm[i]), w_uk[i].T.astype(BF16), w_uv[i].T.astype(BF16),
                  col(b_k_norm[i]), col(b_kr_norm[i]), inv_freq)
        qvt, ka, za, zb, ga, gb, qt, kb, vt = _proj(x2, pos_row, consts, batch, seq, tm, FLASH_Q_TILE // 2)

        sinks = jnp.repeat(a_sinks[i].astype(F32) * log2e, A_BLOCK).reshape(A_KV_HEADS, 1, A_GROUP * A_BLOCK)
        oat = _swa(qvt, ka.reshape(batch, seq, A_KV_WIDTH), band_bias, sinks, batch, seq, tm)
        q_bound = math.sqrt(B_QK_DIM) * jnp.max(jnp.abs(gqn))
        k_bound = jnp.sqrt(B_NOPE * jnp.max(jnp.square(b_k_norm[i].astype(F32)))
                           + B_ROPE * jnp.max(jnp.square(b_kr_norm[i].astype(F32))))
        ob = lax.cond(q_bound * k_bound < MAX_SAFE_SHIFT,
                      lambda: _flash_bounded(k_bound.reshape(1), qt, kb, vt, FLASH_Q_TILE, FLASH_HEADS),
                      lambda: _flash(qt, kb, vt, FLASH_Q_TILE, FLASH_HEADS))

        x2 = _final(x2, p[i].reshape(t, PLE_DIM), oat, za,
                    ob, zb, ga, gb,
                    w_o_a[i].astype(BF16), w_o_b[i].astype(BF16), w_out[i].astype(BF16),
                    row(ple_norm_g[i]), w_ple_gate[i].astype(BF16), w_ple_proj[i].astype(BF16),
                    row(ple_post_g[i]), tm)
    return x2.reshape(batch, seq, D_MODEL)
```

```python
import functools
import math

import numpy as np
import jax
import jax.numpy as jnp
from jax import lax
from jax.experimental import pallas as pl
from jax.experimental.pallas import tpu as pltpu

F32 = jnp.float32
BF16 = jnp.bfloat16

D_MODEL = 1024
PLE_DIM = 256
EPS = 1e-6
NEG = -1e30

A_HEADS = 16
A_KV_HEADS = 2
A_HEAD_DIM = 64
A_WIDTH = A_HEADS * A_HEAD_DIM
A_KV_WIDTH = A_KV_HEADS * A_HEAD_DIM
A_GROUP = A_HEADS // A_KV_HEADS
WINDOW = 128
A_BLOCK = 128
N_BUCKETS = 32
MAX_DISTANCE = 128

B_HEADS = 16
B_Q_RANK = 256
B_KV_RANK = 128
B_NOPE = 64
B_ROPE = 32
B_QK_DIM = B_NOPE + B_ROPE
B_VDIM = 64
B_WIDTH = B_HEADS * B_VDIM
ROPE_THETA = 10000.0

SPLIT_SIZES = (A_WIDTH, A_KV_WIDTH, A_KV_WIDTH, A_WIDTH,
               B_Q_RANK, B_KV_RANK, B_ROPE, B_WIDTH,
               D_MODEL, D_MODEL)

LANES = 128
QK_PAD = 128
V_ROWS = 80
VMEM_LIMIT = 56 * 1024 * 1024
MAX_SAFE_SHIFT = 60.0

TOKEN_TILE = 512
FLASH_Q_TILE = 512
FLASH_HEADS = 4
FINAL_CHAINS = 4
PROJ_GROUPS = 2
N_PROJ_CONSTS = 14

QVT_ROWS = A_WIDTH + A_KV_WIDTH
R_KA = QVT_ROWS
R_KR = R_KA + A_KV_WIDTH
WT_ROWS = R_KR + B_ROPE
SWA_V_ROWS = A_HEAD_DIM + 16
C_ZA = 0
C_LAT = C_ZA + A_WIDTH
LAT_W = B_Q_RANK + B_KV_RANK
C_ZB = C_LAT + LAT_W
C_GA = C_ZB + B_WIDTH
C_GB = C_GA + D_MODEL
C_END = C_GB + D_MODEL


def _dot(a, b):
    return jnp.dot(a, b, preferred_element_type=F32)


def _dot_nt(a, b):
    return lax.dot_general(a, b, (((1,), (1,)), ((), ())), preferred_element_type=F32)


def _row_rms(v, width):
    return v * lax.rsqrt(jnp.sum(v * v, axis=-1, keepdims=True) * (1.0 / width) + EPS)


def _col_rms(v):
    return v * lax.rsqrt(jnp.sum(v * v, axis=0, keepdims=True) * (1.0 / v.shape[0]) + EPS)


def _proj_kernel(x_ref, pos_ref, *refs):
    consts, outs = refs[:N_PROJ_CONSTS], refs[N_PROJ_CONSTS:]
    qvt_ref, ka_ref, za_ref, zb_ref, ga_ref, gb_ref, qt_ref, k_ref, vt_ref = outs
    rows = x_ref.shape[0] // PROJ_GROUPS
    chunks = vt_ref.shape[2] // PROJ_GROUPS
    for gi in range(PROJ_GROUPS):
        ts = pl.ds(gi * rows, rows)
        _proj_rows(x_ref.at[ts, :], pos_ref.at[:, ts], *consts,
                   qvt_ref.at[:, ts], ka_ref.at[ts, :], za_ref.at[ts, :], zb_ref.at[ts, :], ga_ref.at[ts, :],
                   gb_ref.at[ts, :], qt_ref.at[:, :, :, ts], k_ref.at[:, :, ts, :],
                   vt_ref.at[:, :, pl.ds(gi * chunks, chunks)])


def _proj_rows(x_ref, pos_ref, g_ref, wt_ref, w_ref, gq_ref, gk_ref,
               gcq_ref, wuqt_ref, gqn_ref, gckv_ref, wukt_ref, wuvt_ref, gkn_ref, gkr_ref, freq_ref,
               qvt_ref, ka_ref, za_ref, zb_ref, ga_ref, gb_ref, qt_ref, k_ref, vt_ref):
    x = x_ref[...]
    tm = x.shape[0]
    h = (_row_rms(x, D_MODEL) * g_ref[...]).astype(BF16)

    wt_out = _dot_nt(wt_ref[...], h)
    gq = gq_ref[...]
    for hd in range(A_HEADS):
        blk = wt_out[hd * A_HEAD_DIM:(hd + 1) * A_HEAD_DIM, :]
        qvt_ref[hd * A_HEAD_DIM:(hd + 1) * A_HEAD_DIM, :] = (_col_rms(blk) * gq).astype(BF16)
    qvt_ref[A_WIDTH:, :] = wt_out[A_WIDTH:QVT_ROWS, :].astype(BF16)
    gk = gk_ref[...]
    kat = [_col_rms(wt_out[R_KA + g * A_HEAD_DIM:R_KA + (g + 1) * A_HEAD_DIM, :]) * gk
           for g in range(A_KV_HEADS)]
    ka_ref[...] = jnp.concatenate(kat, axis=0).T.astype(BF16)
    krt = wt_out[R_KR:, :]

    def proj(c0, c1):
        return _dot(h, w_ref[:, c0:c1])

    za_ref[...] = proj(C_ZA, C_LAT).astype(BF16)
    lat = proj(C_LAT, C_ZB)
    zb_ref[...] = proj(C_ZB, C_GA).astype(BF16)
    ga_ref[...] = proj(C_GA, C_GB).astype(BF16)
    gb_ref[...] = proj(C_GB, C_END).astype(BF16)

    cq = (_row_rms(lat[:, :B_Q_RANK], B_Q_RANK) * gcq_ref[...]).astype(BF16)
    qt = _dot_nt(wuqt_ref[...], cq)
    ang = freq_ref[...] * pos_ref[...].astype(F32)
    cos = jnp.cos(ang)
    sin = jnp.sin(ang)
    gqn = gqn_ref[...]
    half = B_ROPE // 2
    zeros_q = jnp.zeros((QK_PAD - B_QK_DIM, tm), BF16)
    for hd in range(B_HEADS):
        qn = _col_rms(qt[hd * B_QK_DIM:(hd + 1) * B_QK_DIM, :]) * gqn
        x1 = qn[B_NOPE:B_NOPE + half, :]
        x2 = qn[B_NOPE + half:, :]
        qt_ref[0, hd, 0:B_NOPE, :] = qn[:B_NOPE, :].astype(BF16)
        qt_ref[0, hd, B_NOPE:B_NOPE + half, :] = (x1 * cos - x2 * sin).astype(BF16)
        qt_ref[0, hd, B_NOPE + half:B_QK_DIM, :] = (x2 * cos + x1 * sin).astype(BF16)
        qt_ref[0, hd, B_QK_DIM:, :] = zeros_q

    ckv = (_row_rms(lat[:, B_Q_RANK:], B_KV_RANK) * gckv_ref[...]).astype(BF16)
    knt = _dot_nt(wukt_ref[...], ckv)
    vt = _dot_nt(wuvt_ref[...], ckv)

    krn = _col_rms(krt) * gkr_ref[...]
    x1 = krn[:half, :]
    x2 = krn[half:, :]
    k_tail = jnp.concatenate([x1 * cos - x2 * sin, x2 * cos + x1 * sin,
                              jnp.zeros((QK_PAD - B_QK_DIM, tm), F32)], axis=0)
    gkn = gkn_ref[...]
    for hd in range(B_HEADS):
        kn = _col_rms(knt[hd * B_NOPE:(hd + 1) * B_NOPE, :]) * gkn
        k_ref[0, hd] = jnp.concatenate([kn, k_tail], axis=0).T.astype(BF16)
    tk = vt_ref.shape[-1]
    ones_rows = (lax.broadcasted_iota(jnp.int32, (V_ROWS - B_VDIM, tk), 0) == 0).astype(BF16)
    for hd in range(B_HEADS):
        for c in range(tm // tk):
            vt_ref[0, hd, c, 0:B_VDIM, :] = vt[hd * B_VDIM:(hd + 1) * B_VDIM, c * tk:(c + 1) * tk].astype(BF16)
            vt_ref[0, hd, c, B_VDIM:, :] = ones_rows


def _proj(x2, pos_row, consts, batch, seq, tm, tk):
    t = x2.shape[0]
    nt = seq // tm
    nc = tm // tk
    row = lambda w: pl.BlockSpec((tm, w), lambda i: (i, 0))
    full = lambda a: pl.BlockSpec(a.shape, lambda i: (0,) * a.ndim)
    widths = (A_KV_WIDTH, A_WIDTH, B_WIDTH, D_MODEL, D_MODEL)
    return pl.pallas_call(
        _proj_kernel,
        grid=(t // tm,),
        in_specs=[row(D_MODEL), pl.BlockSpec((1, tm), lambda i: (0, i))] + [full(a) for a in consts],
        out_specs=[pl.BlockSpec((QVT_ROWS, tm), lambda i: (0, i))] + [row(w) for w in widths]
        + [pl.BlockSpec((1, B_HEADS, QK_PAD, tm), lambda i: (i // nt, 0, 0, i % nt)),
           pl.BlockSpec((1, B_HEADS, tm, QK_PAD), lambda i: (i // nt, 0, i % nt, 0)),
           pl.BlockSpec((1, B_HEADS, nc, V_ROWS, tk), lambda i: (i // nt, 0, i % nt, 0, 0))],
        out_shape=[jax.ShapeDtypeStruct((QVT_ROWS, t), BF16)]
        + [jax.ShapeDtypeStruct((t, w), BF16) for w in widths]
        + [jax.ShapeDtypeStruct((batch, B_HEADS, QK_PAD, seq), BF16),
           jax.ShapeDtypeStruct((batch, B_HEADS, seq, QK_PAD), BF16),
           jax.ShapeDtypeStruct((batch, B_HEADS, nt * nc, V_ROWS, tk), BF16)],
        compiler_params=pltpu.CompilerParams(
            dimension_semantics=("arbitrary",), vmem_limit_bytes=VMEM_LIMIT),
        name="proj",
    )(x2, pos_row, *consts)


def _band_bias_kernel(rbt_ref, onehot_ref, o_ref):
    rbt = rbt_ref[...]
    onehot = onehot_ref[...]
    t = jnp.zeros((A_HEADS, WINDOW), F32)
    for b in range(N_BUCKETS):
        t = t + rbt[:, b:b + 1] * onehot[b:b + 1, :]
    t = t * math.log2(math.e)
    band = 2 * A_BLOCK
    key = lax.broadcasted_iota(jnp.int32, (band, band), 0)
    neg = jnp.full((1, A_BLOCK), NEG, F32)
    for hd in range(A_HEADS):
        x = jnp.broadcast_to(jnp.concatenate([neg, t[hd:hd + 1, :]], axis=1), (band, band))
        for bit in range(8):
            x = jnp.where((key >> bit) & 1 == 1, pltpu.roll(x, 1 << bit, 1), x)
        g, hh = divmod(hd, A_GROUP)
        o_ref[g, :, hh * A_BLOCK:(hh + 1) * A_BLOCK] = x[:, :A_BLOCK]


def _band_bias(rel_bias):
    dist = np.arange(WINDOW)
    max_exact = N_BUCKETS // 2
    large = max_exact + (np.log(np.maximum(dist, 1).astype(np.float32) / max_exact)
                         / math.log(MAX_DISTANCE / max_exact) * (N_BUCKETS - max_exact)).astype(np.int32)
    bucket = np.where(dist < max_exact, dist, np.minimum(large, N_BUCKETS - 1))
    onehot = jnp.asarray(np.arange(N_BUCKETS)[:, None] == bucket[None, :], F32)
    return pl.pallas_call(
        _band_bias_kernel,
        out_shape=jax.ShapeDtypeStruct((A_KV_HEADS, 2 * A_BLOCK, A_GROUP * A_BLOCK), F32),
        name="band_bias",
    )(rel_bias.astype(F32).T, onehot)


def _swa_kernel(qv_ref, k_ref, kp_ref, vp_ref, bias_ref, sink_ref, o_ref, *, nsub):
    first = pl.program_id(1) == 0
    band = 2 * A_BLOCK
    width = A_GROUP * A_BLOCK
    pad = jnp.logical_and(first, lax.broadcasted_iota(jnp.int32, (band, width), 0) < A_BLOCK)
    zeros_q = jnp.zeros((A_HEAD_DIM, width), BF16)
    ones_rows = (lax.broadcasted_iota(jnp.int32, (SWA_V_ROWS - A_HEAD_DIM, band), 0) == 0).astype(BF16)

    def scores(sb, g):
        r0 = sb * A_BLOCK
        if sb == 0:
            kb = jnp.concatenate([kp_ref[0], k_ref[0, 0:A_BLOCK, :]], axis=0)
        else:
            kb = k_ref[0, r0 - A_BLOCK:r0 + A_BLOCK, :]
        tiles = [qv_ref[(g * A_GROUP + hh) * A_HEAD_DIM:(g * A_GROUP + hh + 1) * A_HEAD_DIM, r0:r0 + A_BLOCK]
                 for hh in range(A_GROUP)]
        qt = jnp.concatenate(tiles, axis=1)
        rhs = jnp.concatenate([qt, zeros_q] if g == 0 else [zeros_q, qt], axis=0)
        s = _dot(kb, rhs) + bias_ref[g]
        if sb == 0:
            s = jnp.where(pad, NEG, s)
        return s

    def finish(sb, g, s):
        r0 = sb * A_BLOCK
        sink = sink_ref[g]
        m = jnp.maximum(jnp.max(s, axis=0, keepdims=True), sink)
        e = jnp.exp2(s - m).astype(BF16)
        v0 = A_WIDTH + g * A_HEAD_DIM
        if sb == 0:
            vt = jnp.concatenate([vp_ref[g * A_HEAD_DIM:(g + 1) * A_HEAD_DIM, :],
                                  qv_ref[v0:v0 + A_HEAD_DIM, 0:A_BLOCK]], axis=1)
        else:
            vt = qv_ref[v0:v0 + A_HEAD_DIM, r0 - A_BLOCK:r0 + A_BLOCK]
        acc = _dot(jnp.concatenate([vt, ones_rows], axis=0), e)
        den = acc[A_HEAD_DIM:A_HEAD_DIM + 1, :] + jnp.exp2(sink - m)
        o = (acc[:A_HEAD_DIM, :] / den).astype(BF16)
        for hh in range(A_GROUP):
            hd = g * A_GROUP + hh
            o_ref[hd * A_HEAD_DIM:(hd + 1) * A_HEAD_DIM, r0:r0 + A_BLOCK] = o[:, hh * A_BLOCK:(hh + 1) * A_BLOCK]

    units = [(sb, g) for sb in range(nsub) for g in range(A_KV_HEADS)]
    s_next = scores(*units[0])
    for n, unit in enumerate(units):
        s_cur = s_next
        if n + 1 < len(units):
            s_next = scores(*units[n + 1])
        finish(*unit, s_cur)


def _swa(qvt, ka, bias, sinks, batch, seq, tq):
    nsub = tq // A_BLOCK
    nt = seq // tq
    vrow = A_WIDTH // A_KV_WIDTH
    return pl.pallas_call(
        functools.partial(_swa_kernel, nsub=nsub),
        grid=(batch, nt),
        in_specs=[pl.BlockSpec((QVT_ROWS, tq), lambda b, i: (0, b * nt + i)),
                  pl.BlockSpec((1, tq, A_KV_WIDTH), lambda b, i: (b, i, 0)),
                  pl.BlockSpec((1, A_BLOCK, A_KV_WIDTH), lambda b, i: (b, jnp.maximum(i * nsub - 1, 0), 0)),
                  pl.BlockSpec((A_KV_WIDTH, A_BLOCK),
                               lambda b, i: (vrow, jnp.maximum((b * nt + i) * nsub - 1, 0))),
                  pl.BlockSpec(bias.shape, lambda b, i: (0, 0, 0)),
                  pl.BlockSpec(sinks.shape, lambda b, i: (0, 0, 0))],
        out_specs=pl.BlockSpec((A_WIDTH, tq), lambda b, i: (0, b * nt + i)),
        out_shape=jax.ShapeDtypeStruct((A_WIDTH, batch * seq), BF16),
        compiler_params=pltpu.CompilerParams(
            dimension_semantics=("arbitrary", "arbitrary"), vmem_limit_bytes=VMEM_LIMIT),
        name="swa",
    )(qvt, ka, ka, qvt, bias, sinks)


def _flash_kernel(qt_ref, k_ref, vt_ref, o_ref, m_sc, acc_sc, s0_sc, s1_sc, bm0_sc, bm1_sc, *, tq, nh):
    tk = tq // 2
    i = pl.program_id(2)
    m_sc[...] = jnp.full(m_sc.shape, NEG, F32)
    acc_sc[...] = jnp.zeros(acc_sc.shape, F32)
    s_bufs = (s0_sc, s1_sc)
    bm_bufs = (bm0_sc, bm1_sc)

    def scores(h, j, slot, diag_half):
        kb = k_ref[0, h, pl.ds(pl.multiple_of(j * tk, tk), tk), :]
        s = _dot(kb, qt_ref[0, h])
        if diag_half is not None:
            key = lax.broadcasted_iota(jnp.int32, (tk, tq), 0) + diag_half * tk
            qry = lax.broadcasted_iota(jnp.int32, (tk, tq), 1)
            s = jnp.where(key <= qry, s, NEG)
        s_bufs[slot][h] = s
        bm_bufs[slot][h] = jnp.max(s, axis=0, keepdims=True)

    def softmax_pv(h, j, slot):
        m_old = m_sc[h]
        m_new = jnp.maximum(m_old, bm_bufs[slot][h])
        p = jnp.exp2(s_bufs[slot][h] - m_new).astype(BF16)
        alpha = jnp.exp2(m_old - m_new)
        m_sc[h] = m_new
        acc_sc[h] = acc_sc[h] * alpha + _dot(vt_ref[0, h, j], p)

    def stage(j, slot, next_diag_half, has_next=True):
        for h in range(nh):
            if has_next:
                scores(h, j + 1, 1 - slot, next_diag_half)
            softmax_pv(h, j, slot)

    @pl.when(i == 0)
    def _():
        for h in range(nh):
            scores(h, 0, 0, 0)

    @pl.when(i > 0)
    def _():
        for h in range(nh):
            scores(h, 0, 0, None)

    def pair(jj, carry):
        stage(2 * jj, 0, None)
        stage(2 * jj + 1, 1, None)
        return carry

    lax.fori_loop(0, i - 1, pair, 0)

    @pl.when(i > 0)
    def _():
        stage(2 * i - 2, 0, None)
        stage(2 * i - 1, 1, 0)

    stage(2 * i, 0, 1)
    stage(2 * i + 1, 1, None, has_next=False)

    for h in range(nh):
        acc = acc_sc[h]
        o_ref[h * B_VDIM:(h + 1) * B_VDIM, :] = (acc[:B_VDIM, :] / acc[B_VDIM:B_VDIM + 1, :]).astype(BF16)


def _flash(qt, k, vt, tq, nh):
    batch, heads, _, seq = qt.shape
    tk = tq // 2
    nq = seq // tq
    return pl.pallas_call(
        functools.partial(_flash_kernel, tq=tq, nh=nh),
        grid=(batch, heads // nh, nq),
        in_specs=[pl.BlockSpec((1, nh, QK_PAD, tq), lambda b, g, i: (b, g, 0, i)),
                  pl.BlockSpec((1, nh, seq, QK_PAD), lambda b, g, i: (b, g, 0, 0)),
                  pl.BlockSpec((1, nh, seq // tk, V_ROWS, tk), lambda b, g, i: (b, g, 0, 0, 0))],
        out_specs=pl.BlockSpec((nh * B_VDIM, tq), lambda b, g, i: (g, b * nq + i)),
        out_shape=jax.ShapeDtypeStruct((B_WIDTH, batch * seq), BF16),
        scratch_shapes=[pltpu.VMEM((nh, 1, tq), F32), pltpu.VMEM((nh, V_ROWS, tq), F32),
                        pltpu.VMEM((nh, tk, tq), F32), pltpu.VMEM((nh, tk, tq), F32),
                        pltpu.VMEM((nh, 1, tq), F32), pltpu.VMEM((nh, 1, tq), F32)],
        compiler_params=pltpu.CompilerParams(
            dimension_semantics=("arbitrary", "arbitrary", "arbitrary"),
            vmem_limit_bytes=VMEM_LIMIT),
        name="flash",
    )(qt, k, vt)


def _flash_bounded_kernel(kmax_ref, qt_ref, qtn_ref, k_ref, vt_ref, o_ref, mq_sc, mqn_sc, acc_sc, p0_sc, p1_sc,
                          *, tq, nh):
    tk = tq // 2
    i = pl.program_id(2)
    acc_sc[...] = jnp.zeros(acc_sc.shape, F32)

    def shift(q_ref, h):
        q = q_ref[0, h].astype(F32)
        return jnp.sqrt(jnp.sum(q * q, axis=0, keepdims=True)) * kmax_ref[0]

    @pl.when(i > 0)
    def _():
        mq_sc[...] = mqn_sc[...]

    p_bufs = (p0_sc, p1_sc)

    def probs(h, j, slot, diag_half, next_q=False):
        kb = k_ref[0, h, pl.ds(pl.multiple_of(j * tk, tk), tk), :]
        if diag_half == 1:
            s = _dot(kb, qt_ref[0, h, :, tk:])
            key = lax.broadcasted_iota(jnp.int32, (tk, tk), 0)
            qry = lax.broadcasted_iota(jnp.int32, (tk, tk), 1)
            s = jnp.where(key <= qry, s, NEG)
            p_bufs[slot][h, :, tk:] = jnp.exp2(s - mq_sc[h, :, tk:]).astype(BF16)
            return
        s = _dot(kb, (qtn_ref if next_q else qt_ref)[0, h])
        if diag_half is not None:
            key = lax.broadcasted_iota(jnp.int32, (tk, tq), 0)
            qry = lax.broadcasted_iota(jnp.int32, (tk, tq), 1)
            s = jnp.where(key <= qry, s, NEG)
        p_bufs[slot][h] = jnp.exp2(s - (mqn_sc if next_q else mq_sc)[h]).astype(BF16)

    def pv(h, j, slot, upper_half_only=False):
        if upper_half_only:
            acc_sc[h, :, tk:] += _dot(vt_ref[0, h, j], p_bufs[slot][h, :, tk:])
        else:
            acc_sc[h] += _dot(vt_ref[0, h, j], p_bufs[slot][h])

    def stage(j, slot, next_diag_half, next_q=False, upper_half_only=False):
        for h in range(nh):
            probs(h, 0 if next_q else j + 1, 1 - slot, next_diag_half, next_q)
            pv(h, j, slot, upper_half_only)

    def pair(jj, last_next_diag_half=None):
        stage(2 * jj, 0, None)
        stage(2 * jj + 1, 1, last_next_diag_half)

    @pl.when(i == 0)
    def _():
        for h in range(nh):
            mq_sc[h] = shift(qt_ref, h)
        for h in range(nh):
            probs(h, 0, 0, 0)

    plain = i - 1

    def two_pairs(n, carry):
        pair(2 * n)
        pair(2 * n + 1)
        return carry

    lax.fori_loop(0, plain // 2, two_pairs, 0)

    @pl.when(jnp.logical_and(plain > 0, plain % 2 == 1))
    def _():
        pair(plain - 1)

    @pl.when(i > 0)
    def _():
        pair(i - 1, 0)

    for h in range(nh):
        mqn_sc[h] = shift(qtn_ref, h)
    stage(2 * i, 0, 1)
    stage(2 * i + 1, 1, None, next_q=True, upper_half_only=True)

    for h in range(nh):
        acc = acc_sc[h]
        o_ref[h * B_VDIM:(h + 1) * B_VDIM, :] = (acc[:B_VDIM, :] / acc[B_VDIM:B_VDIM + 1, :]).astype(BF16)


def _flash_bounded(kmax, qt, k, vt, tq, nh):
    batch, heads, _, seq = qt.shape
    tk = tq // 2
    nq = seq // tq
    return pl.pallas_call(
        functools.partial(_flash_bounded_kernel, tq=tq, nh=nh),
        grid=(batch, heads // nh, nq),
        in_specs=[pl.BlockSpec(memory_space=pltpu.SMEM),
                  pl.BlockSpec((1, nh, QK_PAD, tq), lambda b, g, i: (b, g, 0, i)),
                  pl.BlockSpec((1, nh, QK_PAD, tq), lambda b, g, i: (b, g, 0, jnp.minimum(i + 1, nq - 1))),
                  pl.BlockSpec((1, nh, seq, QK_PAD), lambda b, g, i: (b, g, 0, 0)),
                  pl.BlockSpec((1, nh, seq // tk, V_ROWS, tk), lambda b, g, i: (b, g, 0, 0, 0))],
        out_specs=pl.BlockSpec((nh * B_VDIM, tq), lambda b, g, i: (g, b * nq + i)),
        out_shape=jax.ShapeDtypeStruct((B_WIDTH, batch * seq), BF16),
        scratch_shapes=[pltpu.VMEM((nh, 1, tq), F32), pltpu.VMEM((nh, 1, tq), F32),
                        pltpu.VMEM((nh, V_ROWS, tq), F32),
                        pltpu.VMEM((nh, tk, tq), BF16), pltpu.VMEM((nh, tk, tq), BF16)],
        compiler_params=pltpu.CompilerParams(
            dimension_semantics=("arbitrary", "arbitrary", "arbitrary"),
            vmem_limit_bytes=VMEM_LIMIT),
        name="flash_bounded",
    )(kmax, qt, qt, k, vt)


def _final_kernel(x_ref, p_ref, oat_ref, za_ref, obt_ref, zb_ref, ga_ref, gb_ref,
                  woa_ref, wob_ref, wout_ref, gpl_ref, wpg_ref, wpp_ref, gpost_ref, out_ref):
    rows = x_ref.shape[0] // FINAL_CHAINS
    groups = [slice(c * rows, (c + 1) * rows) for c in range(FINAL_CHAINS)]
    oa = [oat_ref[:, r].astype(F32).T for r in groups]
    ya = [_dot((oa[c] * jax.nn.silu(za_ref[r, :].astype(F32))).astype(BF16), woa_ref[...])
          for c, r in enumerate(groups)]
    yb = [_dot((obt_ref[:, r].astype(F32).T * jax.nn.silu(zb_ref[r, :].astype(F32))).astype(BF16), wob_ref[...])
          for r in groups]
    emb = [_row_rms(_dot(p_ref[r, :].astype(BF16), wpp_ref[...]), D_MODEL) * gpost_ref[...] for r in groups]
    merged = [jax.nn.sigmoid(ga_ref[r, :].astype(F32)) * ya[c] + jax.nn.sigmoid(gb_ref[r, :].astype(F32)) * yb[c]
              for c, r in enumerate(groups)]
    x1 = [x_ref[r, :] + _dot(merged[c].astype(BF16), wout_ref[...]) for c, r in enumerate(groups)]
    gate = [jax.nn.sigmoid(_dot((_row_rms(x1[c], D_MODEL) * gpl_ref[...]).astype(BF16), wpg_ref[...]))
            for c in range(FINAL_CHAINS)]
    for c, r in enumerate(groups):
        out_ref[r, :] = x1[c] + gate[c] * emb[c]


def _final(x2, p2, oat, za, ob, zb, ga, gb, woa, wob, wout, gpl, wpg, wpp, gpost, tm):
    t = x2.shape[0]
    row = lambda a: pl.BlockSpec((tm, a.shape[1]), lambda i: (i, 0))
    full = lambda a: pl.BlockSpec(a.shape, lambda i: (0,) * a.ndim)
    acts = (x2, p2, oat, za, ob, zb, ga, gb)
    consts = (woa, wob, wout, gpl, wpg, wpp, gpost)
    act_specs = [row(a) for a in acts]
    act_specs[2] = pl.BlockSpec((A_WIDTH, tm), lambda i: (0, i))
    act_specs[4] = pl.BlockSpec((B_WIDTH, tm), lambda i: (0, i))
    return pl.pallas_call(
        _final_kernel,
        grid=(t // tm,),
        in_specs=act_specs + [full(a) for a in consts],
        out_specs=pl.BlockSpec((tm, D_MODEL), lambda i: (i, 0)),
        out_shape=jax.ShapeDtypeStruct((t, D_MODEL), F32),
        compiler_params=pltpu.CompilerParams(
            dimension_semantics=("arbitrary",), vmem_limit_bytes=VMEM_LIMIT),
        name="final",
    )(*acts, *consts)


def kernel(x, p, positions, norm_g, w_in, a_q_norm, a_k_norm, a_sinks, rel_bias, w_o_a, b_cq_norm, w_uq, b_ckv_norm, w_uk, w_uv, b_q_norm, b_k_norm, b_kr_norm, w_o_b, w_out, ple_norm_g, w_ple_gate, w_ple_proj, ple_post_g):
    batch, seq, _ = x.shape
    depth = p.shape[0]
    t = batch * seq
    tm = TOKEN_TILE
    offs = np.concatenate([[0], np.cumsum(SPLIT_SIZES)])
    log2e = math.log2(math.e)
    inv_freq = (ROPE_THETA ** (-jnp.arange(0, B_ROPE, 2, dtype=F32) / B_ROPE))[:, None]
    pos_row = positions.reshape(1, t)
    band_bias = _band_bias(rel_bias)
    row = lambda v: v.astype(F32)[None, :]
    col = lambda v: v.astype(F32)[:, None]

    x2 = x.reshape(t, D_MODEL)
    for i in range(depth):
        w = w_in[i]
        cols = [w[:, offs[j]:offs[j + 1]] for j in range(len(SPLIT_SIZES))]
        wt = jnp.concatenate([cols[0], cols[2], cols[1], cols[6]], axis=1).T.astype(BF16)
        w1 = jnp.concatenate([cols[j] for j in (3, 4, 5, 7, 8, 9)], axis=1).astype(BF16)
        gqn = col(b_q_norm[i]) * (B_QK_DIM ** -0.5 * log2e)
        consts = (row(norm_g[i]), wt, w1,
                  col(a_q_norm[i]) * (A_HEAD_DIM ** -0.5 * log2e), col(a_k_norm[i]),
                  row(b_cq_norm[i]), w_uq[i].T.astype(BF16), gqn,
                  row(b_ckv_norm[i]), w_uk[i].T.astype(BF16), w_uv[i].T.astype(BF16),
                  col(b_k_norm[i]), col(b_kr_norm[i]), inv_freq)
        qvt, ka, za, zb, ga, gb, qt, kb, vt = _proj(x2, pos_row, consts, batch, seq, tm, FLASH_Q_TILE // 2)

        sinks = jnp.repeat(a_sinks[i].astype(F32) * log2e, A_BLOCK).reshape(A_KV_HEADS, 1, A_GROUP * A_BLOCK)
        oat = _swa(qvt, ka.reshape(batch, seq, A_KV_WIDTH), band_bias, sinks, batch, seq, tm)
        q_bound = math.sqrt(B_QK_DIM) * jnp.max(jnp.abs(gqn))
        k_bound = jnp.sqrt(B_NOPE * jnp.max(jnp.square(b_k_norm[i].astype(F32)))
                           + B_ROPE * jnp.max(jnp.square(b_kr_norm[i].astype(F32))))
        ob = lax.cond(q_bound * k_bound < MAX_SAFE_SHIFT,
                      lambda: _flash_bounded(k_bound.reshape(1), qt, kb, vt, FLASH_Q_TILE, FLASH_HEADS),
                      lambda: _flash(qt, kb, vt, FLASH_Q_TILE, FLASH_HEADS))

        x2 = _final(x2, p[i].reshape(t, PLE_DIM), oat, za,
                    ob, zb, ga, gb,
                    w_o_a[i].astype(BF16), w_o_b[i].astype(BF16), w_out[i].astype(BF16),
                    row(ple_norm_g[i]), w_ple_gate[i].astype(BF16), w_ple_proj[i].astype(BF16),
                    row(ple_post_g[i]), tm)
    return x2.reshape(batch, seq, D_MODEL)
```

```python
import functools
import math

import numpy as np
import jax
import jax.numpy as jnp
from jax import lax
from jax.experimental import pallas as pl
from jax.experimental.pallas import tpu as pltpu

F32 = jnp.float32
BF16 = jnp.bfloat16

D_MODEL = 1024
PLE_DIM = 256
EPS = 1e-6
NEG = -1e30

A_HEADS = 16
A_KV_HEADS = 2
A_HEAD_DIM = 64
A_WIDTH = A_HEADS * A_HEAD_DIM
A_KV_WIDTH = A_KV_HEADS * A_HEAD_DIM
A_GROUP = A_HEADS // A_KV_HEADS
WINDOW = 128
A_BLOCK = 128
N_BUCKETS = 32
MAX_DISTANCE = 128

B_HEADS = 16
B_Q_RANK = 256
B_KV_RANK = 128
B_NOPE = 64
B_ROPE = 32
B_QK_DIM = B_NOPE + B_ROPE
B_VDIM = 64
B_WIDTH = B_HEADS * B_VDIM
ROPE_THETA = 10000.0

SPLIT_SIZES = (A_WIDTH, A_KV_WIDTH, A_KV_WIDTH, A_WIDTH,
               B_Q_RANK, B_KV_RANK, B_ROPE, B_WIDTH,
               D_MODEL, D_MODEL)

LANES = 128
QK_PAD = 128
V_ROWS = 80
VMEM_LIMIT = 56 * 1024 * 1024
MAX_SAFE_SHIFT = 60.0

TOKEN_TILE = 512
FLASH_Q_TILE = 512
FLASH_HEADS = 4
FINAL_CHAINS = 4
PROJ_GROUPS = 2
N_PROJ_CONSTS = 14

QVT_ROWS = A_WIDTH + A_KV_WIDTH
R_KA = QVT_ROWS
R_KR = R_KA + A_KV_WIDTH
WT_ROWS = R_KR + B_ROPE
SWA_V_ROWS = A_HEAD_DIM + 16
C_ZA = 0
C_LAT = C_ZA + A_WIDTH
LAT_W = B_Q_RANK + B_KV_RANK
C_ZB = C_LAT + LAT_W
C_GA = C_ZB + B_WIDTH
C_GB = C_GA + D_MODEL
C_END = C_GB + D_MODEL


def _dot(a, b):
    return jnp.dot(a, b, preferred_element_type=F32)


def _dot_nt(a, b):
    return lax.dot_general(a, b, (((1,), (1,)), ((), ())), preferred_element_type=F32)


def _row_rms(v, width):
    return v * lax.rsqrt(jnp.sum(v * v, axis=-1, keepdims=True) * (1.0 / width) + EPS)


def _col_rms(v):
    return v * lax.rsqrt(jnp.sum(v * v, axis=0, keepdims=True) * (1.0 / v.shape[0]) + EPS)


def _proj_kernel(x_ref, pos_ref, *refs):
    consts, outs = refs[:N_PROJ_CONSTS], refs[N_PROJ_CONSTS:]
    qvt_ref, ka_ref, za_ref, zb_ref, ga_ref, gb_ref, qt_ref, k_ref, vt_ref = outs
    rows = x_ref.shape[0] // PROJ_GROUPS
    chunks = vt_ref.shape[2] // PROJ_GROUPS
    for gi in range(PROJ_GROUPS):
        ts = pl.ds(gi * rows, rows)
        _proj_rows(x_ref.at[ts, :], pos_ref.at[:, ts], *consts,
                   qvt_ref.at[:, ts], ka_ref.at[ts, :], za_ref.at[ts, :], zb_ref.at[ts, :], ga_ref.at[ts, :],
                   gb_ref.at[ts, :], qt_ref.at[:, :, :, ts], k_ref.at[:, :, ts, :],
                   vt_ref.at[:, :, pl.ds(gi * chunks, chunks)])


def _proj_rows(x_ref, pos_ref, g_ref, wt_ref, w_ref, gq_ref, gk_ref,
               gcq_ref, wuqt_ref, gqn_ref, gckv_ref, wukt_ref, wuvt_ref, gkn_ref, gkr_ref, freq_ref,
               qvt_ref, ka_ref, za_ref, zb_ref, ga_ref, gb_ref, qt_ref, k_ref, vt_ref):
    x = x_ref[...]
    tm = x.shape[0]
    h = (_row_rms(x, D_MODEL) * g_ref[...]).astype(BF16)

    wt_out = _dot_nt(wt_ref[...], h)
    gq = gq_ref[...]
    for hd in range(A_HEADS):
        blk = wt_out[hd * A_HEAD_DIM:(hd + 1) * A_HEAD_DIM, :]
        qvt_ref[hd * A_HEAD_DIM:(hd + 1) * A_HEAD_DIM, :] = (_col_rms(blk) * gq).astype(BF16)
    qvt_ref[A_WIDTH:, :] = wt_out[A_WIDTH:QVT_ROWS, :].astype(BF16)
    gk = gk_ref[...]
    kat = [_col_rms(wt_out[R_KA + g * A_HEAD_DIM:R_KA + (g + 1) * A_HEAD_DIM, :]) * gk
           for g in range(A_KV_HEADS)]
    ka_ref[...] = jnp.concatenate(kat, axis=0).T.astype(BF16)
    krt = wt_out[R_KR:, :]

    def proj(c0, c1):
        return _dot(h, w_ref[:, c0:c1])

    lat = proj(C_LAT, C_ZB)

    cq = (_row_rms(lat[:, :B_Q_RANK], B_Q_RANK) * gcq_ref[...]).astype(BF16)
    qt = _dot_nt(wuqt_ref[...], cq)
    ang = freq_ref[...] * pos_ref[...].astype(F32)
    cos = jnp.cos(ang)
    sin = jnp.sin(ang)
    gqn = gqn_ref[...]
    half = B_ROPE // 2
    zeros_q = jnp.zeros((QK_PAD - B_QK_DIM, tm), BF16)
    for hd in range(B_HEADS):
        qn = _col_rms(qt[hd * B_QK_DIM:(hd + 1) * B_QK_DIM, :]) * gqn
        x1 = qn[B_NOPE:B_NOPE + half, :]
        x2 = qn[B_NOPE + half:, :]
        qt_ref[0, hd, 0:B_NOPE, :] = qn[:B_NOPE, :].astype(BF16)
        qt_ref[0, hd, B_NOPE:B_NOPE + half, :] = (x1 * cos - x2 * sin).astype(BF16)
        qt_ref[0, hd, B_NOPE + half:B_QK_DIM, :] = (x2 * cos + x1 * sin).astype(BF16)
        qt_ref[0, hd, B_QK_DIM:, :] = zeros_q

    ckv = (_row_rms(lat[:, B_Q_RANK:], B_KV_RANK) * gckv_ref[...]).astype(BF16)
    knt = _dot_nt(wukt_ref[...], ckv)
    vt = _dot_nt(wuvt_ref[...], ckv)

    krn = _col_rms(krt) * gkr_ref[...]
    x1 = krn[:half, :]
    x2 = krn[half:, :]
    k_tail = jnp.concatenate([x1 * cos - x2 * sin, x2 * cos + x1 * sin,
                              jnp.zeros((QK_PAD - B_QK_DIM, tm), F32)], axis=0)
    gkn = gkn_ref[...]
    for hd in range(B_HEADS):
        kn = _col_rms(knt[hd * B_NOPE:(hd + 1) * B_NOPE, :]) * gkn
        k_ref[0, hd] = jnp.concatenate([kn, k_tail], axis=0).T.astype(BF16)
    tk = vt_ref.shape[-1]
    ones_rows = (lax.broadcasted_iota(jnp.int32, (V_ROWS - B_VDIM, tk), 0) == 0).astype(BF16)
    for hd in range(B_HEADS):
        for c in range(tm // tk):
            vt_ref[0, hd, c, 0:B_VDIM, :] = vt[hd * B_VDIM:(hd + 1) * B_VDIM, c * tk:(c + 1) * tk].astype(BF16)
            vt_ref[0, hd, c, B_VDIM:, :] = ones_rows

    za_ref[...] = proj(C_ZA, C_LAT).astype(BF16)
    zb_ref[...] = proj(C_ZB, C_GA).astype(BF16)
    ga_ref[...] = proj(C_GA, C_GB).astype(BF16)
    gb_ref[...] = proj(C_GB, C_END).astype(BF16)


def _proj(x2, pos_row, consts, batch, seq, tm, tk):
    t = x2.shape[0]
    nt = seq // tm
    nc = tm // tk
    row = lambda w: pl.BlockSpec((tm, w), lambda i: (i, 0))
    full = lambda a: pl.BlockSpec(a.shape, lambda i: (0,) * a.ndim)
    widths = (A_KV_WIDTH, A_WIDTH, B_WIDTH, D_MODEL, D_MODEL)
    return pl.pallas_call(
        _proj_kernel,
        grid=(t // tm,),
        in_specs=[row(D_MODEL), pl.BlockSpec((1, tm), lambda i: (0, i))] + [full(a) for a in consts],
        out_specs=[pl.BlockSpec((QVT_ROWS, tm), lambda i: (0, i))] + [row(w) for w in widths]
        + [pl.BlockSpec((1, B_HEADS, QK_PAD, tm), lambda i: (i // nt, 0, 0, i % nt)),
           pl.BlockSpec((1, B_HEADS, tm, QK_PAD), lambda i: (i // nt, 0, i % nt, 0)),
           pl.BlockSpec((1, B_HEADS, nc, V_ROWS, tk), lambda i: (i // nt, 0, i % nt, 0, 0))],
        out_shape=[jax.ShapeDtypeStruct((QVT_ROWS, t), BF16)]
        + [jax.ShapeDtypeStruct((t, w), BF16) for w in widths]
        + [jax.ShapeDtypeStruct((batch, B_HEADS, QK_PAD, seq), BF16),
           jax.ShapeDtypeStruct((batch, B_HEADS, seq, QK_PAD), BF16),
           jax.ShapeDtypeStruct((batch, B_HEADS, nt * nc, V_ROWS, tk), BF16)],
        compiler_params=pltpu.CompilerParams(
            dimension_semantics=("arbitrary",), vmem_limit_bytes=VMEM_LIMIT),
        name="proj",
    )(x2, pos_row, *consts)


def _band_bias_kernel(rbt_ref, onehot_ref, o_ref):
    rbt = rbt_ref[...]
    onehot = onehot_ref[...]
    t = jnp.zeros((A_HEADS, WINDOW), F32)
    for b in range(N_BUCKETS):
        t = t + rbt[:, b:b + 1] * onehot[b:b + 1, :]
    t = t * math.log2(math.e)
    band = 2 * A_BLOCK
    key = lax.broadcasted_iota(jnp.int32, (band, band), 0)
    neg = jnp.full((1, A_BLOCK), NEG, F32)
    for hd in range(A_HEADS):
        x = jnp.broadcast_to(jnp.concatenate([neg, t[hd:hd + 1, :]], axis=1), (band, band))
        for bit in range(8):
            x = jnp.where((key >> bit) & 1 == 1, pltpu.roll(x, 1 << bit, 1), x)
        g, hh = divmod(hd, A_GROUP)
        o_ref[g, :, hh * A_BLOCK:(hh + 1) * A_BLOCK] = x[:, :A_BLOCK]


def _band_bias(rel_bias):
    dist = np.arange(WINDOW)
    max_exact = N_BUCKETS // 2
    large = max_exact + (np.log(np.maximum(dist, 1).astype(np.float32) / max_exact)
                         / math.log(MAX_DISTANCE / max_exact) * (N_BUCKETS - max_exact)).astype(np.int32)
    bucket = np.where(dist < max_exact, dist, np.minimum(large, N_BUCKETS - 1))
    onehot = jnp.asarray(np.arange(N_BUCKETS)[:, None] == bucket[None, :], F32)
    return pl.pallas_call(
        _band_bias_kernel,
        out_shape=jax.ShapeDtypeStruct((A_KV_HEADS, 2 * A_BLOCK, A_GROUP * A_BLOCK), F32),
        name="band_bias",
    )(rel_bias.astype(F32).T, onehot)


def _swa_kernel(qv_ref, k_ref, kp_ref, vp_ref, bias_ref, sink_ref, o_ref, *, nsub):
    first = pl.program_id(1) == 0
    band = 2 * A_BLOCK
    width = A_GROUP * A_BLOCK
    pad = jnp.logical_and(first, lax.broadcasted_iota(jnp.int32, (band, width), 0) < A_BLOCK)
    zeros_q = jnp.zeros((A_HEAD_DIM, width), BF16)
    ones_rows = (lax.broadcasted_iota(jnp.int32, (SWA_V_ROWS - A_HEAD_DIM, band), 0) == 0).astype(BF16)

    def scores(sb, g):
        r0 = sb * A_BLOCK
        if sb == 0:
            kb = jnp.concatenate([kp_ref[0], k_ref[0, 0:A_BLOCK, :]], axis=0)
        else:
            kb = k_ref[0, r0 - A_BLOCK:r0 + A_BLOCK, :]
        tiles = [qv_ref[(g * A_GROUP + hh) * A_HEAD_DIM:(g * A_GROUP + hh + 1) * A_HEAD_DIM, r0:r0 + A_BLOCK]
                 for hh in range(A_GROUP)]
        qt = jnp.concatenate(tiles, axis=1)
        rhs = jnp.concatenate([qt, zeros_q] if g == 0 else [zeros_q, qt], axis=0)
        s = _dot(kb, rhs) + bias_ref[g]
        if sb == 0:
            s = jnp.where(pad, NEG, s)
        return s

    def finish(sb, g, s):
        r0 = sb * A_BLOCK
        sink = sink_ref[g]
        m = jnp.maximum(jnp.max(s, axis=0, keepdims=True), sink)
        e = jnp.exp2(s - m).astype(BF16)
        v0 = A_WIDTH + g * A_HEAD_DIM
        if sb == 0:
            vt = jnp.concatenate([vp_ref[g * A_HEAD_DIM:(g + 1) * A_HEAD_DIM, :],
                                  qv_ref[v0:v0 + A_HEAD_DIM, 0:A_BLOCK]], axis=1)
        else:
            vt = qv_ref[v0:v0 + A_HEAD_DIM, r0 - A_BLOCK:r0 + A_BLOCK]
        acc = _dot(jnp.concatenate([vt, ones_rows], axis=0), e)
        den = acc[A_HEAD_DIM:A_HEAD_DIM + 1, :] + jnp.exp2(sink - m)
        o = (acc[:A_HEAD_DIM, :] / den).astype(BF16)
        for hh in range(A_GROUP):
            hd = g * A_GROUP + hh
            o_ref[hd * A_HEAD_DIM:(hd + 1) * A_HEAD_DIM, r0:r0 + A_BLOCK] = o[:, hh * A_BLOCK:(hh + 1) * A_BLOCK]

    units = [(sb, g) for sb in range(nsub) for g in range(A_KV_HEADS)]
    s_next = scores(*units[0])
    for n, unit in enumerate(units):
        s_cur = s_next
        if n + 1 < len(units):
            s_next = scores(*units[n + 1])
        finish(*unit, s_cur)


def _swa(qvt, ka, bias, sinks, batch, seq, tq):
    nsub = tq // A_BLOCK
    nt = seq // tq
    vrow = A_WIDTH // A_KV_WIDTH
    return pl.pallas_call(
        functools.partial(_swa_kernel, nsub=nsub),
        grid=(batch, nt),
        in_specs=[pl.BlockSpec((QVT_ROWS, tq), lambda b, i: (0, b * nt + i)),
                  pl.BlockSpec((1, tq, A_KV_WIDTH), lambda b, i: (b, i, 0)),
                  pl.BlockSpec((1, A_BLOCK, A_KV_WIDTH), lambda b, i: (b, jnp.maximum(i * nsub - 1, 0), 0)),
                  pl.BlockSpec((A_KV_WIDTH, A_BLOCK),
                               lambda b, i: (vrow, jnp.maximum((b * nt + i) * nsub - 1, 0))),
                  pl.BlockSpec(bias.shape, lambda b, i: (0, 0, 0)),
                  pl.BlockSpec(sinks.shape, lambda b, i: (0, 0, 0))],
        out_specs=pl.BlockSpec((A_WIDTH, tq), lambda b, i: (0, b * nt + i)),
        out_shape=jax.ShapeDtypeStruct((A_WIDTH, batch * seq), BF16),
        compiler_params=pltpu.CompilerParams(
            dimension_semantics=("arbitrary", "arbitrary"), vmem_limit_bytes=VMEM_LIMIT),
        name="swa",
    )(qvt, ka, ka, qvt, bias, sinks)


def _flash_kernel(qt_ref, k_ref, vt_ref, o_ref, m_sc, acc_sc, s0_sc, s1_sc, bm0_sc, bm1_sc, *, tq, nh):
    tk = tq // 2
    i = pl.program_id(2)
    m_sc[...] = jnp.full(m_sc.shape, NEG, F32)
    acc_sc[...] = jnp.zeros(acc_sc.shape, F32)
    s_bufs = (s0_sc, s1_sc)
    bm_bufs = (bm0_sc, bm1_sc)

    def scores(h, j, slot, diag_half):
        kb = k_ref[0, h, pl.ds(pl.multiple_of(j * tk, tk), tk), :]
        s = _dot(kb, qt_ref[0, h])
        if diag_half is not None:
            key = lax.broadcasted_iota(jnp.int32, (tk, tq), 0) + diag_half * tk
            qry = lax.broadcasted_iota(jnp.int32, (tk, tq), 1)
            s = jnp.where(key <= qry, s, NEG)
        s_bufs[slot][h] = s
        bm_bufs[slot][h] = jnp.max(s, axis=0, keepdims=True)

    def softmax_pv(h, j, slot):
        m_old = m_sc[h]
        m_new = jnp.maximum(m_old, bm_bufs[slot][h])
        p = jnp.exp2(s_bufs[slot][h] - m_new).astype(BF16)
        alpha = jnp.exp2(m_old - m_new)
        m_sc[h] = m_new
        acc_sc[h] = acc_sc[h] * alpha + _dot(vt_ref[0, h, j], p)

    def stage(j, slot, next_diag_half, has_next=True):
        for h in range(nh):
            if has_next:
                scores(h, j + 1, 1 - slot, next_diag_half)
            softmax_pv(h, j, slot)

    @pl.when(i == 0)
    def _():
        for h in range(nh):
            scores(h, 0, 0, 0)

    @pl.when(i > 0)
    def _():
        for h in range(nh):
            scores(h, 0, 0, None)

    def pair(jj, carry):
        stage(2 * jj, 0, None)
        stage(2 * jj + 1, 1, None)
        return carry

    lax.fori_loop(0, i - 1, pair, 0)

    @pl.when(i > 0)
    def _():
        stage(2 * i - 2, 0, None)
        stage(2 * i - 1, 1, 0)

    stage(2 * i, 0, 1)
    stage(2 * i + 1, 1, None, has_next=False)

    for h in range(nh):
        acc = acc_sc[h]
        o_ref[h * B_VDIM:(h + 1) * B_VDIM, :] = (acc[:B_VDIM, :] / acc[B_VDIM:B_VDIM + 1, :]).astype(BF16)


def _flash(qt, k, vt, tq, nh):
    batch, heads, _, seq = qt.shape
    tk = tq // 2
    nq = seq // tq
    return pl.pallas_call(
        functools.partial(_flash_kernel, tq=tq, nh=nh),
        grid=(batch, heads // nh, nq),
        in_specs=[pl.BlockSpec((1, nh, QK_PAD, tq), lambda b, g, i: (b, g, 0, i)),
                  pl.BlockSpec((1, nh, seq, QK_PAD), lambda b, g, i: (b, g, 0, 0)),
                  pl.BlockSpec((1, nh, seq // tk, V_ROWS, tk), lambda b, g, i: (b, g, 0, 0, 0))],
        out_specs=pl.BlockSpec((nh * B_VDIM, tq), lambda b, g, i: (g, b * nq + i)),
        out_shape=jax.ShapeDtypeStruct((B_WIDTH, batch * seq), BF16),
        scratch_shapes=[pltpu.VMEM((nh, 1, tq), F32), pltpu.VMEM((nh, V_ROWS, tq), F32),
                        pltpu.VMEM((nh, tk, tq), F32), pltpu.VMEM((nh, tk, tq), F32),
                        pltpu.VMEM((nh, 1, tq), F32), pltpu.VMEM((nh, 1, tq), F32)],
        compiler_params=pltpu.CompilerParams(
            dimension_semantics=("arbitrary", "arbitrary", "arbitrary"),
            vmem_limit_bytes=VMEM_LIMIT),
        name="flash",
    )(qt, k, vt)


def _flash_bounded_kernel(kmax_ref, qt_ref, qtn_ref, k_ref, vt_ref, o_ref, mq_sc, mqn_sc, acc_sc, p0_sc, p1_sc,
                          *, tq, nh):
    tk = tq // 2
    i = pl.program_id(2)
    acc_sc[...] = jnp.zeros(acc_sc.shape, F32)

    def shift(q_ref, h):
        q = q_ref[0, h].astype(F32)
        return jnp.sqrt(jnp.sum(q * q, axis=0, keepdims=True)) * kmax_ref[0]

    @pl.when(i > 0)
    def _():
        mq_sc[...] = mqn_sc[...]

    p_bufs = (p0_sc, p1_sc)

    def probs(h, j, slot, diag_half, next_q=False):
        kb = k_ref[0, h, pl.ds(pl.multiple_of(j * tk, tk), tk), :]
        if diag_half == 1:
            s = _dot(kb, qt_ref[0, h, :, tk:])
            key = lax.broadcasted_iota(jnp.int32, (tk, tk), 0)
            qry = lax.broadcasted_iota(jnp.int32, (tk, tk), 1)
            s = jnp.where(key <= qry, s, NEG)
            p_bufs[slot][h, :, tk:] = jnp.exp2(s - mq_sc[h, :, tk:]).astype(BF16)
            return
        s = _dot(kb, (qtn_ref if next_q else qt_ref)[0, h])
        if diag_half is not None:
            key = lax.broadcasted_iota(jnp.int32, (tk, tq), 0)
            qry = lax.broadcasted_iota(jnp.int32, (tk, tq), 1)
            s = jnp.where(key <= qry, s, NEG)
        p_bufs[slot][h] = jnp.exp2(s - (mqn_sc if next_q else mq_sc)[h]).astype(BF16)

    def pv(h, j, slot, upper_half_only=False):
        if upper_half_only:
            acc_sc[h, :, tk:] += _dot(vt_ref[0, h, j], p_bufs[slot][h, :, tk:])
        else:
            acc_sc[h] += _dot(vt_ref[0, h, j], p_bufs[slot][h])

    def stage(j, slot, next_diag_half, next_q=False, upper_half_only=False):
        for h in range(nh):
            probs(h, 0 if next_q else j + 1, 1 - slot, next_diag_half, next_q)
            pv(h, j, slot, upper_half_only)

    def pair(jj, last_next_diag_half=None):
        stage(2 * jj, 0, None)
        stage(2 * jj + 1, 1, last_next_diag_half)

    @pl.when(i == 0)
    def _():
        for h in range(nh):
            mq_sc[h] = shift(qt_ref, h)
        for h in range(nh):
            probs(h, 0, 0, 0)

    plain = i - 1

    def two_pairs(n, carry):
        pair(2 * n)
        pair(2 * n + 1)
        return carry

    lax.fori_loop(0, plain // 2, two_pairs, 0)

    @pl.when(jnp.logical_and(plain > 0, plain % 2 == 1))
    def _():
        pair(plain - 1)

    @pl.when(i > 0)
    def _():
        pair(i - 1, 0)

    for h in range(nh):
        mqn_sc[h] = shift(qtn_ref, h)
    stage(2 * i, 0, 1)
    stage(2 * i + 1, 1, None, next_q=True, upper_half_only=True)

    for h in range(nh):
        acc = acc_sc[h]
        o_ref[h * B_VDIM:(h + 1) * B_VDIM, :] = (acc[:B_VDIM, :] / acc[B_VDIM:B_VDIM + 1, :]).astype(BF16)


def _flash_bounded(kmax, qt, k, vt, tq, nh):
    batch, heads, _, seq = qt.shape
    tk = tq // 2
    nq = seq // tq
    return pl.pallas_call(
        functools.partial(_flash_bounded_kernel, tq=tq, nh=nh),
        grid=(batch, heads // nh, nq),
        in_specs=[pl.BlockSpec(memory_space=pltpu.SMEM),
                  pl.BlockSpec((1, nh, QK_PAD, tq), lambda b, g, i: (b, g, 0, i)),
                  pl.BlockSpec((1, nh, QK_PAD, tq), lambda b, g, i: (b, g, 0, jnp.minimum(i + 1, nq - 1))),
                  pl.BlockSpec((1, nh, seq, QK_PAD), lambda b, g, i: (b, g, 0, 0)),
                  pl.BlockSpec((1, nh, seq // tk, V_ROWS, tk), lambda b, g, i: (b, g, 0, 0, 0))],
        out_specs=pl.BlockSpec((nh * B_VDIM, tq), lambda b, g, i: (g, b * nq + i)),
        out_shape=jax.ShapeDtypeStruct((B_WIDTH, batch * seq), BF16),
        scratch_shapes=[pltpu.VMEM((nh, 1, tq), F32), pltpu.VMEM((nh, 1, tq), F32),
                        pltpu.VMEM((nh, V_ROWS, tq), F32),
                        pltpu.VMEM((nh, tk, tq), BF16), pltpu.VMEM((nh, tk, tq), BF16)],
        compiler_params=pltpu.CompilerParams(
            dimension_semantics=("arbitrary", "arbitrary", "arbitrary"),
            vmem_limit_bytes=VMEM_LIMIT),
        name="flash_bounded",
    )(kmax, qt, qt, k, vt)


def _final_kernel(x_ref, p_ref, oat_ref, za_ref, obt_ref, zb_ref, ga_ref, gb_ref,
                  woa_ref, wob_ref, wout_ref, gpl_ref, wpg_ref, wpp_ref, gpost_ref, out_ref):
    rows = x_ref.shape[0] // FINAL_CHAINS
    groups = [slice(c * rows, (c + 1) * rows) for c in range(FINAL_CHAINS)]
    oa = [oat_ref[:, r].astype(F32).T for r in groups]
    ya = [_dot((oa[c] * jax.nn.silu(za_ref[r, :].astype(F32))).astype(BF16), woa_ref[...])
          for c, r in enumerate(groups)]
    yb = [_dot((obt_ref[:, r].astype(F32).T * jax.nn.silu(zb_ref[r, :].astype(F32))).astype(BF16), wob_ref[...])
          for r in groups]
    emb = [_row_rms(_dot(p_ref[r, :].astype(BF16), wpp_ref[...]), D_MODEL) * gpost_ref[...] for r in groups]
    merged = [jax.nn.sigmoid(ga_ref[r, :].astype(F32)) * ya[c] + jax.nn.sigmoid(gb_ref[r, :].astype(F32)) * yb[c]
              for c, r in enumerate(groups)]
    x1 = [x_ref[r, :] + _dot(merged[c].astype(BF16), wout_ref[...]) for c, r in enumerate(groups)]
    gate = [jax.nn.sigmoid(_dot((_row_rms(x1[c], D_MODEL) * gpl_ref[...]).astype(BF16), wpg_ref[...]))
            for c in range(FINAL_CHAINS)]
    for c, r in enumerate(groups):
        out_ref[r, :] = x1[c] + gate[c] * emb[c]


def _final(x2, p2, oat, za, ob, zb, ga, gb, woa, wob, wout, gpl, wpg, wpp, gpost, tm):
    t = x2.shape[0]
    row = lambda a: pl.BlockSpec((tm, a.shape[1]), lambda i: (i, 0))
    full = lambda a: pl.BlockSpec(a.shape, lambda i: (0,) * a.ndim)
    acts = (x2, p2, oat, za, ob, zb, ga, gb)
    consts = (woa, wob, wout, gpl, wpg, wpp, gpost)
    act_specs = [row(a) for a in acts]
    act_specs[2] = pl.BlockSpec((A_WIDTH, tm), lambda i: (0, i))
    act_specs[4] = pl.BlockSpec((B_WIDTH, tm), lambda i: (0, i))
    return pl.pallas_call(
        _final_kernel,
        grid=(t // tm,),
        in_specs=act_specs + [full(a) for a in consts],
        out_specs=pl.BlockSpec((tm, D_MODEL), lambda i: (i, 0)),
        out_shape=jax.ShapeDtypeStruct((t, D_MODEL), F32),
        compiler_params=pltpu.CompilerParams(
            dimension_semantics=("arbitrary",), vmem_limit_bytes=VMEM_LIMIT),
        name="final",
    )(*acts, *consts)


def kernel(x, p, positions, norm_g, w_in, a_q_norm, a_k_norm, a_sinks, rel_bias, w_o_a, b_cq_norm, w_uq, b_ckv_norm, w_uk, w_uv, b_q_norm, b_k_norm, b_kr_norm, w_o_b, w_out, ple_norm_g, w_ple_gate, w_ple_proj, ple_post_g):
    batch, seq, _ = x.shape
    depth = p.shape[0]
    t = batch * seq
    tm = TOKEN_TILE
    offs = np.concatenate([[0], np.cumsum(SPLIT_SIZES)])
    log2e = math.log2(math.e)
    inv_freq = (ROPE_THETA ** (-jnp.arange(0, B_ROPE, 2, dtype=F32) / B_ROPE))[:, None]
    pos_row = positions.reshape(1, t)
    band_bias = _band_bias(rel_bias)
    row = lambda v: v.astype(F32)[None, :]
    col = lambda v: v.astype(F32)[:, None]

    x2 = x.reshape(t, D_MODEL)
    for i in range(depth):
        w = w_in[i]
        cols = [w[:, offs[j]:offs[j + 1]] for j in range(len(SPLIT_SIZES))]
        wt = jnp.concatenate([cols[0], cols[2], cols[1], cols[6]], axis=1).T.astype(BF16)
        w1 = jnp.concatenate([cols[j] for j in (3, 4, 5, 7, 8, 9)], axis=1).astype(BF16)
        gqn = col(b_q_norm[i]) * (B_QK_DIM ** -0.5 * log2e)
        consts = (row(norm_g[i]), wt, w1,
                  col(a_q_norm[i]) * (A_HEAD_DIM ** -0.5 * log2e), col(a_k_norm[i]),
                  row(b_cq_norm[i]), w_uq[i].T.astype(BF16), gqn,
                  row(b_ckv_norm[i]), w_uk[i].T.astype(BF16), w_uv[i].T.astype(BF16),
                  col(b_k_norm[i]), col(b_kr_norm[i]), inv_freq)
        qvt, ka, za, zb, ga, gb, qt, kb, vt = _proj(x2, pos_row, consts, batch, seq, tm, FLASH_Q_TILE // 2)

        sinks = jnp.repeat(a_sinks[i].astype(F32) * log2e, A_BLOCK).reshape(A_KV_HEADS, 1, A_GROUP * A_BLOCK)
        oat = _swa(qvt, ka.reshape(batch, seq, A_KV_WIDTH), band_bias, sinks, batch, seq, tm)
        q_bound = math.sqrt(B_QK_DIM) * jnp.max(jnp.abs(gqn))
        k_bound = jnp.sqrt(B_NOPE * jnp.max(jnp.square(b_k_norm[i].astype(F32)))
                           + B_ROPE * jnp.max(jnp.square(b_kr_norm[i].astype(F32))))
        ob = lax.cond(q_bound * k_bound < MAX_SAFE_SHIFT,
                      lambda: _flash_bounded(k_bound.reshape(1), qt, kb, vt, FLASH_Q_TILE, FLASH_HEADS),
                      lambda: _flash(qt, kb, vt, FLASH_Q_TILE, FLASH_HEADS))

        x2 = _final(x2, p[i].reshape(t, PLE_DIM), oat, za,
                    ob, zb, ga, gb,
                    w_o_a[i].astype(BF16), w_o_b[i].astype(BF16), w_out[i].astype(BF16),
                    row(ple_norm_g[i]), w_ple_gate[i].astype(BF16), w_ple_proj[i].astype(BF16),
                    row(ple_post_g[i]), tm)
    return x2.reshape(batch, seq, D_MODEL)
```
